```python
import math
import jax, jax.numpy as jnp
from jax import lax
import numpy as np

D_MODEL = 1024
BATCH = 4
SEQ = 4096
DEPTH = 4

CHUNK = 64
Q_BLOCK = 128
MEM_LEN = 256
FOX_HEAD_DIM = 64
FOX_WIDTH = 3 * D_MODEL // 8
FOX_HEADS = FOX_WIDTH // FOX_HEAD_DIM
MLSTM_HEADS = 4
MLSTM_WIDTH = 3 * D_MODEL // 8
MLSTM_HEAD_DIM = MLSTM_WIDTH // MLSTM_HEADS
MEM_HEADS = 4
MEM_WIDTH = D_MODEL // 4
MEM_HEAD_DIM = MEM_WIDTH // MEM_HEADS
MIX_WIDTH = FOX_WIDTH + MLSTM_WIDTH + MEM_WIDTH
CONV_WIDTH = 4
LN_EPS = 1e-5
DEEPNORM_ALPHA = (2.0 * DEPTH) ** 0.25
DEEPNORM_BETA = (8.0 * DEPTH) ** -0.25
IN_SPLITS = (FOX_WIDTH, FOX_WIDTH, FOX_WIDTH, FOX_HEADS, FOX_WIDTH,
             MLSTM_WIDTH, MLSTM_WIDTH, MLSTM_WIDTH, MLSTM_HEADS, MLSTM_HEADS, MLSTM_WIDTH, MLSTM_WIDTH,
             MEM_WIDTH, MEM_WIDTH)
IN_COLS = sum(IN_SPLITS)

kernel_name = "fox_mlstm_memory_hybrid_deepnorm"


def layer_norm(x, g, b):
    xf = x.astype(jnp.float32)
    mu = jnp.mean(xf, axis=-1, keepdims=True)
    var = jnp.mean(jnp.square(xf - mu), axis=-1, keepdims=True)
    return ((xf - mu) * lax.rsqrt(var + LN_EPS) * g + b).astype(x.dtype)


def split_heads(t, n_heads):
    B, S, W = t.shape
    return t.reshape(B, S, n_heads, W // n_heads).transpose(0, 2, 1, 3)


def forgetting_attention(q, k, v, f_pre):
    B, S, _ = q.shape
    dt = q.dtype
    qh = split_heads(q.astype(jnp.float32), FOX_HEADS) * (FOX_HEAD_DIM ** -0.5)
    kh = split_heads(k.astype(jnp.float32), FOX_HEADS)
    vh = split_heads(v.astype(jnp.float32), FOX_HEADS)
    c = jnp.cumsum(jax.nn.log_sigmoid(f_pre.astype(jnp.float32)), axis=1).transpose(0, 2, 1)
    nb = S // Q_BLOCK
    qb = qh.reshape(B, FOX_HEADS, nb, Q_BLOCK, FOX_HEAD_DIM).transpose(2, 0, 1, 3, 4)
    cb = c.reshape(B, FOX_HEADS, nb, Q_BLOCK).transpose(2, 0, 1, 3)
    starts = jnp.arange(nb, dtype=jnp.int32) * Q_BLOCK
    key_pos = jnp.arange(S, dtype=jnp.int32)

    def block(args):
        q_blk, c_blk, start = args
        logits = (jnp.einsum('bhqd,bhkd->bhqk', q_blk, kh)
                  + c_blk[..., :, None] - c[..., None, :])
        q_pos = start + jnp.arange(Q_BLOCK, dtype=jnp.int32)
        causal = key_pos[None, :] <= q_pos[:, None]
        p = jax.nn.softmax(jnp.where(causal, logits, -jnp.inf), axis=-1)
        return jnp.einsum('bhqk,bhkd->bhqd', p, vh)

    out = lax.map(block, (qb, cb, starts))
    return out.transpose(1, 0, 3, 2, 4).reshape(B, S, FOX_WIDTH).astype(dt)


def causal_depthwise_conv(u, w, b):
    C = u.shape[-1]
    y = lax.conv_general_dilated(u, w[:, None, :].astype(u.dtype), window_strides=(1,),
                                 padding=((CONV_WIDTH - 1, 0),),
                                 dimension_numbers=('NWC', 'WIO', 'NWC'),
                                 feature_group_count=C)
    return y + b


def mlstm(q, k, v, i_pre, f_pre, o_pre, norm_g):
    B, S, _ = q.shape
    dt = q.dtype
    H, dh = MLSTM_HEADS, MLSTM_HEAD_DIM
    qh = split_heads(q.astype(jnp.float32), H)
    kh = split_heads(k.astype(jnp.float32), H) * (dh ** -0.5)
    vh = split_heads(v.astype(jnp.float32), H)
    log_i = i_pre.astype(jnp.float32).transpose(0, 2, 1)
    log_f = jax.nn.log_sigmoid(f_pre.astype(jnp.float32)).transpose(0, 2, 1)
    nc = S // CHUNK

    def chunks(t):
        return jnp.moveaxis(t.reshape(B, H, nc, CHUNK, *t.shape[3:]), 2, 0)

    tril = jnp.tril(jnp.ones((CHUNK, CHUNK), dtype=bool))

    def step(carry, xs):
        C, n, m = carry
        qc, kc, vc, ic, fc = xs
        b = jnp.cumsum(fc, axis=-1)
        log_d = jnp.where(tril, b[..., :, None] - b[..., None, :] + ic[..., None, :], -jnp.inf)
        inter = b + m[..., None]
        m_row = jnp.maximum(inter, jnp.max(log_d, axis=-1))
        s = jnp.einsum('bhtd,bhsd->bhts', qc, kc) * jnp.exp(log_d - m_row[..., None])
        dec = jnp.exp(inter - m_row)
        num = (jnp.einsum('bhts,bhse->bhte', s, vc)
               + dec[..., None] * jnp.einsum('bhtd,bhed->bhte', qc, C))
        den = jnp.sum(s, axis=-1) + dec * jnp.einsum('bhtd,bhd->bht', qc, n)
        h = num / jnp.maximum(jnp.abs(den), jnp.exp(-m_row))[..., None]
        b_last = b[..., -1]
        log_w = b_last[..., None] - b + ic
        m_new = jnp.maximum(b_last + m, jnp.max(log_w, axis=-1))
        w = jnp.exp(log_w - m_new[..., None])
        carry_dec = jnp.exp(b_last + m - m_new)
        C_new = carry_dec[..., None, None] * C + jnp.einsum('bhs,bhse,bhsd->bhed', w, vc, kc)
        n_new = carry_dec[..., None] * n + jnp.einsum('bhs,bhsd->bhd', w, kc)
        return (C_new, n_new, m_new), h

    init = (jnp.zeros((B, H, dh, dh), jnp.float32), jnp.zeros((B, H, dh), jnp.float32),
            jnp.zeros((B, H), jnp.float32))
    _, h = lax.scan(step, init, (chunks(qh), chunks(kh), chunks(vh), chunks(log_i), chunks(log_f)))
    h = jnp.moveaxis(h, 0, 2).reshape(B, H, S, dh).transpose(0, 2, 1, 3)
    h = jax.nn.sigmoid(o_pre.astype(jnp.float32)).reshape(B, S, H, dh) * h
    mu = jnp.mean(h, axis=-1, keepdims=True)
    var = jnp.mean(jnp.square(h - mu), axis=-1, keepdims=True)
    h = ((h - mu) * lax.rsqrt(var + LN_EPS)).reshape(B, S, MLSTM_WIDTH) * norm_g
    return h.astype(dt)


def memory_attention(q, mem_k, mem_v):
    B, S, _ = q.shape
    dt = q.dtype
    qh = split_heads(q.astype(jnp.float32), MEM_HEADS) * (MEM_HEAD_DIM ** -0.5)
    kh = split_heads(mem_k.astype(jnp.float32), MEM_HEADS)
    vh = split_heads(mem_v.astype(jnp.float32), MEM_HEADS)
    p = jax.nn.softmax(jnp.einsum('bhsd,bhmd->bhsm', qh, kh), axis=-1)
    out = jnp.einsum('bhsm,bhmd->bhsd', p, vh)
    return out.transpose(0, 2, 1, 3).reshape(B, S, MEM_WIDTH).astype(dt)


def hybrid_layer(x, mem, w_in, fox_f_bias, conv_w, conv_b, i_bias, f_bias, norm_g,
                 w_mem_kv, w_out, ln_g, ln_b):
    u = x @ w_in
    split_points = np.cumsum(IN_SPLITS)[:-1].tolist()
    (fq, fk, fv, ff, fz, mq, mk, mv, mi, mf, mo, mz, rq, rz) = jnp.split(u, split_points, axis=-1)
    y_fox = forgetting_attention(fq, fk, fv, ff + fox_f_bias) * jax.nn.silu(fz)
    qk = jax.nn.silu(causal_depthwise_conv(jnp.concatenate([mq, mk], axis=-1), conv_w, conv_b))
    mq_c, mk_c = jnp.split(qk, [MLSTM_WIDTH], axis=-1)
    y_ml = mlstm(mq_c, mk_c, mv, mi + i_bias, mf + f_bias, mo, norm_g) * jax.nn.silu(mz)
    mem_k, mem_v = jnp.split(mem @ w_mem_kv, [MEM_WIDTH], axis=-1)
    y_mem = memory_attention(rq, mem_k, mem_v) * jax.nn.silu(rz)
    y = jnp.concatenate([y_fox, y_ml, y_mem], axis=-1) @ w_out
    return layer_norm(DEEPNORM_ALPHA * x + y, ln_g, ln_b)


def setup_inputs(seed: int = 0) -> dict:
    key = jax.random.key(seed)
    ks = jax.random.split(key, 16)
    f32 = jnp.float32
    x = jax.random.normal(ks[0], (BATCH, SEQ, D_MODEL), f32)
    mem = jax.random.normal(ks[1], (BATCH, MEM_LEN, D_MODEL), f32)
    w_in = jax.random.normal(ks[2], (DEPTH, D_MODEL, IN_COLS), f32) * D_MODEL ** -0.5
    fox_f_bias = (jnp.linspace(1.0, 6.0, FOX_HEADS, dtype=f32)[None, :]
                  + 0.1 * jax.random.normal(ks[3], (DEPTH, FOX_HEADS), f32))
    mlstm_conv_w = jax.random.normal(ks[4], (DEPTH, CONV_WIDTH, 2 * MLSTM_WIDTH), f32) * CONV_WIDTH ** -0.5
    mlstm_conv_b = 0.01 * jax.random.normal(ks[5], (DEPTH, 2 * MLSTM_WIDTH), f32)
    mlstm_i_bias = 0.1 * jax.random.normal(ks[6], (DEPTH, MLSTM_HEADS), f32)
    mlstm_f_bias = (jnp.linspace(3.0, 6.0, MLSTM_HEADS, dtype=f32)[None, :]
                    + 0.1 * jax.random.normal(ks[7], (DEPTH, MLSTM_HEADS), f32))
    mlstm_norm_g = 1.0 + 0.02 * jax.random.normal(ks[8], (DEPTH, MLSTM_WIDTH), f32)
    w_mem_kv = jax.random.normal(ks[9], (DEPTH, D_MODEL, 2 * MEM_WIDTH), f32) * D_MODEL ** -0.5
    w_out = (jax.random.normal(ks[10], (DEPTH, MIX_WIDTH, D_MODEL), f32)
             * (MIX_WIDTH ** -0.5) * DEEPNORM_BETA)
    ln_g = 1.0 + 0.02 * jax.random.normal(ks[11], (DEPTH, D_MODEL), f32)
    ln_b = 0.02 * jax.random.normal(ks[12], (DEPTH, D_MODEL), f32)
    return {"x": x, "mem": mem, "w_in": w_in, "fox_f_bias": fox_f_bias,
            "mlstm_conv_w": mlstm_conv_w, "mlstm_conv_b": mlstm_conv_b,
            "mlstm_i_bias": mlstm_i_bias, "mlstm_f_bias": mlstm_f_bias,
            "mlstm_norm_g": mlstm_norm_g, "w_mem_kv": w_mem_kv, "w_out": w_out,
            "ln_g": ln_g, "ln_b": ln_b}


def reference(x, mem, w_in, fox_f_bias, mlstm_conv_w, mlstm_conv_b, mlstm_i_bias, mlstm_f_bias,
              mlstm_norm_g, w_mem_kv, w_out, ln_g, ln_b):
    for l in range(DEPTH):
        x = hybrid_layer(x, mem, w_in[l], fox_f_bias[l], mlstm_conv_w[l], mlstm_conv_b[l],
                         mlstm_i_bias[l], mlstm_f_bias[l], mlstm_norm_g[l], w_mem_kv[l],
                         w_out[l], ln_g[l], ln_b[l])
    return x
```

```python
import functools
import math

import jax
import jax.numpy as jnp
from jax import lax
from jax.experimental import pallas as pl
from jax.experimental.pallas import tpu as pltpu

F32 = jnp.float32
BF16 = jnp.bfloat16

D_MODEL = 1024
DEPTH = 4
FOX_HEADS = 6
FOX_HEAD_DIM = 64
FOX_WIDTH = FOX_HEADS * FOX_HEAD_DIM
FOX_PAIRS = FOX_HEADS // 2
ML_HEADS = 4
ML_HEAD_DIM = 96
ML_WIDTH = ML_HEADS * ML_HEAD_DIM
MEM_HEADS = 4
MEM_HEAD_DIM = 64
MEM_WIDTH = MEM_HEADS * MEM_HEAD_DIM
CONV_WIDTH = 4
LN_EPS = 1e-5
DEEPNORM_ALPHA = (2.0 * DEPTH) ** 0.25

LANES = 128
SUBLANES = 8
ML_PAD = LANES
ML_PW = ML_HEADS * ML_PAD
DEN_LANE = ML_HEAD_DIM

COL_MQ = 0
COL_MK = COL_MQ + ML_PW
COL_MV = COL_MK + ML_PW
COL_MO = COL_MV + ML_PW
COL_MZ = COL_MO + ML_PW
COL_FQ = COL_MZ + ML_PW
COL_FK = COL_FQ + FOX_WIDTH
COL_FV = COL_FK + FOX_WIDTH
COL_FZ = COL_FV + FOX_WIDTH
COL_RQ = COL_FZ + FOX_WIDTH
COL_RZ = COL_RQ + MEM_WIDTH
U_COLS = COL_RZ + MEM_WIDTH
G_COLS = LANES
GATE_I = 24
GATE_F = 32
GATE_ROWS = 40

ML_CHUNK = 256
FOX_TQ = 256
FOX_TK = 256
PROJ_TM = 256
OUT_TM = 256
PROJ_TN = 512

_SRC = {}
_off = 0
for _name, _w in (("fq", FOX_WIDTH), ("fk", FOX_WIDTH), ("fv", FOX_WIDTH), ("ff", FOX_HEADS),
                  ("fz", FOX_WIDTH), ("mq", ML_WIDTH), ("mk", ML_WIDTH), ("mv", ML_WIDTH),
                  ("mi", ML_HEADS), ("mf", ML_HEADS), ("mo", ML_WIDTH), ("mz", ML_WIDTH),
                  ("rq", MEM_WIDTH), ("rz", MEM_WIDTH)):
    _SRC[_name] = (_off, _w)
    _off += _w


def _fox_gate_lane(h):
    return SUBLANES * (h // 2) + (h % 2)


def _sigmoid(x):
    return 1.0 / (1.0 + jnp.exp(-x))


def _silu(x):
    return x * _sigmoid(x)


def _pad_heads(w, axis):
    shp = list(w.shape)
    w = w.reshape(shp[:axis] + [ML_HEADS, ML_HEAD_DIM] + shp[axis + 1:])
    pad = [(0, 0)] * w.ndim
    pad[axis + 1] = (0, ML_PAD - ML_HEAD_DIM)
    w = jnp.pad(w, pad)
    return w.reshape(shp[:axis] + [ML_PW] + shp[axis + 1:])


def _take(w, name):
    o, n = _SRC[name]
    return w[:, o:o + n]


def _relayout_w_in(w):
    cols = [_pad_heads(_take(w, n), 1) for n in ("mq", "mk", "mv", "mo", "mz")]
    cols += [_take(w, n) for n in ("fq", "fk", "fv", "fz", "rq", "rz")]
    gates = jnp.zeros((D_MODEL, G_COLS), w.dtype)
    ff = _take(w, "ff")
    for h in range(FOX_HEADS):
        gates = gates.at[:, _fox_gate_lane(h)].set(ff[:, h])
    gates = gates.at[:, GATE_I:GATE_I + ML_HEADS].set(_take(w, "mi"))
    gates = gates.at[:, GATE_F:GATE_F + ML_HEADS].set(_take(w, "mf"))
    return jnp.concatenate(cols + [gates], axis=1).astype(BF16)


def _gate_bias_row(fox_f_bias, i_bias, f_bias):
    row = jnp.zeros((1, G_COLS), F32)
    for h in range(FOX_HEADS):
        row = row.at[0, _fox_gate_lane(h)].set(fox_f_bias[h])
    row = row.at[0, GATE_I:GATE_I + ML_HEADS].set(i_bias)
    row = row.at[0, GATE_F:GATE_F + ML_HEADS].set(f_bias)
    return row


def _relayout_w_out(w):
    fox = w[:FOX_WIDTH]
    ml = _pad_heads(w[FOX_WIDTH:FOX_WIDTH + ML_WIDTH], 0)
    mem = w[FOX_WIDTH + ML_WIDTH:]
    return jnp.concatenate([fox, ml, mem], axis=0).astype(BF16)


def _proj_kernel(x_ref, w_ref, u_ref, g_ref):
    xb = x_ref[...].astype(BF16)
    for c0 in range(0, U_COLS, PROJ_TN):
        u_ref[:, c0:c0 + PROJ_TN] = jnp.dot(
            xb, w_ref[:, c0:c0 + PROJ_TN], preferred_element_type=F32).astype(BF16)
    g_ref[...] = jnp.dot(xb, w_ref[:, U_COLS:U_COLS + G_COLS], preferred_element_type=F32)


def _project(x2d, w_re):
    n = x2d.shape[0]
    return pl.pallas_call(
        _proj_kernel,
        grid=(n // PROJ_TM,),
        in_specs=[pl.BlockSpec((PROJ_TM, D_MODEL), lambda i: (i, 0)),
                  pl.BlockSpec((D_MODEL, U_COLS + G_COLS), lambda i: (0, 0))],
        out_specs=[pl.BlockSpec((PROJ_TM, U_COLS), lambda i: (i, 0)),
                   pl.BlockSpec((PROJ_TM, G_COLS), lambda i: (i, 0))],
        out_shape=[jax.ShapeDtypeStruct((n, U_COLS), BF16),
                   jax.ShapeDtypeStruct((n, G_COLS), F32)],
        compiler_params=pltpu.CompilerParams(
            dimension_semantics=("arbitrary",), vmem_limit_bytes=56 * 1024 * 1024),
        name="in_proj",
    )(x2d, w_re)


def _gates_kernel(g_ref, bias_ref, cc_ref, ct_ref, carry_ref):
    @pl.when(pl.program_id(1) == 0)
    def _():
        carry_ref[...] = jnp.zeros_like(carry_ref)

    L = ML_CHUNK
    x = g_ref[...] + bias_ref[...]
    log_sig = jnp.minimum(x, 0.0) - jnp.log1p(jnp.exp(-jnp.abs(x)))
    r = lax.broadcasted_iota(jnp.int32, (L, L), 0)
    c = lax.broadcasted_iota(jnp.int32, (L, L), 1)
    tri = (c <= r).astype(F32)
    local = jnp.dot(tri, log_sig, precision=lax.Precision.HIGHEST, preferred_element_type=F32)
    lane = lax.broadcasted_iota(jnp.int32, (L, G_COLS), 1)
    carry = carry_ref[0:1, :]
    out = jnp.where(lane < GATE_I, local + carry, jnp.where(lane < GATE_F, x, local))
    carry_ref[0:1, :] = carry + local[L - 1:L, :]
    cc_ref[...] = out
    ct_ref[...] = out.T[:GATE_ROWS, :]


def _gates(g, bias_row):
    b, s, _ = g.shape
    return pl.pallas_call(
        _gates_kernel,
        grid=(b, s // ML_CHUNK),
        in_specs=[pl.BlockSpec((None, ML_CHUNK, G_COLS), lambda i, j: (i, j, 0)),
                  pl.BlockSpec((1, G_COLS), lambda i, j: (0, 0))],
        out_specs=[pl.BlockSpec((None, ML_CHUNK, G_COLS), lambda i, j: (i, j, 0)),
                   pl.BlockSpec((None, GATE_ROWS, ML_CHUNK), lambda i, j: (i, 0, j))],
        out_shape=[jax.ShapeDtypeStruct((b, s, G_COLS), F32),
                   jax.ShapeDtypeStruct((b, GATE_ROWS, s), F32)],
        scratch_shapes=[pltpu.VMEM((SUBLANES, G_COLS), F32)],
        compiler_params=pltpu.CompilerParams(dimension_semantics=("arbitrary", "arbitrary")),
        name="gate_prefix",
    )(g, bias_row)


def _fox_kernel(q_ref, k_ref, v_ref, z_ref, cc_ref, ct_ref, o_ref, m_sc, l_sc, acc_sc):
    pair = pl.program_id(1)
    qi = pl.program_id(2)
    tq, tk = FOX_TQ, FOX_TK
    lane = lax.broadcasted_iota(jnp.int32, (1, LANES), 1)
    head_masks = [(lane >= FOX_HEAD_DIM * hh) & (lane < FOX_HEAD_DIM * (hh + 1)) for hh in range(2)]
    q = q_ref[...]
    scale = jnp.asarray(FOX_HEAD_DIM ** -0.5, BF16)
    qh = [jnp.where(head_masks[hh], q, jnp.zeros_like(q)) * scale for hh in range(2)]
    cc = cc_ref[...]
    glane = lax.broadcasted_iota(jnp.int32, (tq, G_COLS), 1)
    cq = [jnp.sum(jnp.where(glane == SUBLANES * pair + hh, cc, 0.0), axis=1, keepdims=True)
          for hh in range(2)]

    m_sc[...] = jnp.full_like(m_sc, -jnp.inf)
    l_sc[...] = jnp.zeros_like(l_sc)
    acc_sc[...] = jnp.zeros_like(acc_sc)

    def step(j, masked):
        k0 = pl.multiple_of(j * tk, tk)
        kb = k_ref[pl.ds(k0, tk), :]
        vb = v_ref[pl.ds(k0, tk), :]
        for hh in range(2):
            s = lax.dot_general(qh[hh], kb, (((1,), (1,)), ((), ())), preferred_element_type=F32)
            ck = ct_ref[hh:hh + 1, pl.ds(k0, tk)]
            s = s + (cq[hh] - ck)
            if masked:
                row = lax.broadcasted_iota(jnp.int32, (tq, tk), 0)
                col = lax.broadcasted_iota(jnp.int32, (tq, tk), 1)
                s = jnp.where(col <= row, s, -jnp.inf)
            m_old = m_sc[hh]
            m_new = jnp.maximum(m_old, jnp.max(s, axis=1, keepdims=True))
            alpha = jnp.exp(m_old - m_new)
            p = jnp.exp(s - m_new)
            l_sc[hh] = alpha * l_sc[hh] + jnp.sum(p, axis=1, keepdims=True)
            vh = jnp.where(head_masks[hh], vb, jnp.zeros_like(vb))
            acc_sc[hh] = alpha * acc_sc[hh] + jnp.dot(p.astype(BF16), vh, preferred_element_type=F32)
            m_sc[hh] = m_new

    def body(j, carry):
        step(j, False)
        return carry

    lax.fori_loop(0, qi, body, 0)
    step(qi, True)

    out = acc_sc[0] / l_sc[0] + acc_sc[1] / l_sc[1]
    z = z_ref[...].astype(F32)
    o_ref[...] = (out * _silu(z)).astype(BF16)


def _fox(u, cc, ct):
    b, s, _ = u.shape
    nq = s // FOX_TQ
    cb = LANES
    return pl.pallas_call(
        _fox_kernel,
        grid=(b, FOX_PAIRS, nq),
        in_specs=[
            pl.BlockSpec((None, FOX_TQ, cb), lambda i, p, j: (i, j, COL_FQ // cb + p)),
            pl.BlockSpec((None, s, cb), lambda i, p, j: (i, 0, COL_FK // cb + p)),
            pl.BlockSpec((None, s, cb), lambda i, p, j: (i, 0, COL_FV // cb + p)),
            pl.BlockSpec((None, FOX_TQ, cb), lambda i, p, j: (i, j, COL_FZ // cb + p)),
            pl.BlockSpec((None, FOX_TQ, G_COLS), lambda i, p, j: (i, j, 0)),
            pl.BlockSpec((None, SUBLANES, s), lambda i, p, j: (i, p, 0)),
        ],
        out_specs=pl.BlockSpec((None, FOX_TQ, cb), lambda i, p, j: (i, j, p)),
        out_shape=jax.ShapeDtypeStruct((b, s, FOX_WIDTH), BF16),
        scratch_shapes=[pltpu.VMEM((2, FOX_TQ, 1), F32),
                        pltpu.VMEM((2, FOX_TQ, 1), F32),
                        pltpu.VMEM((2, FOX_TQ, LANES), F32)],
        compiler_params=pltpu.CompilerParams(
            dimension_semantics=("arbitrary", "arbitrary", "arbitrary")),
        name="fox_attn",
    )(u, u, u, u, cc, ct)


def _mlstm_kernel(qk_ref, v_ref, og_ref, z_ref, cc_ref, ct_ref, cw_ref, cb_ref, ng_ref,
                  y_ref, tail_sc, c_sc, m_sc):
    L = ML_CHUNK

    @pl.when(pl.program_id(1) == 0)
    def _():
        tail_sc[...] = jnp.zeros_like(tail_sc)
        c_sc[...] = jnp.zeros_like(c_sc)
        m_sc[...] = jnp.zeros_like(m_sc)

    u = qk_ref[...].astype(F32)
    tail = tail_sc[...]
    w = cw_ref[...]
    row8 = lax.broadcasted_iota(jnp.int32, (SUBLANES, 2 * ML_PW), 0)
    y = u * w[CONV_WIDTH - 1:CONV_WIDTH, :] + cb_ref[...]
    for shift in range(1, CONV_WIDTH):
        r = pltpu.roll(u, shift, axis=0)
        rt = pltpu.roll(tail, shift, axis=0)
        first = jnp.where(row8 < shift, rt, r[:SUBLANES])
        xs = jnp.concatenate([first, r[SUBLANES:]], axis=0)
        y = y + xs * w[CONV_WIDTH - 1 - shift:CONV_WIDTH - shift, :]
    tail_sc[...] = u[L - SUBLANES:, :]
    a = _silu(y)

    cc = cc_ref[...]
    lane = lax.broadcasted_iota(jnp.int32, (1, LANES), 1)
    real = lane < ML_HEAD_DIM
    rr = lax.broadcasted_iota(jnp.int32, (L, L), 0)
    rc = lax.broadcasted_iota(jnp.int32, (L, L), 1)
    tril = rc <= rr
    kscale = ML_HEAD_DIM ** -0.5

    for h in range(ML_HEADS):
        sl = slice(h * ML_PAD, (h + 1) * ML_PAD)
        q = a[:, sl].astype(BF16)
        k = (a[:, ML_PW + h * ML_PAD:ML_PW + (h + 1) * ML_PAD] * kscale).astype(BF16)
        v = v_ref[:, sl]
        v_aug = jnp.where(lane == DEN_LANE, jnp.ones_like(v), v)
        bcol = cc[:, GATE_F + h:GATE_F + h + 1]
        icol = cc[:, GATE_I + h:GATE_I + h + 1]
        brow = ct_ref[GATE_F + h:GATE_F + h + 1, :]
        irow = ct_ref[GATE_I + h:GATE_I + h + 1, :]
        m_prev = m_sc[h, 0:1, 0:1]

        log_d = jnp.where(tril, bcol + (irow - brow), -jnp.inf)
        inter = bcol + m_prev
        m_row = jnp.maximum(inter, jnp.max(log_d, axis=1, keepdims=True))
        dmat = jnp.exp(log_d - m_row)
        s = lax.dot_general(q, k, (((1,), (1,)), ((), ())), preferred_element_type=F32) * dmat
        dec = jnp.exp(inter - m_row)
        c_prev = c_sc[h]
        num = (jnp.dot(s.astype(BF16), v_aug, preferred_element_type=F32)
               + dec * jnp.dot(q, c_prev.astype(BF16), preferred_element_type=F32))
        den = jnp.sum(jnp.where(lane == DEN_LANE, num, 0.0), axis=1, keepdims=True)
        hval = num / jnp.maximum(jnp.abs(den), jnp.exp(-m_row))

        b_last = bcol[L - 1:L, :]
        log_w = b_last - bcol + icol
        m_new = jnp.maximum(b_last + m_prev, jnp.max(log_w, axis=0, keepdims=True))
        wgt = jnp.exp(log_w - m_new)
        cdec = jnp.exp(b_last + m_prev - m_new)
        wv = (wgt * v_aug.astype(F32)).astype(BF16)
        c_sc[h] = cdec * c_prev + lax.dot_general(
            k, wv, (((0,), (0,)), ((), ())), preferred_element_type=F32)
        m_sc[h] = jnp.broadcast_to(m_new, (SUBLANES, LANES))

        hv = jnp.where(real, _sigmoid(og_ref[:, sl].astype(F32)) * hval, 0.0)
        mu = jnp.sum(hv, axis=1, keepdims=True) * (1.0 / ML_HEAD_DIM)
        d = jnp.where(real, hv - mu, 0.0)
        var = jnp.sum(d * d, axis=1, keepdims=True) * (1.0 / ML_HEAD_DIM)
        hn = d * lax.rsqrt(var + LN_EPS) * ng_ref[:, sl]
        y_ref[:, sl] = (hn * _silu(z_ref[:, sl].astype(F32))).astype(BF16)


def _mlstm(u, cc, ct, conv_w, conv_b, norm_g):
    b, s, _ = u.shape
    L = ML_CHUNK
    return pl.pallas_call(
        _mlstm_kernel,
        grid=(b, s // L),
        in_specs=[
            pl.BlockSpec((None, L, 2 * ML_PW), lambda i, j: (i, j, COL_MQ // (2 * ML_PW))),
            pl.BlockSpec((None, L, ML_PW), lambda i, j: (i, j, COL_MV // ML_PW)),
            pl.BlockSpec((None, L, ML_PW), lambda i, j: (i, j, COL_MO // ML_PW)),
            pl.BlockSpec((None, L, ML_PW), lambda i, j: (i, j, COL_MZ // ML_PW)),
            pl.BlockSpec((None, L, G_COLS), lambda i, j: (i, j, 0)),
            pl.BlockSpec((None, GATE_ROWS, L), lambda i, j: (i, 0, j)),
            pl.BlockSpec((CONV_WIDTH, 2 * ML_PW), lambda i, j: (0, 0)),
            pl.BlockSpec((1, 2 * ML_PW), lambda i, j: (0, 0)),
            pl.BlockSpec((1, ML_PW), lambda i, j: (0, 0)),
        ],
        out_specs=pl.BlockSpec((None, L, ML_PW), lambda i, j: (i, j, 0)),
        out_shape=jax.ShapeDtypeStruct((b, s, ML_PW), BF16),
        scratch_shapes=[pltpu.VMEM((SUBLANES, 2 * ML_PW), F32),
                        pltpu.VMEM((ML_HEADS, ML_PAD, LANES), F32),
                        pltpu.VMEM((ML_HEADS, SUBLANES, LANES), F32)],
        compiler_params=pltpu.CompilerParams(
            dimension_semantics=("arbitrary", "arbitrary"), vmem_limit_bytes=48 * 1024 * 1024),
        name="mlstm",
    )(u, u, u, u, cc, ct, conv_w, conv_b, norm_g)


def _memkv_kernel(m_ref, w_ref, o_ref):
    o_ref[...] = jnp.dot(m_ref[...].astype(BF16), w_ref[...].astype(BF16),
                         preferred_element_type=F32).astype(BF16)


def _memkv(mem, w_mem_kv):
    b, ml, _ = mem.shape
    return pl.pallas_call(
        _memkv_kernel,
        grid=(DEPTH, b),
        in_specs=[pl.BlockSpec((None, ml, D_MODEL), lambda l, i: (i, 0, 0)),
                  pl.BlockSpec((None, D_MODEL, 2 * MEM_WIDTH), lambda l, i: (l, 0, 0))],
        out_specs=pl.BlockSpec((None, None, ml, 2 * MEM_WIDTH), lambda l, i: (l, i, 0, 0)),
        out_shape=jax.ShapeDtypeStruct((DEPTH, b, ml, 2 * MEM_WIDTH), BF16),
        compiler_params=pltpu.CompilerParams(dimension_semantics=("arbitrary", "arbitrary")),
        name="mem_kv",
    )(mem, w_mem_kv)


def _out_kernel(x_ref, yf_ref, ym_ref, rq_ref, rz_ref, mk_ref, mv_ref, wo_ref, g_ref, b_ref, o_ref):
    rq = rq_ref[...]
    mk = mk_ref[...]
    mv = mv_ref[...]
    lane = lax.broadcasted_iota(jnp.int32, (1, MEM_WIDTH), 1)
    scale = jnp.asarray(MEM_HEAD_DIM ** -0.5, BF16)
    att = jnp.zeros((OUT_TM, MEM_WIDTH), F32)
    for h in range(MEM_HEADS):
        msk = (lane >= MEM_HEAD_DIM * h) & (lane < MEM_HEAD_DIM * (h + 1))
        qh = jnp.where(msk, rq, jnp.zeros_like(rq)) * scale
        s = lax.dot_general(qh, mk, (((1,), (1,)), ((), ())), preferred_element_type=F32)
        p = jnp.exp(s - jnp.max(s, axis=1, keepdims=True))
        l = jnp.sum(p, axis=1, keepdims=True)
        vh = jnp.where(msk, mv, jnp.zeros_like(mv))
        att = att + jnp.dot(p.astype(BF16), vh, preferred_element_type=F32) / l
    y_mem = (att * _silu(rz_ref[...].astype(F32))).astype(BF16)
    y = (jnp.dot(yf_ref[...], wo_ref[0:FOX_WIDTH, :], preferred_element_type=F32)
         + jnp.dot(ym_ref[...], wo_ref[FOX_WIDTH:FOX_WIDTH + ML_PW, :], preferred_element_type=F32)
         + jnp.dot(y_mem, wo_ref[FOX_WIDTH + ML_PW:, :], preferred_element_type=F32))
    r = DEEPNORM_ALPHA * x_ref[...] + y
    mu = jnp.mean(r, axis=1, keepdims=True)
    d = r - mu
    var = jnp.mean(d * d, axis=1, keepdims=True)
    o_ref[...] = d * lax.rsqrt(var + LN_EPS) * g_ref[...] + b_ref[...]


def _out_proj(x, y_fox, y_ml, u, memkv_l, w_out_re, ln_g, ln_b):
    b, s, _ = x.shape
    tm = OUT_TM
    mix = FOX_WIDTH + ML_PW + MEM_WIDTH
    ml = memkv_l.shape[1]
    return pl.pallas_call(
        _out_kernel,
        grid=(b, s // tm),
        in_specs=[
            pl.BlockSpec((None, tm, D_MODEL), lambda i, j: (i, j, 0)),
            pl.BlockSpec((None, tm, FOX_WIDTH), lambda i, j: (i, j, 0)),
            pl.BlockSpec((None, tm, ML_PW), lambda i, j: (i, j, 0)),
            pl.BlockSpec((None, tm, MEM_WIDTH), lambda i, j: (i, j, COL_RQ // MEM_WIDTH)),
            pl.BlockSpec((None, tm, MEM_WIDTH), lambda i, j: (i, j, COL_RZ // MEM_WIDTH)),
            pl.BlockSpec((None, ml, MEM_WIDTH), lambda i, j: (i, 0, 0)),
            pl.BlockSpec((None, ml, MEM_WIDTH), lambda i, j: (i, 0, 1)),
            pl.BlockSpec((mix, D_MODEL), lambda i, j: (0, 0)),
            pl.BlockSpec((1, D_MODEL), lambda i, j: (0, 0)),
            pl.BlockSpec((1, D_MODEL), lambda i, j: (0, 0)),
        ],
        out_specs=pl.BlockSpec((None, tm, D_MODEL), lambda i, j: (i, j, 0)),
        out_shape=jax.ShapeDtypeStruct((b, s, D_MODEL), F32),
        compiler_params=pltpu.CompilerParams(
            dimension_semantics=("arbitrary", "arbitrary"), vmem_limit_bytes=48 * 1024 * 1024),
        name="out_proj_ln",
    )(x, y_fox, y_ml, u, u, memkv_l, memkv_l, w_out_re, ln_g, ln_b)


def kernel(x, mem, w_in, fox_f_bias, mlstm_conv_w, mlstm_conv_b, mlstm_i_bias, mlstm_f_bias,
           mlstm_norm_g, w_mem_kv, w_out, ln_g, ln_b):
    b, s, d = x.shape
    memkv = _memkv(mem, w_mem_kv)
    for l in range(DEPTH):
        w_re = _relayout_w_in(w_in[l])
        bias_row = _gate_bias_row(fox_f_bias[l], mlstm_i_bias[l], mlstm_f_bias[l])
        conv_w = jnp.concatenate([_pad_heads(mlstm_conv_w[l][:, :ML_WIDTH], 1),
                                  _pad_heads(mlstm_conv_w[l][:, ML_WIDTH:], 1)], axis=1)
        conv_b = jnp.concatenate([_pad_heads(mlstm_conv_b[l][None, :ML_WIDTH], 1),
                                  _pad_heads(mlstm_conv_b[l][None, ML_WIDTH:], 1)], axis=1)
        norm_g = _pad_heads(mlstm_norm_g[l][None, :], 1)
        w_out_re = _relayout_w_out(w_out[l])

        u2d, g2d = _project(x.reshape(b * s, d), w_re)
        u = u2d.reshape(b, s, U_COLS)
        cc, ct = _gates(g2d.reshape(b, s, G_COLS), bias_row)
        y_fox = _fox(u, cc, ct)
        y_ml = _mlstm(u, cc, ct, conv_w, conv_b, norm_g)
        x = _out_proj(x, y_fox, y_ml, u, memkv[l], w_out_re, ln_g[l][None, :], ln_b[l][None, :])
    return x
```

```python
import math

import numpy as np
import jax
import jax.numpy as jnp
from jax import lax
from jax.experimental import pallas as pl
from jax.experimental.pallas import tpu as pltpu

F32 = jnp.float32
BF16 = jnp.bfloat16

D_MODEL = 1024
DEPTH = 4
FOX_HEADS = 6
FOX_HEAD_DIM = 64
FOX_WIDTH = FOX_HEADS * FOX_HEAD_DIM
FOX_PAIRS = FOX_HEADS // 2
ML_HEADS = 4
ML_HEAD_DIM = 96
ML_WIDTH = ML_HEADS * ML_HEAD_DIM
MEM_HEADS = 4
MEM_HEAD_DIM = 64
MEM_WIDTH = MEM_HEADS * MEM_HEAD_DIM
CONV_WIDTH = 4
LN_EPS = 1e-5
DEEPNORM_ALPHA = (2.0 * DEPTH) ** 0.25
LOG2E = math.log2(math.e)

LANES = 128
SUBLANES = 8
ML_PAD = LANES
ML_PW = ML_HEADS * ML_PAD
DEN_LANE = ML_HEAD_DIM

UML_COLS = 5 * ML_PW
UFX_COLS = 2 * FOX_WIDTH
UMEM_COLS = 2 * MEM_WIDTH
UQV_COLS = 2 * FOX_WIDTH
G_COLS = LANES
W_COLS = UML_COLS + UFX_COLS + UMEM_COLS + UQV_COLS + G_COLS
COL_FX = UML_COLS
COL_MEM = COL_FX + UFX_COLS
COL_QV = COL_MEM + UMEM_COLS
COL_G = COL_QV + UQV_COLS
GATE_I = 48
GATE_F = 56
GATE_ROWS = 64
N_PIECES = 3

ML_CHUNK = 256
FOX_TQ = 256
FOX_TK = 256
FOX_VROWS = FOX_HEAD_DIM + 16
PROJ_TM = 256
OUT_TM = 256
PROJ_TN = 512

_SRC = {}
_off = 0
for _name, _w in (("fq", FOX_WIDTH), ("fk", FOX_WIDTH), ("fv", FOX_WIDTH), ("ff", FOX_HEADS),
                  ("fz", FOX_WIDTH), ("mq", ML_WIDTH), ("mk", ML_WIDTH), ("mv", ML_WIDTH),
                  ("mi", ML_HEADS), ("mf", ML_HEADS), ("mo", ML_WIDTH), ("mz", ML_WIDTH),
                  ("rq", MEM_WIDTH), ("rz", MEM_WIDTH)):
    _SRC[_name] = (_off, _w)
    _off += _w


def _sigmoid(x):
    return 1.0 / (1.0 + jnp.exp(-x))


def _silu(x):
    return x * _sigmoid(x)


def _zeros_like_cols(t, n):
    return jnp.zeros(t.shape[:-1] + (n,), t.dtype)


def _pad_heads_last(t):
    parts = []
    for h in range(ML_HEADS):
        parts.append(t[..., h * ML_HEAD_DIM:(h + 1) * ML_HEAD_DIM])
        parts.append(_zeros_like_cols(t, ML_PAD - ML_HEAD_DIM))
    return jnp.concatenate(parts, axis=-1)


def _spread_gates(ff, mi, mf):
    parts = []
    for h in range(FOX_HEADS):
        parts += [ff[..., h:h + 1], _zeros_like_cols(ff, SUBLANES - 1)]
    parts += [mi, _zeros_like_cols(mi, GATE_F - GATE_I - ML_HEADS),
              mf, _zeros_like_cols(mf, G_COLS - GATE_F - ML_HEADS)]
    return jnp.concatenate(parts, axis=-1)


def _relayout_w_in(w):
    def take(name):
        o, n = _SRC[name]
        return w[..., o:o + n]

    cols = [_pad_heads_last(take(n)) for n in ("mq", "mk", "mv", "mo", "mz")]
    cols += [take("fk"), take("fz"), take("rq"), take("rz")]
    cols += [take("fq") * (FOX_HEAD_DIM ** -0.5 * LOG2E), take("fv")]
    cols += [_spread_gates(take("ff"), take("mi"), take("mf"))]
    return jnp.concatenate(cols, axis=-1).astype(BF16)


def _relayout_w_out(w):
    parts = [w[:, :FOX_WIDTH]]
    for h in range(ML_HEADS):
        o = FOX_WIDTH + h * ML_HEAD_DIM
        parts += [w[:, o:o + ML_HEAD_DIM],
                  jnp.zeros((w.shape[0], ML_PAD - ML_HEAD_DIM, w.shape[2]), w.dtype)]
    parts.append(w[:, FOX_WIDTH + ML_WIDTH:])
    return jnp.concatenate(parts, axis=1).astype(BF16)


def _bias_selectors():
    selk = np.zeros((N_PIECES * LANES, LANES), np.float32)
    selq = np.zeros((N_PIECES * LANES, LANES), np.float32)
    onek = np.zeros((1, LANES), np.float32)
    oneq = np.zeros((1, LANES), np.float32)
    for h in range(FOX_HEADS):
        base = SUBLANES * h
        for j in range(N_PIECES):
            selk[j * LANES + base, base + j] = -1.0
            selq[j * LANES + base, base + N_PIECES + j] = 1.0
            onek[0, base + N_PIECES + j] = 1.0
            oneq[0, base + j] = 1.0
    return (jnp.asarray(selk, BF16), jnp.asarray(selq, BF16), jnp.asarray(onek), jnp.asarray(oneq))


def _proj_kernel(x_ref, w_ref, uml_ref, ufx_ref, umem_ref, g_ref, qt_ref, vt_ref):
    xb = x_ref[...].astype(BF16)

    def mm(c0, n):
        return jnp.dot(xb, w_ref[:, c0:c0 + n], preferred_element_type=F32)

    for c0 in range(0, UML_COLS, PROJ_TN):
        uml_ref[:, c0:c0 + PROJ_TN] = mm(c0, PROJ_TN).astype(BF16)
    ufx_ref[...] = mm(COL_FX, UFX_COLS).astype(BF16)
    umem_ref[...] = mm(COL_MEM, UMEM_COLS).astype(BF16)
    qt_ref[...] = mm(COL_QV, FOX_WIDTH).T.astype(BF16)
    v_t = mm(COL_QV + FOX_WIDTH, FOX_WIDTH).T.astype(BF16)
    ones = jnp.ones((FOX_VROWS - FOX_HEAD_DIM, v_t.shape[1]), BF16)
    rows = []
    for h in range(FOX_HEADS):
        rows += [v_t[h * FOX_HEAD_DIM:(h + 1) * FOX_HEAD_DIM, :], ones]
    vt_ref[...] = jnp.concatenate(rows, axis=0)
    g_ref[...] = mm(COL_G, G_COLS)


def _project(x, w_re):
    b, s, _ = x.shape
    tm = PROJ_TM

    def row(n):
        return pl.BlockSpec((None, tm, n), lambda i, j: (i, j, 0))

    def col(n):
        return pl.BlockSpec((None, n, tm), lambda i, j: (i, 0, j))

    return pl.pallas_call(
        _proj_kernel,
        grid=(b, s // tm),
        in_specs=[row(D_MODEL), pl.BlockSpec((D_MODEL, W_COLS), lambda i, j: (0, 0))],
        out_specs=[row(UML_COLS), row(UFX_COLS), row(UMEM_COLS), row(G_COLS),
                   col(FOX_WIDTH), col(FOX_HEADS * FOX_VROWS)],
        out_shape=[jax.ShapeDtypeStruct((b, s, UML_COLS), BF16),
                   jax.ShapeDtypeStruct((b, s, UFX_COLS), BF16),
                   jax.ShapeDtypeStruct((b, s, UMEM_COLS), BF16),
                   jax.ShapeDtypeStruct((b, s, G_COLS), F32),
                   jax.ShapeDtypeStruct((b, FOX_WIDTH, s), BF16),
                   jax.ShapeDtypeStruct((b, FOX_HEADS * FOX_VROWS, s), BF16)],
        compiler_params=pltpu.CompilerParams(
            dimension_semantics=("arbitrary", "arbitrary"), vmem_limit_bytes=56 * 1024 * 1024),
        name="in_proj",
    )(x, w_re)


def _gates_kernel(g_ref, bias_ref, selk_ref, selq_ref, onek_ref, oneq_ref,
                  cc_ref, ct_ref, kb_ref, qbt_ref, carry_ref):
    @pl.when(pl.program_id(1) == 0)
    def _():
        carry_ref[...] = jnp.zeros_like(carry_ref)

    L = ML_CHUNK
    x = g_ref[...] + bias_ref[...]
    log_sig = jnp.minimum(x, 0.0) - jnp.log1p(jnp.exp(-jnp.abs(x)))
    r = lax.broadcasted_iota(jnp.int32, (L, L), 0)
    c = lax.broadcasted_iota(jnp.int32, (L, L), 1)
    tri = (c <= r).astype(F32)
    local = jnp.dot(tri, log_sig, precision=lax.Precision.HIGHEST, preferred_element_type=F32)
    lane = lax.broadcasted_iota(jnp.int32, (L, G_COLS), 1)
    carry = carry_ref[0:1, :]
    out = jnp.where(lane < GATE_I, local + carry, jnp.where(lane < GATE_F, x, local))
    carry_ref[0:1, :] = carry + local[L - 1:L, :]
    cc_ref[...] = out
    ct_ref[...] = out.T[:GATE_ROWS, :]

    c2 = out * LOG2E
    hi = c2.astype(BF16)
    r1 = c2 - hi.astype(F32)
    mid = r1.astype(BF16)
    lo = (r1 - mid.astype(F32)).astype(BF16)
    pieces = jnp.concatenate([hi, mid, lo], axis=1)
    kb_ref[...] = (jnp.dot(pieces, selk_ref[...], preferred_element_type=F32)
                   + onek_ref[...]).astype(BF16)
    qb = jnp.dot(pieces, selq_ref[...], preferred_element_type=F32) + oneq_ref[...]
    qbt_ref[...] = qb.T.astype(BF16)


def _gates(g, bias_row, sels):
    b, s, _ = g.shape
    L = ML_CHUNK
    const = lambda shape: pl.BlockSpec(shape, lambda i, j: (0, 0))
    return pl.pallas_call(
        _gates_kernel,
        grid=(b, s // L),
        in_specs=[pl.BlockSpec((None, L, G_COLS), lambda i, j: (i, j, 0)),
                  const((1, G_COLS)),
                  const((N_PIECES * LANES, LANES)), const((N_PIECES * LANES, LANES)),
                  const((1, LANES)), const((1, LANES))],
        out_specs=[pl.BlockSpec((None, L, G_COLS), lambda i, j: (i, j, 0)),
                   pl.BlockSpec((None, GATE_ROWS, L), lambda i, j: (i, 0, j)),
                   pl.BlockSpec((None, L, LANES), lambda i, j: (i, j, 0)),
                   pl.BlockSpec((None, LANES, L), lambda i, j: (i, 0, j))],
        out_shape=[jax.ShapeDtypeStruct((b, s, G_COLS), F32),
                   jax.ShapeDtypeStruct((b, GATE_ROWS, s), F32),
                   jax.ShapeDtypeStruct((b, s, LANES), BF16),
                   jax.ShapeDtypeStruct((b, LANES, s), BF16)],
        scratch_shapes=[pltpu.VMEM((SUBLANES, G_COLS), F32)],
        compiler_params=pltpu.CompilerParams(dimension_semantics=("arbitrary", "arbitrary")),
        name="gate_prefix",
    )(g, bias_row, *sels)


def _fox_kernel(qt_ref, qbt_ref, k_ref, kb_ref, vt_ref, z_ref, o_ref,
                qa_sc, sa_sc, sb_sc, m_sc, acc_sc):
    qi = pl.program_id(1)
    tq, tk = FOX_TQ, FOX_TK
    hd = FOX_HEAD_DIM

    brow = lax.broadcasted_iota(jnp.int32, (LANES, tq), 0)
    qbt = qbt_ref[...]
    zslot = jnp.zeros((hd, tq), BF16)
    for h in range(FOX_HEADS):
        qh = qt_ref[h * hd:(h + 1) * hd, :]
        top = [qh, zslot] if h % 2 == 0 else [zslot, qh]
        own = (brow >= SUBLANES * h) & (brow < SUBLANES * (h + 1))
        qa_sc[h] = jnp.concatenate(top + [jnp.where(own, qbt, jnp.zeros_like(qbt))], axis=0)

    m_sc[...] = jnp.full_like(m_sc, -jnp.inf)
    acc_sc[...] = jnp.zeros_like(acc_sc)

    def logits(j, h, dst):
        k0 = pl.multiple_of(j * tk, tk)
        p = h // 2
        kk = jnp.concatenate([k_ref[pl.ds(k0, tk), p * LANES:(p + 1) * LANES],
                              kb_ref[pl.ds(k0, tk), :]], axis=1)
        dst[h] = jnp.dot(kk, qa_sc[h], preferred_element_type=F32)

    def softmax_pv(j, h, src, masked):
        k0 = pl.multiple_of(j * tk, tk)
        st = src[h]
        if masked:
            kr = lax.broadcasted_iota(jnp.int32, (tk, tq), 0)
            qc = lax.broadcasted_iota(jnp.int32, (tk, tq), 1)
            st = jnp.where(kr <= qc, st, -jnp.inf)
        m_old = m_sc[h]
        m_new = jnp.maximum(m_old, jnp.max(st, axis=0, keepdims=True))
        alpha = jnp.exp2(m_old - m_new)
        pt = jnp.exp2(st - m_new).astype(BF16)
        vt = vt_ref[h * FOX_VROWS:(h + 1) * FOX_VROWS, pl.ds(k0, tk)]
        acc_sc[h] = alpha * acc_sc[h] + jnp.dot(vt, pt, preferred_element_type=F32)
        m_sc[h] = m_new

    def stage(j_sm, src, masked, j_qk, dst):
        for h in range(FOX_HEADS):
            if j_sm is not None:
                softmax_pv(j_sm, h, src, masked)
            if j_qk is not None:
                logits(j_qk, h, dst)

    stage(None, None, False, 0, sa_sc)

    def body(i, carry):
        stage(2 * i, sa_sc, False, 2 * i + 1, sb_sc)
        stage(2 * i + 1, sb_sc, False, 2 * i + 2, sa_sc)
        return carry

    lax.fori_loop(0, qi // 2, body, 0)

    @pl.when(qi % 2 == 0)
    def _():
        stage(qi, sa_sc, True, None, None)

    @pl.when(qi % 2 == 1)
    def _():
        stage(qi - 1, sa_sc, False, qi, sb_sc)
        stage(qi, sb_sc, True, None, None)

    out_t = jnp.concatenate(
        [acc_sc[h, :hd, :] / acc_sc[h, hd:hd + 1, :] for h in range(FOX_HEADS)], axis=0)
    z = z_ref[...].astype(F32)
    o_ref[...] = (out_t.T * _silu(z)).astype(BF16)


def _fox(qt, qbt, ufx, kb, vt):
    b, s, _ = ufx.shape
    tq = FOX_TQ
    return pl.pallas_call(
        _fox_kernel,
        grid=(b, s // tq),
        in_specs=[
            pl.BlockSpec((None, FOX_WIDTH, tq), lambda i, j: (i, 0, j)),
            pl.BlockSpec((None, LANES, tq), lambda i, j: (i, 0, j)),
            pl.BlockSpec((None, s, FOX_WIDTH), lambda i, j: (i, 0, 0)),
            pl.BlockSpec((None, s, LANES), lambda i, j: (i, 0, 0)),
            pl.BlockSpec((None, FOX_HEADS * FOX_VROWS, s), lambda i, j: (i, 0, 0)),
            pl.BlockSpec((None, tq, FOX_WIDTH), lambda i, j: (i, j, 1)),
        ],
        out_specs=pl.BlockSpec((None, tq, FOX_WIDTH), lambda i, j: (i, j, 0)),
        out_shape=jax.ShapeDtypeStruct((b, s, FOX_WIDTH), BF16),
        scratch_shapes=[pltpu.VMEM((FOX_HEADS, 2 * LANES, tq), BF16),
                        pltpu.VMEM((FOX_HEADS, FOX_TK, tq), F32),
                        pltpu.VMEM((FOX_HEADS, FOX_TK, tq), F32),
                        pltpu.VMEM((FOX_HEADS, 1, tq), F32),
                        pltpu.VMEM((FOX_HEADS, FOX_VROWS, tq), F32)],
        compiler_params=pltpu.CompilerParams(
            dimension_semantics=("arbitrary", "arbitrary"), vmem_limit_bytes=48 * 1024 * 1024),
        name="fox_attn",
    )(qt, qbt, ufx, kb, vt, ufx)


def _mlstm_kernel(qk_ref, v_ref, og_ref, z_ref, cc_ref, ct_ref, cw_ref, cb_ref, ng_ref,
                  y_ref, tail_sc, c_sc, m_sc):
    L = ML_CHUNK

    @pl.when(pl.program_id(1) == 0)
    def _():
        tail_sc[...] = jnp.zeros_like(tail_sc)
        c_sc[...] = jnp.zeros_like(c_sc)
        m_sc[...] = jnp.zeros_like(m_sc)

    u = qk_ref[...].astype(F32)
    tail = tail_sc[...]
    w = cw_ref[...]
    row8 = lax.broadcasted_iota(jnp.int32, (SUBLANES, 2 * ML_PW), 0)
    y = u * w[CONV_WIDTH - 1:CONV_WIDTH, :] + cb_ref[...]
    for shift in range(1, CONV_WIDTH):
        r = pltpu.roll(u, shift, axis=0)
        rt = pltpu.roll(tail, shift, axis=0)
        first = jnp.where(row8 < shift, rt, r[:SUBLANES])
        xs = jnp.concatenate([first, r[SUBLANES:]], axis=0)
        y = y + xs * w[CONV_WIDTH - 1 - shift:CONV_WIDTH - shift, :]
    tail_sc[...] = u[L - SUBLANES:, :]
    a = _silu(y)

    cc = cc_ref[...]
    lane = lax.broadcasted_iota(jnp.int32, (1, LANES), 1)
    real = lane < ML_HEAD_DIM
    rr = lax.broadcasted_iota(jnp.int32, (L, L), 0)
    rc = lax.broadcasted_iota(jnp.int32, (L, L), 1)
    tril = rc <= rr
    kscale = ML_HEAD_DIM ** -0.5

    for h in range(ML_HEADS):
        sl = slice(h * ML_PAD, (h + 1) * ML_PAD)
        q = a[:, sl].astype(BF16)
        k = (a[:, ML_PW + h * ML_PAD:ML_PW + (h + 1) * ML_PAD] * kscale).astype(BF16)
        v = v_ref[:, sl]
        v_aug = jnp.where(lane == DEN_LANE, jnp.ones_like(v), v)
        bcol = cc[:, GATE_F + h:GATE_F + h + 1]
        icol = cc[:, GATE_I + h:GATE_I + h + 1]
        brow = ct_ref[GATE_F + h:GATE_F + h + 1, :]
        irow = ct_ref[GATE_I + h:GATE_I + h + 1, :]
        m_prev = m_sc[h, 0:1, 0:1]

        log_d = jnp.where(tril, bcol + (irow - brow), -jnp.inf)
        inter = bcol + m_prev
        m_row = jnp.maximum(inter, jnp.max(log_d, axis=1, keepdims=True))
        dmat = jnp.exp(log_d - m_row)
        s = lax.dot_general(q, k, (((1,), (1,)), ((), ())), preferred_element_type=F32) * dmat
        dec = jnp.exp(inter - m_row)
        c_prev = c_sc[h]
        num = (jnp.dot(s.astype(BF16), v_aug, preferred_element_type=F32)
               + dec * jnp.dot(q, c_prev.astype(BF16), preferred_element_type=F32))
        den = jnp.sum(jnp.where(lane == DEN_LANE, num, 0.0), axis=1, keepdims=True)
        hval = num / jnp.maximum(jnp.abs(den), jnp.exp(-m_row))

        b_last = bcol[L - 1:L, :]
        log_w = b_last - bcol + icol
        m_new = jnp.maximum(b_last + m_prev, jnp.max(log_w, axis=0, keepdims=True))
        wgt = jnp.exp(log_w - m_new)
        cdec = jnp.exp(b_last + m_prev - m_new)
        wv = (wgt * v_aug.astype(F32)).astype(BF16)
        c_sc[h] = cdec * c_prev + lax.dot_general(
            k, wv, (((0,), (0,)), ((), ())), preferred_element_type=F32)
        m_sc[h] = jnp.broadcast_to(m_new, (SUBLANES, LANES))

        hv = jnp.where(real, _sigmoid(og_ref[:, sl].astype(F32)) * hval, 0.0)
        mu = jnp.sum(hv, axis=1, keepdims=True) * (1.0 / ML_HEAD_DIM)
        d = jnp.where(real, hv - mu, 0.0)
        var = jnp.sum(d * d, axis=1, keepdims=True) * (1.0 / ML_HEAD_DIM)
        hn = d * lax.rsqrt(var + LN_EPS) * ng_ref[:, sl]
        y_ref[:, sl] = (hn * _silu(z_ref[:, sl].astype(F32))).astype(BF16)


def _mlstm(uml, cc, ct, conv_w, conv_b, norm_g):
    b, s, _ = uml.shape
    L = ML_CHUNK
    return pl.pallas_call(
        _mlstm_kernel,
        grid=(b, s // L),
        in_specs=[
            pl.BlockSpec((None, L, 2 * ML_PW), lambda i, j: (i, j, 0)),
            pl.BlockSpec((None, L, ML_PW), lambda i, j: (i, j, 2)),
            pl.BlockSpec((None, L, ML_PW), lambda i, j: (i, j, 3)),
            pl.BlockSpec((None, L, ML_PW), lambda i, j: (i, j, 4)),
            pl.BlockSpec((None, L, G_COLS), lambda i, j: (i, j, 0)),
            pl.BlockSpec((None, GATE_ROWS, L), lambda i, j: (i, 0, j)),
            pl.BlockSpec((CONV_WIDTH, 2 * ML_PW), lambda i, j: (0, 0)),
            pl.BlockSpec((1, 2 * ML_PW), lambda i, j: (0, 0)),
            pl.BlockSpec((1, ML_PW), lambda i, j: (0, 0)),
        ],
        out_specs=pl.BlockSpec((None, L, ML_PW), lambda i, j: (i, j, 0)),
        out_shape=jax.ShapeDtypeStruct((b, s, ML_PW), BF16),
        scratch_shapes=[pltpu.VMEM((SUBLANES, 2 * ML_PW), F32),
                        pltpu.VMEM((ML_HEADS, ML_PAD, LANES), F32),
                        pltpu.VMEM((ML_HEADS, SUBLANES, LANES), F32)],
        compiler_params=pltpu.CompilerParams(
            dimension_semantics=("arbitrary", "arbitrary"), vmem_limit_bytes=48 * 1024 * 1024),
        name="mlstm",
    )(uml, uml, uml, uml, cc, ct, conv_w, conv_b, norm_g)


def _memkv_kernel(m_ref, w_ref, o_ref):
    o_ref[...] = jnp.dot(m_ref[...].astype(BF16), w_ref[...].astype(BF16),
                         preferred_element_type=F32).astype(BF16)


def _memkv(mem, w_mem_kv):
    b, ml, _ = mem.shape
    return pl.pallas_call(
        _memkv_kernel,
        grid=(DEPTH, b),
        in_specs=[pl.BlockSpec((None, ml, D_MODEL), lambda l, i: (i, 0, 0)),
                  pl.BlockSpec((None, D_MODEL, 2 * MEM_WIDTH), lambda l, i: (l, 0, 0))],
        out_specs=pl.BlockSpec((None, None, ml, 2 * MEM_WIDTH), lambda l, i: (l, i, 0, 0)),
        out_shape=jax.ShapeDtypeStruct((DEPTH, b, ml, 2 * MEM_WIDTH), BF16),
        compiler_params=pltpu.CompilerParams(dimension_semantics=("arbitrary", "arbitrary")),
        name="mem_kv",
    )(mem, w_mem_kv)


def _out_kernel(x_ref, yf_ref, ym_ref, rq_ref, rz_ref, mk_ref, mv_ref, wo_ref, g_ref, b_ref, o_ref):
    rq = rq_ref[...]
    mk = mk_ref[...]
    mv = mv_ref[...]
    lane = lax.broadcasted_iota(jnp.int32, (1, MEM_WIDTH), 1)
    scale = jnp.asarray(MEM_HEAD_DIM ** -0.5, BF16)
    att = jnp.zeros((OUT_TM, MEM_WIDTH), F32)
    for h in range(MEM_HEADS):
        msk = (lane >= MEM_HEAD_DIM * h) & (lane < MEM_HEAD_DIM * (h + 1))
        qh = jnp.where(msk, rq, jnp.zeros_like(rq)) * scale
        s = lax.dot_general(qh, mk, (((1,), (1,)), ((), ())), preferred_element_type=F32)
        p = jnp.exp(s - jnp.max(s, axis=1, keepdims=True))
        l = jnp.sum(p, axis=1, keepdims=True)
        vh = jnp.where(msk, mv, jnp.zeros_like(mv))
        att = att + jnp.dot(p.astype(BF16), vh, preferred_element_type=F32) / l
    y_mem = (att * _silu(rz_ref[...].astype(F32))).astype(BF16)
    y = (jnp.dot(yf_ref[...], wo_ref[0:FOX_WIDTH, :], preferred_element_type=F32)
         + jnp.dot(ym_ref[...], wo_ref[FOX_WIDTH:FOX_WIDTH + ML_PW, :], preferred_element_type=F32)
         + jnp.dot(y_mem, wo_ref[FOX_WIDTH + ML_PW:, :], preferred_element_type=F32))
    r = DEEPNORM_ALPHA * x_ref[...] + y
    mu = jnp.mean(r, axis=1, keepdims=True)
    d = r - mu
    var = jnp.mean(d * d, axis=1, keepdims=True)
    o_ref[...] = d * lax.rsqrt(var + LN_EPS) * g_ref[...] + b_ref[...]


def _out_proj(x, y_fox, y_ml, umem, memkv, layer, w_out_re, ln_g, ln_b):
    b, s, _ = x.shape
    tm = OUT_TM
    mix = FOX_WIDTH + ML_PW + MEM_WIDTH
    ml = memkv.shape[2]
    return pl.pallas_call(
        _out_kernel,
        grid=(b, s // tm),
        in_specs=[
            pl.BlockSpec((None, tm, D_MODEL), lambda i, j: (i, j, 0)),
            pl.BlockSpec((None, tm, FOX_WIDTH), lambda i, j: (i, j, 0)),
            pl.BlockSpec((None, tm, ML_PW), lambda i, j: (i, j, 0)),
            pl.BlockSpec((None, tm, MEM_WIDTH), lambda i, j: (i, j, 0)),
            pl.BlockSpec((None, tm, MEM_WIDTH), lambda i, j: (i, j, 1)),
            pl.BlockSpec((None, None, ml, MEM_WIDTH), lambda i, j: (layer, i, 0, 0)),
            pl.BlockSpec((None, None, ml, MEM_WIDTH), lambda i, j: (layer, i, 0, 1)),
            pl.BlockSpec((None, mix, D_MODEL), lambda i, j: (layer, 0, 0)),
            pl.BlockSpec((None, 1, D_MODEL), lambda i, j: (layer, 0, 0)),
            pl.BlockSpec((None, 1, D_MODEL), lambda i, j: (layer, 0, 0)),
        ],
        out_specs=pl.BlockSpec((None, tm, D_MODEL), lambda i, j: (i, j, 0)),
        out_shape=jax.ShapeDtypeStruct((b, s, D_MODEL), F32),
        compiler_params=pltpu.CompilerParams(
            dimension_semantics=("arbitrary", "arbitrary"), vmem_limit_bytes=48 * 1024 * 1024),
        name="out_proj_ln",
    )(x, y_fox, y_ml, umem, umem, memkv, memkv, w_out_re, ln_g, ln_b)


def kernel(x, mem, w_in, fox_f_bias, mlstm_conv_w, mlstm_conv_b, mlstm_i_bias, mlstm_f_bias,
           mlstm_norm_g, w_mem_kv, w_out, ln_g, ln_b):
    w_re = _relayout_w_in(w_in)
    w_out_re = _relayout_w_out(w_out)
    bias_rows = _spread_gates(fox_f_bias, mlstm_i_bias, mlstm_f_bias)[:, None, :]
    conv_w = jnp.concatenate([_pad_heads_last(mlstm_conv_w[..., :ML_WIDTH]),
                              _pad_heads_last(mlstm_conv_w[..., ML_WIDTH:])], axis=-1)
    conv_b = jnp.concatenate([_pad_heads_last(mlstm_conv_b[..., :ML_WIDTH]),
                              _pad_heads_last(mlstm_conv_b[..., ML_WIDTH:])], axis=-1)[:, None, :]
    norm_g = _pad_heads_last(mlstm_norm_g)[:, None, :]
    ln_g3 = ln_g[:, None, :]
    ln_b3 = ln_b[:, None, :]
    sels = _bias_selectors()

    memkv = _memkv(mem, w_mem_kv)
    for l in range(DEPTH):
        uml, ufx, umem, g, qt, vt = _project(x, w_re[l])
        cc, ct, kb, qbt = _gates(g, bias_rows[l], sels)
        y_fox = _fox(qt, qbt, ufx, kb, vt)
        y_ml = _mlstm(uml, cc, ct, conv_w[l], conv_b[l], norm_g[l])
        x = _out_proj(x, y_fox, y_ml, umem, memkv, l, w_out_re, ln_g3, ln_b3)
    return x
```

```python
import math

import numpy as np
import jax
import jax.numpy as jnp
from jax import lax
from jax.experimental import pallas as pl
from jax.experimental.pallas import tpu as pltpu

F32 = jnp.float32
BF16 = jnp.bfloat16

D_MODEL = 1024
DEPTH = 4
FOX_HEADS = 6
FOX_HEAD_DIM = 64
FOX_WIDTH = FOX_HEADS * FOX_HEAD_DIM
ML_HEADS = 4
ML_HEAD_DIM = 96
ML_WIDTH = ML_HEADS * ML_HEAD_DIM
MEM_HEADS = 4
MEM_HEAD_DIM = 64
MEM_WIDTH = MEM_HEADS * MEM_HEAD_DIM
CONV_WIDTH = 4
LN_EPS = 1e-5
DEEPNORM_ALPHA = (2.0 * DEPTH) ** 0.25
LOG2E = math.log2(math.e)

LANES = 128
SUBLANES = 8
BF16_ROWS = 16
ML_PAD = LANES
ML_PW = ML_HEADS * ML_PAD
ML_VROWS = ML_HEAD_DIM + BF16_ROWS
FOX_VROWS = FOX_HEAD_DIM + BF16_ROWS

COL_QK = 0
COL_MV = COL_QK + 2 * ML_PW
COL_MZ = COL_MV + 2 * ML_WIDTH
COL_FK = COL_MZ + ML_WIDTH + FOX_WIDTH
COL_FV = COL_FK + 2 * FOX_WIDTH
COL_RQ = COL_FV + FOX_WIDTH + LANES
W_COLS = COL_RQ + 2 * MEM_WIDTH
G_COLS = LANES
GATE_I = 48
GATE_F = 56
N_PIECES = 3
ML_BIAS_ROWS = 2 * BF16_ROWS

ML_CHUNK = 256
FOX_TQ = 256
FOX_TK = 256
PROJ_TM = 256
OUT_TM = 256

_SRC = {}
_off = 0
for _name, _w in (("fq", FOX_WIDTH), ("fk", FOX_WIDTH), ("fv", FOX_WIDTH), ("ff", FOX_HEADS),
                  ("fz", FOX_WIDTH), ("mq", ML_WIDTH), ("mk", ML_WIDTH), ("mv", ML_WIDTH),
                  ("mi", ML_HEADS), ("mf", ML_HEADS), ("mo", ML_WIDTH), ("mz", ML_WIDTH),
                  ("rq", MEM_WIDTH), ("rz", MEM_WIDTH)):
    _SRC[_name] = (_off, _w)
    _off += _w


def _sigmoid(x):
    return 1.0 / (1.0 + jnp.exp(-x))


def _silu(x):
    return x * _sigmoid(x)


def _split3(x):
    hi = x.astype(BF16)
    r1 = x - hi.astype(F32)
    mid = r1.astype(BF16)
    lo = (r1 - mid.astype(F32)).astype(BF16)
    return hi, mid, lo


def _with_ones_rows(t, head_dim, heads):
    ones = jnp.ones((BF16_ROWS, t.shape[1]), t.dtype)
    rows = []
    for h in range(heads):
        rows += [t[h * head_dim:(h + 1) * head_dim, :], ones]
    return jnp.concatenate(rows, axis=0)


def _zeros_like_cols(t, n):
    return jnp.zeros(t.shape[:-1] + (n,), t.dtype)


def _pad_heads_last(t):
    parts = []
    for h in range(ML_HEADS):
        parts.append(t[..., h * ML_HEAD_DIM:(h + 1) * ML_HEAD_DIM])
        parts.append(_zeros_like_cols(t, ML_PAD - ML_HEAD_DIM))
    return jnp.concatenate(parts, axis=-1)


def _spread_gates(ff, mi, mf):
    parts = []
    for h in range(FOX_HEADS):
        parts += [ff[..., h:h + 1], _zeros_like_cols(ff, SUBLANES - 1)]
    parts += [mi, _zeros_like_cols(mi, GATE_F - GATE_I - ML_HEADS),
              mf, _zeros_like_cols(mf, G_COLS - GATE_F - ML_HEADS)]
    return jnp.concatenate(parts, axis=-1)


def _relayout_w_in(w):
    def take(name):
        o, n = _SRC[name]
        return w[..., o:o + n]

    cols = [_pad_heads_last(take("mq")), _pad_heads_last(take("mk")),
            take("mv"), take("mo"), take("mz"), take("fz"),
            take("fk"), take("fq") * (FOX_HEAD_DIM ** -0.5 * LOG2E),
            take("fv"), _spread_gates(take("ff"), take("mi"), take("mf")),
            take("rq"), take("rz")]
    return jnp.concatenate(cols, axis=-1).astype(BF16)


def _bias_selectors():
    n = N_PIECES * LANES
    selk = np.zeros((n, LANES), np.float32)
    selq = np.zeros((n, LANES), np.float32)
    selm = np.zeros((n, LANES), np.float32)
    onek = np.zeros((1, LANES), np.float32)
    oneq = np.zeros((1, LANES), np.float32)
    onem = np.zeros((1, LANES), np.float32)
    for h in range(FOX_HEADS):
        base = SUBLANES * h
        for j in range(N_PIECES):
            selk[j * LANES + base, base + j] = -1.0
            selq[j * LANES + base, base + N_PIECES + j] = 1.0
            onek[0, base + N_PIECES + j] = 1.0
            oneq[0, base + j] = 1.0
    for h in range(ML_HEADS):
        base = SUBLANES * h
        for j in range(N_PIECES):
            selm[j * LANES + GATE_I + h, base + j] = 1.0
            onem[0, base + N_PIECES + j] = 1.0
    return (jnp.asarray(selk, BF16), jnp.asarray(selq, BF16), jnp.asarray(selm, BF16),
            jnp.asarray(onek), jnp.asarray(oneq), jnp.asarray(onem))


def _proj_kernel(x_ref, w_ref, cw_ref, cb_ref,
                 kc_ref, qct_ref, mvt_ref, ogt_ref, zgt_ref, ufx_ref, umem_ref, g_ref,
                 fqt_ref, fvt_ref, tail_sc):
    tm = PROJ_TM

    @pl.when(pl.program_id(1) == 0)
    def _():
        tail_sc[...] = jnp.zeros_like(tail_sc)

    xb = x_ref[...].astype(BF16)

    def mm(c0, n):
        return jnp.dot(xb, w_ref[:, c0:c0 + n], preferred_element_type=F32)

    u = mm(COL_QK, 2 * ML_PW)
    tail = tail_sc[...]
    w = cw_ref[...]
    row8 = lax.broadcasted_iota(jnp.int32, (SUBLANES, 2 * ML_PW), 0)
    y = u * w[CONV_WIDTH - 1:CONV_WIDTH, :] + cb_ref[...]
    for shift in range(1, CONV_WIDTH):
        r = pltpu.roll(u, shift, axis=0)
        rt = pltpu.roll(tail, shift, axis=0)
        first = jnp.where(row8 < shift, rt, r[:SUBLANES])
        xs = jnp.concatenate([first, r[SUBLANES:]], axis=0)
        y = y + xs * w[CONV_WIDTH - 1 - shift:CONV_WIDTH - shift, :]
    tail_sc[...] = u[tm - SUBLANES:, :]
    a = _silu(y)
    qct_ref[...] = a[:, :ML_PW].T.astype(BF16)
    kc_ref[...] = (a[:, ML_PW:] * (ML_HEAD_DIM ** -0.5)).astype(BF16)

    t = mm(COL_MV, 2 * ML_WIDTH)
    mvt_ref[...] = _with_ones_rows(t[:, :ML_WIDTH].T.astype(BF16), ML_HEAD_DIM, ML_HEADS)
    ogt_ref[...] = _sigmoid(t[:, ML_WIDTH:]).T.astype(BF16)

    t = _silu(mm(COL_MZ, ML_WIDTH + FOX_WIDTH))
    zgt_ref[...] = t[:, :ML_WIDTH].T.astype(BF16)
    ufx_ref[:, FOX_WIDTH:] = t[:, ML_WIDTH:].astype(BF16)

    t = mm(COL_FK, 2 * FOX_WIDTH)
    ufx_ref[:, :FOX_WIDTH] = t[:, :FOX_WIDTH].astype(BF16)
    fqt_ref[...] = t[:, FOX_WIDTH:].T.astype(BF16)

    t = mm(COL_FV, FOX_WIDTH + G_COLS)
    fvt_ref[...] = _with_ones_rows(t[:, :FOX_WIDTH].T.astype(BF16), FOX_HEAD_DIM, FOX_HEADS)
    g_ref[...] = t[:, FOX_WIDTH:]

    t = mm(COL_RQ, 2 * MEM_WIDTH)
    umem_ref[:, :MEM_WIDTH] = t[:, :MEM_WIDTH].astype(BF16)
    umem_ref[:, MEM_WIDTH:] = _silu(t[:, MEM_WIDTH:]).astype(BF16)


def _project(x, w_re, conv_w, conv_b):
    b, s, _ = x.shape
    tm = PROJ_TM

    def row(n):
        return pl.BlockSpec((None, tm, n), lambda i, j: (i, j, 0))

    def col(n):
        return pl.BlockSpec((None, n, tm), lambda i, j: (i, 0, j))

    def const(shape):
        return pl.BlockSpec(shape, lambda i, j: (0, 0))

    def rows_shape(n):
        return jax.ShapeDtypeStruct((b, s, n), BF16)

    def cols_shape(n):
        return jax.ShapeDtypeStruct((b, n, s), BF16)

    return pl.pallas_call(
        _proj_kernel,
        grid=(b, s // tm),
        in_specs=[row(D_MODEL), const((D_MODEL, W_COLS)),
                  const((CONV_WIDTH, 2 * ML_PW)), const((1, 2 * ML_PW))],
        out_specs=[row(ML_PW), col(ML_PW), col(ML_HEADS * ML_VROWS), col(ML_WIDTH), col(ML_WIDTH),
                   row(2 * FOX_WIDTH), row(2 * MEM_WIDTH), row(G_COLS),
                   col(FOX_WIDTH), col(FOX_HEADS * FOX_VROWS)],
        out_shape=[rows_shape(ML_PW), cols_shape(ML_PW), cols_shape(ML_HEADS * ML_VROWS),
                   cols_shape(ML_WIDTH), cols_shape(ML_WIDTH),
                   rows_shape(2 * FOX_WIDTH), rows_shape(2 * MEM_WIDTH),
                   jax.ShapeDtypeStruct((b, s, G_COLS), F32),
                   cols_shape(FOX_WIDTH), cols_shape(FOX_HEADS * FOX_VROWS)],
        scratch_shapes=[pltpu.VMEM((SUBLANES, 2 * ML_PW), F32)],
        compiler_params=pltpu.CompilerParams(
            dimension_semantics=("arbitrary", "arbitrary"), vmem_limit_bytes=56 * 1024 * 1024),
        name="in_proj",
    )(x, w_re, conv_w, conv_b)


def _gates_kernel(g_ref, bias_ref, selk_ref, selq_ref, selm_ref, onek_ref, oneq_ref, onem_ref,
                  kb_ref, qbt_ref, maug_ref, mrow_ref, carry_ref):
    @pl.when(pl.program_id(1) == 0)
    def _():
        carry_ref[...] = jnp.zeros_like(carry_ref)

    L = ML_CHUNK
    x = g_ref[...] + bias_ref[...]
    log_sig = jnp.minimum(x, 0.0) - jnp.log1p(jnp.exp(-jnp.abs(x)))
    r = lax.broadcasted_iota(jnp.int32, (L, L), 0)
    c = lax.broadcasted_iota(jnp.int32, (L, L), 1)
    tri = (c <= r).astype(F32)
    local = jnp.dot(tri, log_sig, precision=lax.Precision.HIGHEST, preferred_element_type=F32)
    carry = carry_ref[0:1, :]
    c_run = local + carry
    carry_ref[0:1, :] = carry + local[L - 1:L, :]

    pieces = jnp.concatenate(_split3(c_run * LOG2E), axis=1)
    kb_ref[...] = (jnp.dot(pieces, selk_ref[...], preferred_element_type=F32)
                   + onek_ref[...]).astype(BF16)
    qb = jnp.dot(pieces, selq_ref[...], preferred_element_type=F32) + oneq_ref[...]
    qbt_ref[...] = qb.T.astype(BF16)

    rc = x - pltpu.roll(local, LANES - (GATE_F - GATE_I), axis=1)
    pieces = jnp.concatenate(_split3(rc), axis=1)
    maug_ref[...] = (jnp.dot(pieces, selm_ref[...], preferred_element_type=F32)
                     + onem_ref[...]).astype(BF16)
    lane = lax.broadcasted_iota(jnp.int32, (L, G_COLS), 1)
    rows = jnp.where(lane < GATE_F, rc, local).T[GATE_I:GATE_I + 2 * SUBLANES, :]
    r8 = rows[:SUBLANES]
    pos = lax.broadcasted_iota(jnp.int32, (SUBLANES, L), 1)
    cm = r8
    shift = 1
    while shift < L:
        cm = jnp.maximum(cm, jnp.where(pos >= shift, pltpu.roll(cm, shift, axis=1), -jnp.inf))
        shift *= 2
    mrow_ref[...] = jnp.concatenate([rows, cm], axis=0)


def _gates(g, bias_row, sels):
    b, s, _ = g.shape
    L = ML_CHUNK
    const = lambda shape: pl.BlockSpec(shape, lambda i, j: (0, 0))
    sel_shape = (N_PIECES * LANES, LANES)
    return pl.pallas_call(
        _gates_kernel,
        grid=(b, s // L),
        in_specs=[pl.BlockSpec((None, L, G_COLS), lambda i, j: (i, j, 0)),
                  const((1, G_COLS)),
                  const(sel_shape), const(sel_shape), const(sel_shape),
                  const((1, LANES)), const((1, LANES)), const((1, LANES))],
        out_specs=[pl.BlockSpec((None, L, LANES), lambda i, j: (i, j, 0)),
                   pl.BlockSpec((None, LANES, L), lambda i, j: (i, 0, j)),
                   pl.BlockSpec((None, L, LANES), lambda i, j: (i, j, 0)),
                   pl.BlockSpec((None, 3 * SUBLANES, L), lambda i, j: (i, 0, j))],
        out_shape=[jax.ShapeDtypeStruct((b, s, LANES), BF16),
                   jax.ShapeDtypeStruct((b, LANES, s), BF16),
                   jax.ShapeDtypeStruct((b, s, LANES), BF16),
                   jax.ShapeDtypeStruct((b, 3 * SUBLANES, s), F32)],
        scratch_shapes=[pltpu.VMEM((SUBLANES, G_COLS), F32)],
        compiler_params=pltpu.CompilerParams(dimension_semantics=("arbitrary", "arbitrary")),
        name="gate_prefix",
    )(g, bias_row, *sels)


def _fox_kernel(qt_ref, qbt_ref, k_ref, kb_ref, vt_ref, zg_ref, o_ref,
                qa_sc, sa_sc, sb_sc, m_sc, acc_sc):
    qi = pl.program_id(1)
    tq, tk = FOX_TQ, FOX_TK
    hd = FOX_HEAD_DIM

    brow = lax.broadcasted_iota(jnp.int32, (LANES, tq), 0)
    qbt = qbt_ref[...]
    zslot = jnp.zeros((hd, tq), BF16)
    for h in range(FOX_HEADS):
        qh = qt_ref[h * hd:(h + 1) * hd, :]
        top = [qh, zslot] if h % 2 == 0 else [zslot, qh]
        own = (brow >= SUBLANES * h) & (brow < SUBLANES * (h + 1))
        qa_sc[h] = jnp.concatenate(top + [jnp.where(own, qbt, jnp.zeros_like(qbt))], axis=0)

    m_sc[...] = jnp.full_like(m_sc, -jnp.inf)
    acc_sc[...] = jnp.zeros_like(acc_sc)

    def logits(j, h, dst):
        k0 = pl.multiple_of(j * tk, tk)
        p = h // 2
        kk = jnp.concatenate([k_ref[pl.ds(k0, tk), p * LANES:(p + 1) * LANES],
                              kb_ref[pl.ds(k0, tk), :]], axis=1)
        dst[h] = jnp.dot(kk, qa_sc[h], preferred_element_type=F32)

    def softmax_pv(j, h, src, masked):
        k0 = pl.multiple_of(j * tk, tk)
        st = src[h]
        if masked:
            kr = lax.broadcasted_iota(jnp.int32, (tk, tq), 0)
            qc = lax.broadcasted_iota(jnp.int32, (tk, tq), 1)
            st = jnp.where(kr <= qc, st, -jnp.inf)
        m_old = m_sc[h]
        m_new = jnp.maximum(m_old, jnp.max(st, axis=0, keepdims=True))
        alpha = jnp.exp2(m_old - m_new)
        pt = jnp.exp2(st - m_new).astype(BF16)
        vt = vt_ref[h * FOX_VROWS:(h + 1) * FOX_VROWS, pl.ds(k0, tk)]
        acc_sc[h] = alpha * acc_sc[h] + jnp.dot(vt, pt, preferred_element_type=F32)
        m_sc[h] = m_new

    def stage(j_sm, src, masked, j_qk, dst):
        for h in range(FOX_HEADS):
            if j_sm is not None:
                softmax_pv(j_sm, h, src, masked)
            if j_qk is not None:
                logits(j_qk, h, dst)

    stage(None, None, False, 0, sa_sc)

    def body(i, carry):
        stage(2 * i, sa_sc, False, 2 * i + 1, sb_sc)
        stage(2 * i + 1, sb_sc, False, 2 * i + 2, sa_sc)
        return carry

    lax.fori_loop(0, qi // 2, body, 0)

    @pl.when(qi % 2 == 0)
    def _():
        stage(qi, sa_sc, True, None, None)

    @pl.when(qi % 2 == 1)
    def _():
        stage(qi - 1, sa_sc, False, qi, sb_sc)
        stage(qi, sb_sc, True, None, None)

    out_t = jnp.concatenate(
        [acc_sc[h, :hd, :] / acc_sc[h, hd:hd + 1, :] for h in range(FOX_HEADS)], axis=0)
    o_ref[...] = (out_t.T * zg_ref[...].astype(F32)).astype(BF16)


def _fox(qt, qbt, ufx, kb, vt):
    b, s, _ = ufx.shape
    tq = FOX_TQ
    return pl.pallas_call(
        _fox_kernel,
        grid=(b, s // tq),
        in_specs=[
            pl.BlockSpec((None, FOX_WIDTH, tq), lambda i, j: (i, 0, j)),
            pl.BlockSpec((None, LANES, tq), lambda i, j: (i, 0, j)),
            pl.BlockSpec((None, s, FOX_WIDTH), lambda i, j: (i, 0, 0)),
            pl.BlockSpec((None, s, LANES), lambda i, j: (i, 0, 0)),
            pl.BlockSpec((None, FOX_HEADS * FOX_VROWS, s), lambda i, j: (i, 0, 0)),
            pl.BlockSpec((None, tq, FOX_WIDTH), lambda i, j: (i, j, 1)),
        ],
        out_specs=pl.BlockSpec((None, tq, FOX_WIDTH), lambda i, j: (i, j, 0)),
        out_shape=jax.ShapeDtypeStruct((b, s, FOX_WIDTH), BF16),
        scratch_shapes=[pltpu.VMEM((FOX_HEADS, 2 * LANES, tq), BF16),
                        pltpu.VMEM((FOX_HEADS, FOX_TK, tq), F32),
                        pltpu.VMEM((FOX_HEADS, FOX_TK, tq), F32),
                        pltpu.VMEM((FOX_HEADS, 1, tq), F32),
                        pltpu.VMEM((FOX_HEADS, FOX_VROWS, tq), F32)],
        compiler_params=pltpu.CompilerParams(
            dimension_semantics=("arbitrary", "arbitrary"), vmem_limit_bytes=48 * 1024 * 1024),
        name="fox_attn",
    )(qt, qbt, ufx, kb, vt, ufx)


def _mlstm_kernel(kc_ref, qct_ref, mvt_ref, ogt_ref, zgt_ref, maug_ref, mrow_ref, ng_ref,
                  y_ref, c_sc, m_sc):
    L = ML_CHUNK
    dv = ML_HEAD_DIM

    @pl.when(pl.program_id(1) == 0)
    def _():
        c_sc[...] = jnp.zeros_like(c_sc)
        m_sc[...] = jnp.zeros_like(m_sc)

    kr = lax.broadcasted_iota(jnp.int32, (L, L), 0)
    qc = lax.broadcasted_iota(jnp.int32, (L, L), 1)
    visible = kr <= qc
    aug = maug_ref[...]
    brow = lax.broadcasted_iota(jnp.int32, (ML_BIAS_ROWS, L), 0)
    bzero = jnp.zeros((LANES - ML_BIAS_ROWS, L), BF16)

    heads = []
    for h in range(ML_HEADS):
        r = mrow_ref[h:h + 1, :]
        b = mrow_ref[SUBLANES + h:SUBLANES + h + 1, :]
        cm = mrow_ref[2 * SUBLANES + h:2 * SUBLANES + h + 1, :]
        m_prev = m_sc[h, 0:1, 0:1]
        g = jnp.maximum(m_prev, cm)
        g_last = g[:, L - 1:L]
        g_hi, g_mid, g_lo = (t.astype(F32) for t in _split3(g))
        rel = brow - SUBLANES * h
        blk = jnp.where(rel == N_PIECES, -g_hi,
                        jnp.where(rel == N_PIECES + 1, -g_mid,
                                  jnp.where(rel == N_PIECES + 2, -g_lo,
                                            jnp.where((rel >= 0) & (rel < N_PIECES), 1.0, 0.0))))
        bmat = jnp.concatenate([blk.astype(BF16), bzero], axis=0)
        k_h = kc_ref[:, h * ML_PAD:(h + 1) * ML_PAD]
        q_h = qct_ref[h * ML_PAD:(h + 1) * ML_PAD, :]
        ct = c_sc[h]
        z = jnp.dot(aug, bmat, preferred_element_type=F32)
        s_raw = jnp.dot(k_h, q_h, preferred_element_type=F32)
        inter = jnp.dot(ct.astype(BF16), q_h, preferred_element_type=F32)
        heads.append((r, b, g, g_last, m_prev, k_h, ct, z, s_raw, inter))

    outs = []
    for h in range(ML_HEADS):
        r, b, g, g_last, m_prev, k_h, ct, z, s_raw, inter = heads[h]
        d = jnp.exp(jnp.where(visible, z, -jnp.inf))
        p = (s_raw * d).astype(BF16)
        vt = mvt_ref[h * ML_VROWS:(h + 1) * ML_VROWS, :]
        num = jnp.dot(vt, p, preferred_element_type=F32) + jnp.exp(m_prev - g) * inter
        den = num[dv:dv + 1, :]
        hv = num[:dv, :] / jnp.maximum(jnp.abs(den), jnp.exp(-(b + g)))

        wv = (vt.astype(F32) * jnp.exp(r - g_last)).astype(BF16)
        c_sc[h] = jnp.exp(m_prev - g_last) * ct + jnp.dot(wv, k_h, preferred_element_type=F32)
        m_sc[h] = jnp.broadcast_to(b[:, L - 1:L] + g_last, (SUBLANES, LANES))

        hv = hv * ogt_ref[h * dv:(h + 1) * dv, :].astype(F32)
        mu = jnp.mean(hv, axis=0, keepdims=True)
        dd = hv - mu
        var = jnp.mean(dd * dd, axis=0, keepdims=True)
        hn = dd * lax.rsqrt(var + LN_EPS) * ng_ref[h * dv:(h + 1) * dv, :]
        outs.append(hn * zgt_ref[h * dv:(h + 1) * dv, :].astype(F32))
    y_ref[...] = jnp.concatenate(outs, axis=0).T.astype(BF16)


def _mlstm(kc, qct, mvt, ogt, zgt, maug, mrow, norm_g_b):
    b, s, _ = kc.shape
    L = ML_CHUNK

    def col(n):
        return pl.BlockSpec((None, n, L), lambda i, j: (i, 0, j))

    def row(n):
        return pl.BlockSpec((None, L, n), lambda i, j: (i, j, 0))

    return pl.pallas_call(
        _mlstm_kernel,
        grid=(b, s // L),
        in_specs=[row(ML_PW), col(ML_PW), col(ML_HEADS * ML_VROWS), col(ML_WIDTH), col(ML_WIDTH),
                  row(LANES), col(3 * SUBLANES),
                  pl.BlockSpec((ML_WIDTH, L), lambda i, j: (0, 0))],
        out_specs=row(ML_WIDTH),
        out_shape=jax.ShapeDtypeStruct((b, s, ML_WIDTH), BF16),
        scratch_shapes=[pltpu.VMEM((ML_HEADS, ML_VROWS, LANES), F32),
                        pltpu.VMEM((ML_HEADS, SUBLANES, LANES), F32)],
        compiler_params=pltpu.CompilerParams(
            dimension_semantics=("arbitrary", "arbitrary"), vmem_limit_bytes=48 * 1024 * 1024),
        name="mlstm",
    )(kc, qct, mvt, ogt, zgt, maug, mrow, norm_g_b)


def _memkv_kernel(m_ref, w_ref, o_ref):
    o_ref[...] = jnp.dot(m_ref[...].astype(BF16), w_ref[...].astype(BF16),
                         preferred_element_type=F32).astype(BF16)


def _memkv(mem, w_mem_kv):
    b, ml, _ = mem.shape
    return pl.pallas_call(
        _memkv_kernel,
        grid=(DEPTH, b),
        in_specs=[pl.BlockSpec((None, ml, D_MODEL), lambda l, i: (i, 0, 0)),
                  pl.BlockSpec((None, D_MODEL, 2 * MEM_WIDTH), lambda l, i: (l, 0, 0))],
        out_specs=pl.BlockSpec((None, None, ml, 2 * MEM_WIDTH), lambda l, i: (l, i, 0, 0)),
        out_shape=jax.ShapeDtypeStruct((DEPTH, b, ml, 2 * MEM_WIDTH), BF16),
        compiler_params=pltpu.CompilerParams(dimension_semantics=("arbitrary", "arbitrary")),
        name="mem_kv",
    )(mem, w_mem_kv)


def _out_kernel(x_ref, yf_ref, ym_ref, rq_ref, rzg_ref, mk_ref, mv_ref, wo_ref, g_ref, b_ref, o_ref):
    rq = rq_ref[...]
    mk = mk_ref[...]
    mv = mv_ref[...]
    lane = lax.broadcasted_iota(jnp.int32, (1, MEM_WIDTH), 1)
    scale = jnp.asarray(MEM_HEAD_DIM ** -0.5, BF16)
    att = jnp.zeros((OUT_TM, MEM_WIDTH), F32)
    for h in range(MEM_HEADS):
        msk = (lane >= MEM_HEAD_DIM * h) & (lane < MEM_HEAD_DIM * (h + 1))
        qh = jnp.where(msk, rq, jnp.zeros_like(rq)) * scale
        s = lax.dot_general(qh, mk, (((1,), (1,)), ((), ())), preferred_element_type=F32)
        p = jnp.exp(s - jnp.max(s, axis=1, keepdims=True))
        l = jnp.sum(p, axis=1, keepdims=True)
        vh = jnp.where(msk, mv, jnp.zeros_like(mv))
        att = att + jnp.dot(p.astype(BF16), vh, preferred_element_type=F32) / l
    y_mem = (att * rzg_ref[...].astype(F32)).astype(BF16)
    o_fox, o_ml = FOX_WIDTH, FOX_WIDTH + ML_WIDTH
    y = (jnp.dot(yf_ref[...], wo_ref[0:o_fox, :], preferred_element_type=F32)
         + jnp.dot(ym_ref[...], wo_ref[o_fox:o_ml, :], preferred_element_type=F32)
         + jnp.dot(y_mem, wo_ref[o_ml:, :], preferred_element_type=F32))
    r = DEEPNORM_ALPHA * x_ref[...] + y
    mu = jnp.mean(r, axis=1, keepdims=True)
    d = r - mu
    var = jnp.mean(d * d, axis=1, keepdims=True)
    o_ref[...] = d * lax.rsqrt(var + LN_EPS) * g_ref[...] + b_ref[...]


def _out_proj(x, y_fox, y_ml, umem, memkv, layer, w_out, ln_g, ln_b):
    b, s, _ = x.shape
    tm = OUT_TM
    ml = memkv.shape[2]
    return pl.pallas_call(
        _out_kernel,
        grid=(b, s // tm),
        in_specs=[
            pl.BlockSpec((None, tm, D_MODEL), lambda i, j: (i, j, 0)),
            pl.BlockSpec((None, tm, FOX_WIDTH), lambda i, j: (i, j, 0)),
            pl.BlockSpec((None, tm, ML_WIDTH), lambda i, j: (i, j, 0)),
            pl.BlockSpec((None, tm, MEM_WIDTH), lambda i, j: (i, j, 0)),
            pl.BlockSpec((None, tm, MEM_WIDTH), lambda i, j: (i, j, 1)),
            pl.BlockSpec((None, None, ml, MEM_WIDTH), lambda i, j: (layer, i, 0, 0)),
            pl.BlockSpec((None, None, ml, MEM_WIDTH), lambda i, j: (layer, i, 0, 1)),
            pl.BlockSpec((None, D_MODEL, D_MODEL), lambda i, j: (layer, 0, 0)),
            pl.BlockSpec((None, 1, D_MODEL), lambda i, j: (layer, 0, 0)),
            pl.BlockSpec((None, 1, D_MODEL), lambda i, j: (layer, 0, 0)),
        ],
        out_specs=pl.BlockSpec((None, tm, D_MODEL), lambda i, j: (i, j, 0)),
        out_shape=jax.ShapeDtypeStruct((b, s, D_MODEL), F32),
        compiler_params=pltpu.CompilerParams(
            dimension_semantics=("arbitrary", "arbitrary"), vmem_limit_bytes=48 * 1024 * 1024),
        name="out_proj_ln",
    )(x, y_fox, y_ml, umem, umem, memkv, memkv, w_out, ln_g, ln_b)


def kernel(x, mem, w_in, fox_f_bias, mlstm_conv_w, mlstm_conv_b, mlstm_i_bias, mlstm_f_bias,
           mlstm_norm_g, w_mem_kv, w_out, ln_g, ln_b):
    w_re = _relayout_w_in(w_in)
    w_out_b = w_out.astype(BF16)
    bias_rows = _spread_gates(fox_f_bias, mlstm_i_bias, mlstm_f_bias)[:, None, :]
    conv_w = jnp.concatenate([_pad_heads_last(mlstm_conv_w[..., :ML_WIDTH]),
                              _pad_heads_last(mlstm_conv_w[..., ML_WIDTH:])], axis=-1)
    conv_b = jnp.concatenate([_pad_heads_last(mlstm_conv_b[..., :ML_WIDTH]),
                              _pad_heads_last(mlstm_conv_b[..., ML_WIDTH:])], axis=-1)[:, None, :]
    norm_g_b = jnp.broadcast_to(mlstm_norm_g[:, :, None], (DEPTH, ML_WIDTH, ML_CHUNK))
    ln_g3 = ln_g[:, None, :]
    ln_b3 = ln_b[:, None, :]
    sels = _bias_selectors()

    memkv = _memkv(mem, w_mem_kv)
    for l in range(DEPTH):
        kc, qct, mvt, ogt, zgt, ufx, umem, g, fqt, fvt = _project(x, w_re[l], conv_w[l], conv_b[l])
        kb, qbt, maug, mrow = _gates(g, bias_rows[l], sels)
        y_fox = _fox(fqt, qbt, ufx, kb, fvt)
        y_ml = _mlstm(kc, qct, mvt, ogt, zgt, maug, mrow, norm_g_b[l])
        x = _out_proj(x, y_fox, y_ml, umem, memkv, l, w_out_b, ln_g3, ln_b3)
    return x
```

```python
import math

import numpy as np
import jax
import jax.numpy as jnp
from jax import lax
from jax.experimental import pallas as pl
from jax.experimental.pallas import tpu as pltpu

F32 = jnp.float32
BF16 = jnp.bfloat16

D_MODEL = 1024
DEPTH = 4
FOX_HEADS = 6
FOX_HEAD_DIM = 64
FOX_WIDTH = FOX_HEADS * FOX_HEAD_DIM
ML_HEADS = 4
ML_HEAD_DIM = 96
ML_WIDTH = ML_HEADS * ML_HEAD_DIM
MEM_HEADS = 4
MEM_HEAD_DIM = 64
MEM_WIDTH = MEM_HEADS * MEM_HEAD_DIM
CONV_WIDTH = 4
LN_EPS = 1e-5
DEEPNORM_ALPHA = (2.0 * DEPTH) ** 0.25
LOG2E = math.log2(math.e)

LANES = 128
SUBLANES = 8
BF16_ROWS = 16
ML_PAD = LANES
ML_PW = ML_HEADS * ML_PAD
ML_VROWS = ML_HEAD_DIM + BF16_ROWS
FOX_VROWS = FOX_HEAD_DIM + BF16_ROWS

COL_QK = 0
COL_MV = COL_QK + 2 * ML_PW
COL_MZ = COL_MV + 2 * ML_WIDTH
COL_FK = COL_MZ + ML_WIDTH + FOX_WIDTH
COL_FV = COL_FK + 2 * FOX_WIDTH
COL_RQ = COL_FV + FOX_WIDTH + LANES
W_COLS = COL_RQ + 2 * MEM_WIDTH
G_COLS = LANES
GATE_I = 48
GATE_F = 56
N_PIECES = 3
ML_BIAS_ROWS = 2 * BF16_ROWS

ML_CHUNK = 256
FOX_TQ = 512
FOX_TK = 256
PROJ_TM = 256
OUT_TM = 256

_SRC = {}
_off = 0
for _name, _w in (("fq", FOX_WIDTH), ("fk", FOX_WIDTH), ("fv", FOX_WIDTH), ("ff", FOX_HEADS),
                  ("fz", FOX_WIDTH), ("mq", ML_WIDTH), ("mk", ML_WIDTH), ("mv", ML_WIDTH),
                  ("mi", ML_HEADS), ("mf", ML_HEADS), ("mo", ML_WIDTH), ("mz", ML_WIDTH),
                  ("rq", MEM_WIDTH), ("rz", MEM_WIDTH)):
    _SRC[_name] = (_off, _w)
    _off += _w


def _sigmoid(x):
    return 1.0 / (1.0 + jnp.exp(-x))


def _silu(x):
    return x * _sigmoid(x)


def _split3(x):
    hi = x.astype(BF16)
    r1 = x - hi.astype(F32)
    mid = r1.astype(BF16)
    lo = (r1 - mid.astype(F32)).astype(BF16)
    return hi, mid, lo


def _with_ones_rows(t, head_dim, heads):
    ones = jnp.ones((BF16_ROWS, t.shape[1]), t.dtype)
    rows = []
    for h in range(heads):
        rows += [t[h * head_dim:(h + 1) * head_dim, :], ones]
    return jnp.concatenate(rows, axis=0)


def _zeros_like_cols(t, n):
    return jnp.zeros(t.shape[:-1] + (n,), t.dtype)


def _pad_heads_last(t):
    parts = []
    for h in range(ML_HEADS):
        parts.append(t[..., h * ML_HEAD_DIM:(h + 1) * ML_HEAD_DIM])
        parts.append(_zeros_like_cols(t, ML_PAD - ML_HEAD_DIM))
    return jnp.concatenate(parts, axis=-1)


def _spread_gates(ff, mi, mf):
    parts = []
    for h in range(FOX_HEADS):
        parts += [ff[..., h:h + 1], _zeros_like_cols(ff, SUBLANES - 1)]
    parts += [mi, _zeros_like_cols(mi, GATE_F - GATE_I - ML_HEADS),
              mf, _zeros_like_cols(mf, G_COLS - GATE_F - ML_HEADS)]
    return jnp.concatenate(parts, axis=-1)


def _relayout_w_in(w):
    def take(name):
        o, n = _SRC[name]
        return w[..., o:o + n]

    cols = [_pad_heads_last(take("mq")), _pad_heads_last(take("mk")),
            take("mv"), take("mo"), take("mz"), take("fz"),
            take("fk"), take("fq") * (FOX_HEAD_DIM ** -0.5 * LOG2E),
            take("fv"), _spread_gates(take("ff"), take("mi"), take("mf")),
            take("rq"), take("rz")]
    return jnp.concatenate(cols, axis=-1).astype(BF16)


def _bias_selectors():
    n = N_PIECES * LANES
    selk = np.zeros((n, LANES), np.float32)
    selq = np.zeros((n, LANES), np.float32)
    selm = np.zeros((n, LANES), np.float32)
    onek = np.zeros((1, LANES), np.float32)
    oneq = np.zeros((1, LANES), np.float32)
    onem = np.zeros((1, LANES), np.float32)
    for h in range(FOX_HEADS):
        base = SUBLANES * h
        for j in range(N_PIECES):
            selk[j * LANES + base, base + j] = -1.0
            selq[j * LANES + base, base + N_PIECES + j] = 1.0
            onek[0, base + N_PIECES + j] = 1.0
            oneq[0, base + j] = 1.0
    for h in range(ML_HEADS):
        base = SUBLANES * h
        for j in range(N_PIECES):
            selm[j * LANES + GATE_I + h, base + j] = 1.0
            onem[0, base + N_PIECES + j] = 1.0
    return (jnp.asarray(selk, BF16), jnp.asarray(selq, BF16), jnp.asarray(selm, BF16),
            jnp.asarray(onek), jnp.asarray(oneq), jnp.asarray(onem))


def _proj_kernel(x_ref, w_ref, cw_ref, cb_ref,
                 kc_ref, qct_ref, mvt_ref, ogt_ref, zgt_ref, ufx_ref, umem_ref, g_ref,
                 fqt_ref, fvt_ref, tail_sc):
    tm = PROJ_TM

    @pl.when(pl.program_id(1) == 0)
    def _():
        tail_sc[...] = jnp.zeros_like(tail_sc)

    xb = x_ref[...].astype(BF16)

    def mm(c0, n):
        return jnp.dot(xb, w_ref[:, c0:c0 + n], preferred_element_type=F32)

    u = mm(COL_QK, 2 * ML_PW)
    tail = tail_sc[...]
    w = cw_ref[...]
    row8 = lax.broadcasted_iota(jnp.int32, (SUBLANES, 2 * ML_PW), 0)
    y = u * w[CONV_WIDTH - 1:CONV_WIDTH, :] + cb_ref[...]
    for shift in range(1, CONV_WIDTH):
        r = pltpu.roll(u, shift, axis=0)
        rt = pltpu.roll(tail, shift, axis=0)
        first = jnp.where(row8 < shift, rt, r[:SUBLANES])
        xs = jnp.concatenate([first, r[SUBLANES:]], axis=0)
        y = y + xs * w[CONV_WIDTH - 1 - shift:CONV_WIDTH - shift, :]
    tail_sc[...] = u[tm - SUBLANES:, :]
    a = _silu(y)
    qct_ref[...] = a[:, :ML_PW].T.astype(BF16)
    kc_ref[...] = (a[:, ML_PW:] * (ML_HEAD_DIM ** -0.5)).astype(BF16)

    t = mm(COL_MV, 2 * ML_WIDTH)
    mvt_ref[...] = _with_ones_rows(t[:, :ML_WIDTH].T.astype(BF16), ML_HEAD_DIM, ML_HEADS)
    ogt_ref[...] = _sigmoid(t[:, ML_WIDTH:]).T.astype(BF16)

    t = _silu(mm(COL_MZ, ML_WIDTH + FOX_WIDTH))
    zgt_ref[...] = t[:, :ML_WIDTH].T.astype(BF16)
    ufx_ref[:, FOX_WIDTH:] = t[:, ML_WIDTH:].astype(BF16)

    t = mm(COL_FK, 2 * FOX_WIDTH)
    ufx_ref[:, :FOX_WIDTH] = t[:, :FOX_WIDTH].astype(BF16)
    fqt_ref[...] = t[:, FOX_WIDTH:].T.astype(BF16)

    t = mm(COL_FV, FOX_WIDTH + G_COLS)
    fvt_ref[...] = _with_ones_rows(t[:, :FOX_WIDTH].T.astype(BF16), FOX_HEAD_DIM, FOX_HEADS)
    g_ref[...] = t[:, FOX_WIDTH:]

    t = mm(COL_RQ, 2 * MEM_WIDTH)
    umem_ref[:, :MEM_WIDTH] = t[:, :MEM_WIDTH].astype(BF16)
    umem_ref[:, MEM_WIDTH:] = _silu(t[:, MEM_WIDTH:]).astype(BF16)


def _project(x, w_re, conv_w, conv_b):
    b, s, _ = x.shape
    tm = PROJ_TM

    def row(n):
        return pl.BlockSpec((None, tm, n), lambda i, j: (i, j, 0))

    def col(n):
        return pl.BlockSpec((None, n, tm), lambda i, j: (i, 0, j))

    def const(shape):
        return pl.BlockSpec(shape, lambda i, j: (0, 0))

    def rows_shape(n):
        return jax.ShapeDtypeStruct((b, s, n), BF16)

    def cols_shape(n):
        return jax.ShapeDtypeStruct((b, n, s), BF16)

    return pl.pallas_call(
        _proj_kernel,
        grid=(b, s // tm),
        in_specs=[row(D_MODEL), const((D_MODEL, W_COLS)),
                  const((CONV_WIDTH, 2 * ML_PW)), const((1, 2 * ML_PW))],
        out_specs=[row(ML_PW), col(ML_PW), col(ML_HEADS * ML_VROWS), col(ML_WIDTH), col(ML_WIDTH),
                   row(2 * FOX_WIDTH), row(2 * MEM_WIDTH), row(G_COLS),
                   col(FOX_WIDTH), col(FOX_HEADS * FOX_VROWS)],
        out_shape=[rows_shape(ML_PW), cols_shape(ML_PW), cols_shape(ML_HEADS * ML_VROWS),
                   cols_shape(ML_WIDTH), cols_shape(ML_WIDTH),
                   rows_shape(2 * FOX_WIDTH), rows_shape(2 * MEM_WIDTH),
                   jax.ShapeDtypeStruct((b, s, G_COLS), F32),
                   cols_shape(FOX_WIDTH), cols_shape(FOX_HEADS * FOX_VROWS)],
        scratch_shapes=[pltpu.VMEM((SUBLANES, 2 * ML_PW), F32)],
        compiler_params=pltpu.CompilerParams(
            dimension_semantics=("arbitrary", "arbitrary"), vmem_limit_bytes=56 * 1024 * 1024),
        name="in_proj",
    )(x, w_re, conv_w, conv_b)


def _gates_kernel(g_ref, bias_ref, selk_ref, selq_ref, selm_ref, onek_ref, oneq_ref, onem_ref,
                  kb_ref, qbt_ref, maug_ref, mrow_ref, carry_ref):
    @pl.when(pl.program_id(0) == 0)
    def _():
        carry_ref[...] = jnp.zeros_like(carry_ref)

    L = ML_CHUNK
    r = lax.broadcasted_iota(jnp.int32, (L, L), 0)
    c = lax.broadcasted_iota(jnp.int32, (L, L), 1)
    tri = (c <= r).astype(BF16)
    lane = lax.broadcasted_iota(jnp.int32, (L, G_COLS), 1)
    pos = lax.broadcasted_iota(jnp.int32, (SUBLANES, L), 1)

    for bi in range(g_ref.shape[0]):
        x = g_ref[bi] + bias_ref[...]
        log_sig = jnp.minimum(x, 0.0) - jnp.log1p(jnp.exp(-jnp.abs(x)))
        parts = jnp.dot(tri, jnp.concatenate(_split3(log_sig), axis=1),
                        preferred_element_type=F32)
        local = (parts[:, :LANES] + parts[:, LANES:2 * LANES]) + parts[:, 2 * LANES:]
        carry = carry_ref[bi, 0:1, :]
        c_run = local + carry
        carry_ref[bi, 0:1, :] = carry + local[L - 1:L, :]

        pieces = jnp.concatenate(_split3(c_run * LOG2E), axis=1)
        kb_ref[bi] = (jnp.dot(pieces, selk_ref[...], preferred_element_type=F32)
                      + onek_ref[...]).astype(BF16)
        qb = jnp.dot(pieces, selq_ref[...], preferred_element_type=F32) + oneq_ref[...]
        qbt_ref[bi] = qb.T.astype(BF16)

        rc = x - pltpu.roll(local, LANES - (GATE_F - GATE_I), axis=1)
        pieces = jnp.concatenate(_split3(rc), axis=1)
        maug_ref[bi] = (jnp.dot(pieces, selm_ref[...], preferred_element_type=F32)
                        + onem_ref[...]).astype(BF16)
        rows = jnp.where(lane < GATE_F, rc, local).T[GATE_I:GATE_I + 2 * SUBLANES, :]
        cm = rows[:SUBLANES]
        shift = 1
        while shift < L:
            cm = jnp.maximum(cm, jnp.where(pos >= shift, pltpu.roll(cm, shift, axis=1), -jnp.inf))
            shift *= 2
        mrow_ref[bi] = jnp.concatenate([rows, cm], axis=0)


def _gates(g, bias_row, sels):
    b, s, _ = g.shape
    L = ML_CHUNK
    const = lambda shape: pl.BlockSpec(shape, lambda j: (0, 0))
    sel_shape = (N_PIECES * LANES, LANES)
    return pl.pallas_call(
        _gates_kernel,
        grid=(s // L,),
        in_specs=[pl.BlockSpec((b, L, G_COLS), lambda j: (0, j, 0)),
                  const((1, G_COLS)),
                  const(sel_shape), const(sel_shape), const(sel_shape),
                  const((1, LANES)), const((1, LANES)), const((1, LANES))],
        out_specs=[pl.BlockSpec((b, L, LANES), lambda j: (0, j, 0)),
                   pl.BlockSpec((b, LANES, L), lambda j: (0, 0, j)),
                   pl.BlockSpec((b, L, LANES), lambda j: (0, j, 0)),
                   pl.BlockSpec((b, 3 * SUBLANES, L), lambda j: (0, 0, j))],
        out_shape=[jax.ShapeDtypeStruct((b, s, LANES), BF16),
                   jax.ShapeDtypeStruct((b, LANES, s), BF16),
                   jax.ShapeDtypeStruct((b, s, LANES), BF16),
                   jax.ShapeDtypeStruct((b, 3 * SUBLANES, s), F32)],
        scratch_shapes=[pltpu.VMEM((b, SUBLANES, G_COLS), F32)],
        compiler_params=pltpu.CompilerParams(dimension_semantics=("arbitrary",)),
        name="gate_prefix",
    )(g, bias_row, *sels)


def _fox_kernel(qt_ref, qbt_ref, k_ref, kb_ref, vt_ref, zg_ref, o_ref,
                qa_sc, sa_sc, sb_sc, mxa_sc, mxb_sc, m_sc, acc_sc):
    qi = pl.program_id(1)
    tq, tk = FOX_TQ, FOX_TK
    hd = FOX_HEAD_DIM

    brow = lax.broadcasted_iota(jnp.int32, (LANES, tq), 0)
    qbt = qbt_ref[...]
    zslot = jnp.zeros((hd, tq), BF16)
    for h in range(FOX_HEADS):
        qh = qt_ref[h * hd:(h + 1) * hd, :]
        top = [qh, zslot] if h % 2 == 0 else [zslot, qh]
        own = (brow >= SUBLANES * h) & (brow < SUBLANES * (h + 1))
        qa_sc[h] = jnp.concatenate(top + [jnp.where(own, qbt, jnp.zeros_like(qbt))], axis=0)

    m_sc[...] = jnp.full_like(m_sc, -jnp.inf)
    acc_sc[...] = jnp.zeros_like(acc_sc)

    def logits(j, h, dst, mxd, key_offset):
        k0 = pl.multiple_of(j * tk, tk)
        p = h // 2
        kk = jnp.concatenate([k_ref[pl.ds(k0, tk), p * LANES:(p + 1) * LANES],
                              kb_ref[pl.ds(k0, tk), :]], axis=1)
        st = jnp.dot(kk, qa_sc[h], preferred_element_type=F32)
        if key_offset is not None:
            kr = lax.broadcasted_iota(jnp.int32, (tk, tq), 0)
            qc = lax.broadcasted_iota(jnp.int32, (tk, tq), 1)
            st = jnp.where(kr + key_offset <= qc, st, -jnp.inf)
        dst[h] = st
        mxd[h] = jnp.max(st, axis=0, keepdims=True)

    def softmax_pv(j, h, src, mxs):
        k0 = pl.multiple_of(j * tk, tk)
        m_old = m_sc[h]
        m_new = jnp.maximum(m_old, mxs[h])
        alpha = jnp.exp2(m_old - m_new)
        pt = jnp.exp2(src[h] - m_new).astype(BF16)
        vt = vt_ref[h * FOX_VROWS:(h + 1) * FOX_VROWS, pl.ds(k0, tk)]
        acc_sc[h] = alpha * acc_sc[h] + jnp.dot(vt, pt, preferred_element_type=F32)
        m_sc[h] = m_new

    buf_a = (sa_sc, mxa_sc)
    buf_b = (sb_sc, mxb_sc)

    def stage(j_qk, dst, key_offset, j_sm, src):
        if j_qk is not None:
            logits(j_qk, 0, *dst, key_offset)
        for h in range(FOX_HEADS):
            if j_qk is not None and h + 1 < FOX_HEADS:
                logits(j_qk, h + 1, *dst, key_offset)
            if j_sm is not None:
                softmax_pv(j_sm, h, *src)

    stage(2 * qi, buf_a, 0, None, None)
    stage(2 * qi + 1, buf_b, tk, 2 * qi, buf_a)
    stage(0, buf_a, None, 2 * qi + 1, buf_b)

    def body(i, carry):
        stage(2 * i + 1, buf_b, None, 2 * i, buf_a)
        stage(2 * i + 2, buf_a, None, 2 * i + 1, buf_b)
        return carry

    lax.fori_loop(0, qi, body, 0)

    out_t = jnp.concatenate(
        [acc_sc[h, :hd, :] / acc_sc[h, hd:hd + 1, :] for h in range(FOX_HEADS)], axis=0)
    o_ref[...] = (out_t.T * zg_ref[...].astype(F32)).astype(BF16)


def _fox(qt, qbt, ufx, kb, vt):
    b, s, _ = ufx.shape
    tq = FOX_TQ
    return pl.pallas_call(
        _fox_kernel,
        grid=(b, s // tq),
        in_specs=[
            pl.BlockSpec((None, FOX_WIDTH, tq), lambda i, j: (i, 0, j)),
            pl.BlockSpec((None, LANES, tq), lambda i, j: (i, 0, j)),
            pl.BlockSpec((None, s, FOX_WIDTH), lambda i, j: (i, 0, 0)),
            pl.BlockSpec((None, s, LANES), lambda i, j: (i, 0, 0)),
            pl.BlockSpec((None, FOX_HEADS * FOX_VROWS, s), lambda i, j: (i, 0, 0)),
            pl.BlockSpec((None, tq, FOX_WIDTH), lambda i, j: (i, j, 1)),
        ],
        out_specs=pl.BlockSpec((None, tq, FOX_WIDTH), lambda i, j: (i, j, 0)),
        out_shape=jax.ShapeDtypeStruct((b, s, FOX_WIDTH), BF16),
        scratch_shapes=[pltpu.VMEM((FOX_HEADS, 2 * LANES, tq), BF16),
                        pltpu.VMEM((FOX_HEADS, FOX_TK, tq), F32),
                        pltpu.VMEM((FOX_HEADS, FOX_TK, tq), F32),
                        pltpu.VMEM((FOX_HEADS, 1, tq), F32),
                        pltpu.VMEM((FOX_HEADS, 1, tq), F32),
                        pltpu.VMEM((FOX_HEADS, 1, tq), F32),
                        pltpu.VMEM((FOX_HEADS, FOX_VROWS, tq), F32)],
        compiler_params=pltpu.CompilerParams(
            dimension_semantics=("arbitrary", "arbitrary"), vmem_limit_bytes=48 * 1024 * 1024),
        name="fox_attn",
    )(qt, qbt, ufx, kb, vt, ufx)


def _mlstm_kernel(kc_ref, qct_ref, mvt_ref, ogt_ref, zgt_ref, maug_ref, mrow_ref, ng_ref,
                  y_ref, c_sc, m_sc):
    L = ML_CHUNK
    dv = ML_HEAD_DIM

    @pl.when(pl.program_id(1) == 0)
    def _():
        c_sc[...] = jnp.zeros_like(c_sc)
        m_sc[...] = jnp.zeros_like(m_sc)

    kr = lax.broadcasted_iota(jnp.int32, (L, L), 0)
    qc = lax.broadcasted_iota(jnp.int32, (L, L), 1)
    visible = kr <= qc
    aug = maug_ref[...]
    brow = lax.broadcasted_iota(jnp.int32, (ML_BIAS_ROWS, L), 0)
    bzero = jnp.zeros((LANES - ML_BIAS_ROWS, L), BF16)

    heads = []
    for h in range(ML_HEADS):
        r = mrow_ref[h:h + 1, :]
        b = mrow_ref[SUBLANES + h:SUBLANES + h + 1, :]
        cm = mrow_ref[2 * SUBLANES + h:2 * SUBLANES + h + 1, :]
        m_prev = m_sc[h, 0:1, 0:1]
        g = jnp.maximum(m_prev, cm)
        g_last = g[:, L - 1:L]
        g_hi, g_mid, g_lo = (t.astype(F32) for t in _split3(g))
        rel = brow - SUBLANES * h
        blk = jnp.where(rel == N_PIECES, -g_hi,
                        jnp.where(rel == N_PIECES + 1, -g_mid,
                                  jnp.where(rel == N_PIECES + 2, -g_lo,
                                            jnp.where((rel >= 0) & (rel < N_PIECES), 1.0, 0.0))))
        bmat = jnp.concatenate([blk.astype(BF16), bzero], axis=0)
        k_h = kc_ref[:, h * ML_PAD:(h + 1) * ML_PAD]
        q_h = qct_ref[h * ML_PAD:(h + 1) * ML_PAD, :]
        ct = c_sc[h]
        z = jnp.dot(aug, bmat, preferred_element_type=F32)
        s_raw = jnp.dot(k_h, q_h, preferred_element_type=F32)
        inter = jnp.dot(ct.astype(BF16), q_h, preferred_element_type=F32)
        heads.append((r, b, g, g_last, m_prev, k_h, ct, z, s_raw, inter))

    outs = []
    for h in range(ML_HEADS):
        r, b, g, g_last, m_prev, k_h, ct, z, s_raw, inter = heads[h]
        d = jnp.exp(jnp.where(visible, z, -jnp.inf))
        p = (s_raw * d).astype(BF16)
        vt = mvt_ref[h * ML_VROWS:(h + 1) * ML_VROWS, :]
        num = jnp.dot(vt, p, preferred_element_type=F32) + jnp.exp(m_prev - g) * inter
        den = num[dv:dv + 1, :]
        hv = num[:dv, :] / jnp.maximum(jnp.abs(den), jnp.exp(-(b + g)))

        wv = (vt.astype(F32) * jnp.exp(r - g_last)).astype(BF16)
        c_sc[h] = jnp.exp(m_prev - g_last) * ct + jnp.dot(wv, k_h, preferred_element_type=F32)
        m_sc[h] = jnp.broadcast_to(b[:, L - 1:L] + g_last, (SUBLANES, LANES))

        hv = hv * ogt_ref[h * dv:(h + 1) * dv, :].astype(F32)
        mu = jnp.mean(hv, axis=0, keepdims=True)
        dd = hv - mu
        var = jnp.mean(dd * dd, axis=0, keepdims=True)
        hn = dd * lax.rsqrt(var + LN_EPS) * ng_ref[h * dv:(h + 1) * dv, :]
        outs.append(hn * zgt_ref[h * dv:(h + 1) * dv, :].astype(F32))
    y_ref[...] = jnp.concatenate(outs, axis=0).T.astype(BF16)


def _mlstm(kc, qct, mvt, ogt, zgt, maug, mrow, norm_g_b):
    b, s, _ = kc.shape
    L = ML_CHUNK

    def col(n):
        return pl.BlockSpec((None, n, L), lambda i, j: (i, 0, j))

    def row(n):
        return pl.BlockSpec((None, L, n), lambda i, j: (i, j, 0))

    return pl.pallas_call(
        _mlstm_kernel,
        grid=(b, s // L),
        in_specs=[row(ML_PW), col(ML_PW), col(ML_HEADS * ML_VROWS), col(ML_WIDTH), col(ML_WIDTH),
                  row(LANES), col(3 * SUBLANES),
                  pl.BlockSpec((ML_WIDTH, L), lambda i, j: (0, 0))],
        out_specs=row(ML_WIDTH),
        out_shape=jax.ShapeDtypeStruct((b, s, ML_WIDTH), BF16),
        scratch_shapes=[pltpu.VMEM((ML_HEADS, ML_VROWS, LANES), F32),
                        pltpu.VMEM((ML_HEADS, SUBLANES, LANES), F32)],
        compiler_params=pltpu.CompilerParams(
            dimension_semantics=("arbitrary", "arbitrary"), vmem_limit_bytes=48 * 1024 * 1024),
        name="mlstm",
    )(kc, qct, mvt, ogt, zgt, maug, mrow, norm_g_b)


def _memkv_kernel(m_ref, w_ref, o_ref):
    o_ref[...] = jnp.dot(m_ref[...].astype(BF16), w_ref[...].astype(BF16),
                         preferred_element_type=F32).astype(BF16)


def _memkv(mem, w_mem_kv):
    b, ml, _ = mem.shape
    return pl.pallas_call(
        _memkv_kernel,
        grid=(DEPTH, b),
        in_specs=[pl.BlockSpec((None, ml, D_MODEL), lambda l, i: (i, 0, 0)),
                  pl.BlockSpec((None, D_MODEL, 2 * MEM_WIDTH), lambda l, i: (l, 0, 0))],
        out_specs=pl.BlockSpec((None, None, ml, 2 * MEM_WIDTH), lambda l, i: (l, i, 0, 0)),
        out_shape=jax.ShapeDtypeStruct((DEPTH, b, ml, 2 * MEM_WIDTH), BF16),
        compiler_params=pltpu.CompilerParams(dimension_semantics=("arbitrary", "arbitrary")),
        name="mem_kv",
    )(mem, w_mem_kv)


def _out_kernel(x_ref, yf_ref, ym_ref, rq_ref, rzg_ref, mk_ref, mv_ref, wo_ref, g_ref, b_ref, o_ref):
    rq = rq_ref[...]
    mk = mk_ref[...]
    mv = mv_ref[...]
    lane = lax.broadcasted_iota(jnp.int32, (1, MEM_WIDTH), 1)
    scale = jnp.asarray(MEM_HEAD_DIM ** -0.5, BF16)
    att = jnp.zeros((OUT_TM, MEM_WIDTH), F32)
    for h in range(MEM_HEADS):
        msk = (lane >= MEM_HEAD_DIM * h) & (lane < MEM_HEAD_DIM * (h + 1))
        qh = jnp.where(msk, rq, jnp.zeros_like(rq)) * scale
        s = lax.dot_general(qh, mk, (((1,), (1,)), ((), ())), preferred_element_type=F32)
        p = jnp.exp(s - jnp.max(s, axis=1, keepdims=True))
        l = jnp.sum(p, axis=1, keepdims=True)
        vh = jnp.where(msk, mv, jnp.zeros_like(mv))
        att = att + jnp.dot(p.astype(BF16), vh, preferred_element_type=F32) / l
    y_mem = (att * rzg_ref[...].astype(F32)).astype(BF16)
    o_fox, o_ml = FOX_WIDTH, FOX_WIDTH + ML_WIDTH
    y = (jnp.dot(yf_ref[...], wo_ref[0:o_fox, :], preferred_element_type=F32)
         + jnp.dot(ym_ref[...], wo_ref[o_fox:o_ml, :], preferred_element_type=F32)
         + jnp.dot(y_mem, wo_ref[o_ml:, :], preferred_element_type=F32))
    r = DEEPNORM_ALPHA * x_ref[...] + y
    mu = jnp.mean(r, axis=1, keepdims=True)
    d = r - mu
    var = jnp.mean(d * d, axis=1, keepdims=True)
    o_ref[...] = d * lax.rsqrt(var + LN_EPS) * g_ref[...] + b_ref[...]


def _out_proj(x, y_fox, y_ml, umem, memkv, layer, w_out, ln_g, ln_b):
    b, s, _ = x.shape
    tm = OUT_TM
    ml = memkv.shape[2]
    return pl.pallas_call(
        _out_kernel,
        grid=(b, s // tm),
        in_specs=[
            pl.BlockSpec((None, tm, D_MODEL), lambda i, j: (i, j, 0)),
            pl.BlockSpec((None, tm, FOX_WIDTH), lambda i, j: (i, j, 0)),
            pl.BlockSpec((None, tm, ML_WIDTH), lambda i, j: (i, j, 0)),
            pl.BlockSpec((None, tm, MEM_WIDTH), lambda i, j: (i, j, 0)),
            pl.BlockSpec((None, tm, MEM_WIDTH), lambda i, j: (i, j, 1)),
            pl.BlockSpec((None, None, ml, MEM_WIDTH), lambda i, j: (layer, i, 0, 0)),
            pl.BlockSpec((None, None, ml, MEM_WIDTH), lambda i, j: (layer, i, 0, 1)),
            pl.BlockSpec((None, D_MODEL, D_MODEL), lambda i, j: (layer, 0, 0)),
            pl.BlockSpec((None, 1, D_MODEL), lambda i, j: (layer, 0, 0)),
            pl.BlockSpec((None, 1, D_MODEL), lambda i, j: (layer, 0, 0)),
        ],
        out_specs=pl.BlockSpec((None, tm, D_MODEL), lambda i, j: (i, j, 0)),
        out_shape=jax.ShapeDtypeStruct((b, s, D_MODEL), F32),
        compiler_params=pltpu.CompilerParams(
            dimension_semantics=("arbitrary", "arbitrary"), vmem_limit_bytes=48 * 1024 * 1024),
        name="out_proj_ln",
    )(x, y_fox, y_ml, umem, umem, memkv, memkv, w_out, ln_g, ln_b)


def kernel(x, mem, w_in, fox_f_bias, mlstm_conv_w, mlstm_conv_b, mlstm_i_bias, mlstm_f_bias,
           mlstm_norm_g, w_mem_kv, w_out, ln_g, ln_b):
    w_re = _relayout_w_in(w_in)
    w_out_b = w_out.astype(BF16)
    bias_rows = _spread_gates(fox_f_bias, mlstm_i_bias, mlstm_f_bias)[:, None, :]
    conv_w = jnp.concatenate([_pad_heads_last(mlstm_conv_w[..., :ML_WIDTH]),
                              _pad_heads_last(mlstm_conv_w[..., ML_WIDTH:])], axis=-1)
    conv_b = jnp.concatenate([_pad_heads_last(mlstm_conv_b[..., :ML_WIDTH]),
                              _pad_heads_last(mlstm_conv_b[..., ML_WIDTH:])], axis=-1)[:, None, :]
    norm_g_b = jnp.broadcast_to(mlstm_norm_g[:, :, None], (DEPTH, ML_WIDTH, ML_CHUNK))
    ln_g3 = ln_g[:, None, :]
    ln_b3 = ln_b[:, None, :]
    sels = _bias_selectors()

    memkv = _memkv(mem, w_mem_kv)
    for l in range(DEPTH):
        kc, qct, mvt, ogt, zgt, ufx, umem, g, fqt, fvt = _project(x, w_re[l], conv_w[l], conv_b[l])
        kb, qbt, maug, mrow = _gates(g, bias_rows[l], sels)
        y_fox = _fox(fqt, qbt, ufx, kb, fvt)
        y_ml = _mlstm(kc, qct, mvt, ogt, zgt, maug, mrow, norm_g_b[l])
        x = _out_proj(x, y_fox, y_ml, umem, memkv, l, w_out_b, ln_g3, ln_b3)
    return x
```

```python
import math

import numpy as np
import jax
import jax.numpy as jnp
from jax import lax
from jax.experimental import pallas as pl
from jax.experimental.pallas import tpu as pltpu

F32 = jnp.float32
BF16 = jnp.bfloat16

D_MODEL = 1024
DEPTH = 4
FOX_HEADS = 6
FOX_HEAD_DIM = 64
FOX_WIDTH = FOX_HEADS * FOX_HEAD_DIM
ML_HEADS = 4
ML_HEAD_DIM = 96
ML_WIDTH = ML_HEADS * ML_HEAD_DIM
MEM_HEADS = 4
MEM_HEAD_DIM = 64
MEM_WIDTH = MEM_HEADS * MEM_HEAD_DIM
CONV_WIDTH = 4
LN_EPS = 1e-5
DEEPNORM_ALPHA = (2.0 * DEPTH) ** 0.25
LOG2E = math.log2(math.e)

LANES = 128
SUBLANES = 8
BF16_ROWS = 16
ML_PAD = LANES
ML_PW = ML_HEADS * ML_PAD
ML_VROWS = ML_HEAD_DIM + BF16_ROWS
FOX_VROWS = FOX_HEAD_DIM + BF16_ROWS
MEM_VROWS = MEM_HEAD_DIM + BF16_ROWS

COL_QK = 0
COL_MV = COL_QK + 2 * ML_PW
COL_MZ = COL_MV + 2 * ML_WIDTH
COL_FK = COL_MZ + ML_WIDTH + FOX_WIDTH
COL_FV = COL_FK + 2 * FOX_WIDTH
COL_RQ = COL_FV + FOX_WIDTH + LANES
W_COLS = COL_RQ + 2 * MEM_WIDTH
G_COLS = LANES
GATE_I = 48
GATE_F = 56
N_PIECES = 3
ML_BIAS_ROWS = 2 * BF16_ROWS

ML_CHUNK = 256
FOX_TQ = 512
FOX_TK = 256
PROJ_TM = 512
OUT_TM = 512
OUT_SUB = 128

_SRC = {}
_off = 0
for _name, _w in (("fq", FOX_WIDTH), ("fk", FOX_WIDTH), ("fv", FOX_WIDTH), ("ff", FOX_HEADS),
                  ("fz", FOX_WIDTH), ("mq", ML_WIDTH), ("mk", ML_WIDTH), ("mv", ML_WIDTH),
                  ("mi", ML_HEADS), ("mf", ML_HEADS), ("mo", ML_WIDTH), ("mz", ML_WIDTH),
                  ("rq", MEM_WIDTH), ("rz", MEM_WIDTH)):
    _SRC[_name] = (_off, _w)
    _off += _w


def _sigmoid(x):
    return 1.0 / (1.0 + jnp.exp(-x))


def _silu(x):
    return x * _sigmoid(x)


def _split3(x):
    hi = x.astype(BF16)
    r1 = x - hi.astype(F32)
    mid = r1.astype(BF16)
    lo = (r1 - mid.astype(F32)).astype(BF16)
    return hi, mid, lo


def _with_ones_rows(t, head_dim, heads):
    ones = jnp.ones((BF16_ROWS, t.shape[1]), t.dtype)
    rows = []
    for h in range(heads):
        rows += [t[h * head_dim:(h + 1) * head_dim, :], ones]
    return jnp.concatenate(rows, axis=0)


def _zeros_like_cols(t, n):
    return jnp.zeros(t.shape[:-1] + (n,), t.dtype)


def _pad_heads_last(t):
    parts = []
    for h in range(ML_HEADS):
        parts.append(t[..., h * ML_HEAD_DIM:(h + 1) * ML_HEAD_DIM])
        parts.append(_zeros_like_cols(t, ML_PAD - ML_HEAD_DIM))
    return jnp.concatenate(parts, axis=-1)


def _spread_gates(ff, mi, mf):
    parts = []
    for h in range(FOX_HEADS):
        parts += [ff[..., h:h + 1], _zeros_like_cols(ff, SUBLANES - 1)]
    parts += [mi, _zeros_like_cols(mi, GATE_F - GATE_I - ML_HEADS),
              mf, _zeros_like_cols(mf, G_COLS - GATE_F - ML_HEADS)]
    return jnp.concatenate(parts, axis=-1)


def _relayout_w_in(w):
    def take(name):
        o, n = _SRC[name]
        return w[..., o:o + n]

    cols = [_pad_heads_last(take("mq")), _pad_heads_last(take("mk")),
            take("mv"), take("mo"), take("mz"), take("fz"),
            take("fk"), take("fq") * (FOX_HEAD_DIM ** -0.5 * LOG2E),
            take("fv"), _spread_gates(take("ff"), take("mi"), take("mf")),
            take("rq"), take("rz")]
    return jnp.concatenate(cols, axis=-1).astype(BF16)


def _bias_selectors():
    n = N_PIECES * LANES
    selk = np.zeros((n, LANES), np.float32)
    selq = np.zeros((n, LANES), np.float32)
    selm = np.zeros((n, LANES), np.float32)
    onek = np.zeros((1, LANES), np.float32)
    oneq = np.zeros((1, LANES), np.float32)
    onem = np.zeros((1, LANES), np.float32)
    for h in range(FOX_HEADS):
        base = SUBLANES * h
        for j in range(N_PIECES):
            selk[j * LANES + base, base + j] = -1.0
            selq[j * LANES + base, base + N_PIECES + j] = 1.0
            onek[0, base + N_PIECES + j] = 1.0
            oneq[0, base + j] = 1.0
    for h in range(ML_HEADS):
        base = SUBLANES * h
        for j in range(N_PIECES):
            selm[j * LANES + GATE_I + h, base + j] = 1.0
            onem[0, base + N_PIECES + j] = 1.0
    return (jnp.asarray(selk, BF16), jnp.asarray(selq, BF16), jnp.asarray(selm, BF16),
            jnp.asarray(onek), jnp.asarray(oneq), jnp.asarray(onem))


def _proj_kernel(x_ref, w_ref, cw_ref, cb_ref,
                 kc_ref, qct_ref, mvt_ref, ogt_ref, zgt_ref, ufx_ref, rqt_ref, rzg_ref, g_ref,
                 fqt_ref, fvt_ref, tail_sc):
    tm = PROJ_TM

    @pl.when(pl.program_id(1) == 0)
    def _():
        tail_sc[...] = jnp.zeros_like(tail_sc)

    xb = x_ref[...].astype(BF16)

    def mm(c0, n):
        return jnp.dot(xb, w_ref[:, c0:c0 + n], preferred_element_type=F32)

    u = mm(COL_QK, 2 * ML_PW)
    tail = tail_sc[...]
    w = cw_ref[...]
    row8 = lax.broadcasted_iota(jnp.int32, (SUBLANES, 2 * ML_PW), 0)
    y = u * w[CONV_WIDTH - 1:CONV_WIDTH, :] + cb_ref[...]
    for shift in range(1, CONV_WIDTH):
        r = pltpu.roll(u, shift, axis=0)
        rt = pltpu.roll(tail, shift, axis=0)
        first = jnp.where(row8 < shift, rt, r[:SUBLANES])
        xs = jnp.concatenate([first, r[SUBLANES:]], axis=0)
        y = y + xs * w[CONV_WIDTH - 1 - shift:CONV_WIDTH - shift, :]
    tail_sc[...] = u[tm - SUBLANES:, :]
    a = _silu(y)
    qct_ref[...] = a[:, :ML_PW].T.astype(BF16)
    kc_ref[...] = (a[:, ML_PW:] * (ML_HEAD_DIM ** -0.5)).astype(BF16)

    t = mm(COL_MV, 2 * ML_WIDTH)
    mvt_ref[...] = _with_ones_rows(t[:, :ML_WIDTH].T.astype(BF16), ML_HEAD_DIM, ML_HEADS)
    ogt_ref[...] = _sigmoid(t[:, ML_WIDTH:]).T.astype(BF16)

    t = _silu(mm(COL_MZ, ML_WIDTH + FOX_WIDTH))
    zgt_ref[...] = t[:, :ML_WIDTH].T.astype(BF16)
    ufx_ref[:, FOX_WIDTH:] = t[:, ML_WIDTH:].astype(BF16)

    t = mm(COL_FK, 2 * FOX_WIDTH)
    ufx_ref[:, :FOX_WIDTH] = t[:, :FOX_WIDTH].astype(BF16)
    fqt_ref[...] = t[:, FOX_WIDTH:].T.astype(BF16)

    t = mm(COL_FV, FOX_WIDTH + G_COLS)
    fvt_ref[...] = _with_ones_rows(t[:, :FOX_WIDTH].T.astype(BF16), FOX_HEAD_DIM, FOX_HEADS)
    g_ref[...] = t[:, FOX_WIDTH:]

    t = mm(COL_RQ, 2 * MEM_WIDTH)
    rqt_ref[...] = t[:, :MEM_WIDTH].T.astype(BF16)
    rzg_ref[...] = _silu(t[:, MEM_WIDTH:]).astype(BF16)


def _project(x, w_re, conv_w, conv_b, layer):
    b, s, _ = x.shape
    tm = PROJ_TM

    def row(n):
        return pl.BlockSpec((None, tm, n), lambda i, j: (i, j, 0))

    def col(n):
        return pl.BlockSpec((None, n, tm), lambda i, j: (i, 0, j))

    def per_layer(shape):
        return pl.BlockSpec((None,) + shape, lambda i, j: (layer, 0, 0))

    def rows_shape(n):
        return jax.ShapeDtypeStruct((b, s, n), BF16)

    def cols_shape(n):
        return jax.ShapeDtypeStruct((b, n, s), BF16)

    return pl.pallas_call(
        _proj_kernel,
        grid=(b, s // tm),
        in_specs=[row(D_MODEL), per_layer((D_MODEL, W_COLS)),
                  per_layer((CONV_WIDTH, 2 * ML_PW)), per_layer((1, 2 * ML_PW))],
        out_specs=[row(ML_PW), col(ML_PW), col(ML_HEADS * ML_VROWS), col(ML_WIDTH), col(ML_WIDTH),
                   row(2 * FOX_WIDTH), col(MEM_WIDTH), row(MEM_WIDTH), row(G_COLS),
                   col(FOX_WIDTH), col(FOX_HEADS * FOX_VROWS)],
        out_shape=[rows_shape(ML_PW), cols_shape(ML_PW), cols_shape(ML_HEADS * ML_VROWS),
                   cols_shape(ML_WIDTH), cols_shape(ML_WIDTH),
                   rows_shape(2 * FOX_WIDTH), cols_shape(MEM_WIDTH), rows_shape(MEM_WIDTH),
                   jax.ShapeDtypeStruct((b, s, G_COLS), F32),
                   cols_shape(FOX_WIDTH), cols_shape(FOX_HEADS * FOX_VROWS)],
        scratch_shapes=[pltpu.VMEM((SUBLANES, 2 * ML_PW), F32)],
        compiler_params=pltpu.CompilerParams(
            dimension_semantics=("arbitrary", "arbitrary"), vmem_limit_bytes=56 * 1024 * 1024),
        name="in_proj",
    )(x, w_re, conv_w, conv_b)


def _gates_kernel(g_ref, bias_ref, selk_ref, selq_ref, selm_ref, onek_ref, oneq_ref, onem_ref,
                  kb_ref, qbt_ref, maug_ref, mrow_ref, carry_ref):
    @pl.when(pl.program_id(0) == 0)
    def _():
        carry_ref[...] = jnp.zeros_like(carry_ref)

    L = ML_CHUNK
    r = lax.broadcasted_iota(jnp.int32, (L, L), 0)
    c = lax.broadcasted_iota(jnp.int32, (L, L), 1)
    tri = (c <= r).astype(BF16)
    lane = lax.broadcasted_iota(jnp.int32, (L, G_COLS), 1)
    pos = lax.broadcasted_iota(jnp.int32, (SUBLANES, L), 1)

    for bi in range(g_ref.shape[0]):
        x = g_ref[bi] + bias_ref[...]
        log_sig = jnp.minimum(x, 0.0) - jnp.log1p(jnp.exp(-jnp.abs(x)))
        parts = jnp.dot(tri, jnp.concatenate(_split3(log_sig), axis=1),
                        preferred_element_type=F32)
        local = (parts[:, :LANES] + parts[:, LANES:2 * LANES]) + parts[:, 2 * LANES:]
        carry = carry_ref[bi, 0:1, :]
        c_run = local + carry
        carry_ref[bi, 0:1, :] = carry + local[L - 1:L, :]

        pieces = jnp.concatenate(_split3(c_run * LOG2E), axis=1)
        kb_ref[bi] = (jnp.dot(pieces, selk_ref[...], preferred_element_type=F32)
                      + onek_ref[...]).astype(BF16)
        qb = jnp.dot(pieces, selq_ref[...], preferred_element_type=F32) + oneq_ref[...]
        qbt_ref[bi] = qb.T.astype(BF16)

        rc = x - pltpu.roll(local, LANES - (GATE_F - GATE_I), axis=1)
        pieces = jnp.concatenate(_split3(rc), axis=1)
        maug_ref[bi] = (jnp.dot(pieces, selm_ref[...], preferred_element_type=F32)
                        + onem_ref[...]).astype(BF16)
        rows = jnp.where(lane < GATE_F, rc, local).T[GATE_I:GATE_I + 2 * SUBLANES, :]
        cm = rows[:SUBLANES]
        shift = 1
        while shift < L:
            cm = jnp.maximum(cm, jnp.where(pos >= shift, pltpu.roll(cm, shift, axis=1), -jnp.inf))
            shift *= 2
        mrow_ref[bi] = jnp.concatenate([rows, cm], axis=0)


def _gates(g, bias_row, sels):
    b, s, _ = g.shape
    L = ML_CHUNK
    const = lambda shape: pl.BlockSpec(shape, lambda j: (0, 0))
    sel_shape = (N_PIECES * LANES, LANES)
    return pl.pallas_call(
        _gates_kernel,
        grid=(s // L,),
        in_specs=[pl.BlockSpec((b, L, G_COLS), lambda j: (0, j, 0)),
                  const((1, G_COLS)),
                  const(sel_shape), const(sel_shape), const(sel_shape),
                  const((1, LANES)), const((1, LANES)), const((1, LANES))],
        out_specs=[pl.BlockSpec((b, L, LANES), lambda j: (0, j, 0)),
                   pl.BlockSpec((b, LANES, L), lambda j: (0, 0, j)),
                   pl.BlockSpec((b, L, LANES), lambda j: (0, j, 0)),
                   pl.BlockSpec((b, 3 * SUBLANES, L), lambda j: (0, 0, j))],
        out_shape=[jax.ShapeDtypeStruct((b, s, LANES), BF16),
                   jax.ShapeDtypeStruct((b, LANES, s), BF16),
                   jax.ShapeDtypeStruct((b, s, LANES), BF16),
                   jax.ShapeDtypeStruct((b, 3 * SUBLANES, s), F32)],
        scratch_shapes=[pltpu.VMEM((b, SUBLANES, G_COLS), F32)],
        compiler_params=pltpu.CompilerParams(dimension_semantics=("arbitrary",)),
        name="gate_prefix",
    )(g, bias_row, *sels)


def _fox_kernel(qt_ref, qbt_ref, k_ref, kb_ref, vt_ref, zg_ref, o_ref,
                qa_sc, sa_sc, sb_sc, mxa_sc, mxb_sc, m_sc, acc_sc):
    qi = pl.program_id(1)
    tq, tk = FOX_TQ, FOX_TK
    hd = FOX_HEAD_DIM

    brow = lax.broadcasted_iota(jnp.int32, (LANES, tq), 0)
    qbt = qbt_ref[...]
    zslot = jnp.zeros((hd, tq), BF16)
    for h in range(FOX_HEADS):
        qh = qt_ref[h * hd:(h + 1) * hd, :]
        top = [qh, zslot] if h % 2 == 0 else [zslot, qh]
        own = (brow >= SUBLANES * h) & (brow < SUBLANES * (h + 1))
        qa_sc[h] = jnp.concatenate(top + [jnp.where(own, qbt, jnp.zeros_like(qbt))], axis=0)

    m_sc[...] = jnp.full_like(m_sc, -jnp.inf)
    acc_sc[...] = jnp.zeros_like(acc_sc)

    def logits(j, h, dst, mxd, key_offset):
        k0 = pl.multiple_of(j * tk, tk)
        p = h // 2
        kk = jnp.concatenate([k_ref[pl.ds(k0, tk), p * LANES:(p + 1) * LANES],
                              kb_ref[pl.ds(k0, tk), :]], axis=1)
        st = jnp.dot(kk, qa_sc[h], preferred_element_type=F32)
        if key_offset is not None:
            kr = lax.broadcasted_iota(jnp.int32, (tk, tq), 0)
            qc = lax.broadcasted_iota(jnp.int32, (tk, tq), 1)
            st = jnp.where(kr + key_offset <= qc, st, -jnp.inf)
        dst[h] = st
        mxd[h] = jnp.max(st, axis=0, keepdims=True)

    def softmax_pv(j, h, src, mxs):
        k0 = pl.multiple_of(j * tk, tk)
        m_old = m_sc[h]
        m_new = jnp.maximum(m_old, mxs[h])
        alpha = jnp.exp2(m_old - m_new)
        pt = jnp.exp2(src[h] - m_new).astype(BF16)
        vt = vt_ref[h * FOX_VROWS:(h + 1) * FOX_VROWS, pl.ds(k0, tk)]
        acc_sc[h] = alpha * acc_sc[h] + jnp.dot(vt, pt, preferred_element_type=F32)
        m_sc[h] = m_new

    buf_a = (sa_sc, mxa_sc)
    buf_b = (sb_sc, mxb_sc)

    def stage(j_qk, dst, key_offset, j_sm, src):
        if j_qk is not None:
            logits(j_qk, 0, *dst, key_offset)
        for h in range(FOX_HEADS):
            if j_qk is not None and h + 1 < FOX_HEADS:
                logits(j_qk, h + 1, *dst, key_offset)
            if j_sm is not None:
                softmax_pv(j_sm, h, *src)

    stage(2 * qi, buf_a, 0, None, None)
    stage(2 * qi + 1, buf_b, tk, 2 * qi, buf_a)
    stage(0, buf_a, None, 2 * qi + 1, buf_b)

    def body(i, carry):
        stage(2 * i + 1, buf_b, None, 2 * i, buf_a)
        stage(2 * i + 2, buf_a, None, 2 * i + 1, buf_b)
        return carry

    lax.fori_loop(0, qi, body, 0)

    out_t = jnp.concatenate(
        [acc_sc[h, :hd, :] / acc_sc[h, hd:hd + 1, :] for h in range(FOX_HEADS)], axis=0)
    o_ref[...] = (out_t.T * zg_ref[...].astype(F32)).astype(BF16)


def _fox(qt, qbt, ufx, kb, vt):
    b, s, _ = ufx.shape
    tq = FOX_TQ
    return pl.pallas_call(
        _fox_kernel,
        grid=(b, s // tq),
        in_specs=[
            pl.BlockSpec((None, FOX_WIDTH, tq), lambda i, j: (i, 0, j)),
            pl.BlockSpec((None, LANES, tq), lambda i, j: (i, 0, j)),
            pl.BlockSpec((None, s, FOX_WIDTH), lambda i, j: (i, 0, 0)),
            pl.BlockSpec((None, s, LANES), lambda i, j: (i, 0, 0)),
            pl.BlockSpec((None, FOX_HEADS * FOX_VROWS, s), lambda i, j: (i, 0, 0)),
            pl.BlockSpec((None, tq, FOX_WIDTH), lambda i, j: (i, j, 1)),
        ],
        out_specs=pl.BlockSpec((None, tq, FOX_WIDTH), lambda i, j: (i, j, 0)),
        out_shape=jax.ShapeDtypeStruct((b, s, FOX_WIDTH), BF16),
        scratch_shapes=[pltpu.VMEM((FOX_HEADS, 2 * LANES, tq), BF16),
                        pltpu.VMEM((FOX_HEADS, FOX_TK, tq), F32),
                        pltpu.VMEM((FOX_HEADS, FOX_TK, tq), F32),
                        pltpu.VMEM((FOX_HEADS, 1, tq), F32),
                        pltpu.VMEM((FOX_HEADS, 1, tq), F32),
                        pltpu.VMEM((FOX_HEADS, 1, tq), F32),
                        pltpu.VMEM((FOX_HEADS, FOX_VROWS, tq), F32)],
        compiler_params=pltpu.CompilerParams(
            dimension_semantics=("arbitrary", "arbitrary"), vmem_limit_bytes=48 * 1024 * 1024),
        name="fox_attn",
    )(qt, qbt, ufx, kb, vt, ufx)


def _mlstm_kernel(kc_ref, qct_ref, mvt_ref, ogt_ref, zgt_ref, maug_ref, mrow_ref, ng_ref,
                  y_ref, c_sc, m_sc):
    L = ML_CHUNK
    dv = ML_HEAD_DIM

    @pl.when(pl.program_id(1) == 0)
    def _():
        c_sc[...] = jnp.zeros_like(c_sc)
        m_sc[...] = jnp.zeros_like(m_sc)

    kr = lax.broadcasted_iota(jnp.int32, (L, L), 0)
    qc = lax.broadcasted_iota(jnp.int32, (L, L), 1)
    visible = kr <= qc
    aug = maug_ref[...]
    brow = lax.broadcasted_iota(jnp.int32, (ML_BIAS_ROWS, L), 0)
    bzero = jnp.zeros((LANES - ML_BIAS_ROWS, L), BF16)

    heads = []
    for h in range(ML_HEADS):
        r = mrow_ref[h:h + 1, :]
        b = mrow_ref[SUBLANES + h:SUBLANES + h + 1, :]
        cm = mrow_ref[2 * SUBLANES + h:2 * SUBLANES + h + 1, :]
        m_prev = m_sc[h, 0:1, 0:1]
        g = jnp.maximum(m_prev, cm)
        g_last = g[:, L - 1:L]
        g_hi, g_mid, g_lo = (t.astype(F32) for t in _split3(g))
        rel = brow - SUBLANES * h
        blk = jnp.where(rel == N_PIECES, -g_hi,
                        jnp.where(rel == N_PIECES + 1, -g_mid,
                                  jnp.where(rel == N_PIECES + 2, -g_lo,
                                            jnp.where((rel >= 0) & (rel < N_PIECES), 1.0, 0.0))))
        bmat = jnp.concatenate([blk.astype(BF16), bzero], axis=0)
        k_h = kc_ref[:, h * ML_PAD:(h + 1) * ML_PAD]
        q_h = qct_ref[h * ML_PAD:(h + 1) * ML_PAD, :]
        ct = c_sc[h]
        z = jnp.dot(aug, bmat, preferred_element_type=F32)
        s_raw = jnp.dot(k_h, q_h, preferred_element_type=F32)
        inter = jnp.dot(ct.astype(BF16), q_h, preferred_element_type=F32)
        heads.append((r, b, g, g_last, m_prev, k_h, ct, z, s_raw, inter))

    outs = []
    for h in range(ML_HEADS):
        r, b, g, g_last, m_prev, k_h, ct, z, s_raw, inter = heads[h]
        d = jnp.exp(jnp.where(visible, z, -jnp.inf))
        p = (s_raw * d).astype(BF16)
        vt = mvt_ref[h * ML_VROWS:(h + 1) * ML_VROWS, :]
        num = jnp.dot(vt, p, preferred_element_type=F32) + jnp.exp(m_prev - g) * inter
        den = num[dv:dv + 1, :]
        hv = num[:dv, :] / jnp.maximum(jnp.abs(den), jnp.exp(-(b + g)))

        wv = (vt.astype(F32) * jnp.exp(r - g_last)).astype(BF16)
        c_sc[h] = jnp.exp(m_prev - g_last) * ct + jnp.dot(wv, k_h, preferred_element_type=F32)
        m_sc[h] = jnp.broadcast_to(b[:, L - 1:L] + g_last, (SUBLANES, LANES))

        hv = hv * ogt_ref[h * dv:(h + 1) * dv, :].astype(F32)
        mu = jnp.mean(hv, axis=0, keepdims=True)
        dd = hv - mu
        var = jnp.mean(dd * dd, axis=0, keepdims=True)
        hn = dd * lax.rsqrt(var + LN_EPS) * ng_ref[h * dv:(h + 1) * dv, :]
        outs.append(hn * zgt_ref[h * dv:(h + 1) * dv, :].astype(F32))
    y_ref[...] = jnp.concatenate(outs, axis=0).T.astype(BF16)


def _mlstm(kc, qct, mvt, ogt, zgt, maug, mrow, norm_g_b):
    b, s, _ = kc.shape
    L = ML_CHUNK

    def col(n):
        return pl.BlockSpec((None, n, L), lambda i, j: (i, 0, j))

    def row(n):
        return pl.BlockSpec((None, L, n), lambda i, j: (i, j, 0))

    return pl.pallas_call(
        _mlstm_kernel,
        grid=(b, s // L),
        in_specs=[row(ML_PW), col(ML_PW), col(ML_HEADS * ML_VROWS), col(ML_WIDTH), col(ML_WIDTH),
                  row(LANES), col(3 * SUBLANES),
                  pl.BlockSpec((ML_WIDTH, L), lambda i, j: (0, 0))],
        out_specs=row(ML_WIDTH),
        out_shape=jax.ShapeDtypeStruct((b, s, ML_WIDTH), BF16),
        scratch_shapes=[pltpu.VMEM((ML_HEADS, ML_VROWS, LANES), F32),
                        pltpu.VMEM((ML_HEADS, SUBLANES, LANES), F32)],
        compiler_params=pltpu.CompilerParams(
            dimension_semantics=("arbitrary", "arbitrary"), vmem_limit_bytes=48 * 1024 * 1024),
        name="mlstm",
    )(kc, qct, mvt, ogt, zgt, maug, mrow, norm_g_b)


def _memkv_kernel(m_ref, w_ref, k4_ref, vt_ref):
    kv = jnp.dot(m_ref[...].astype(BF16), w_ref[...].astype(BF16), preferred_element_type=F32)
    k = kv[:, :MEM_WIDTH] * (MEM_HEAD_DIM ** -0.5)
    lane = lax.broadcasted_iota(jnp.int32, k.shape, 1)
    for h in range(MEM_HEADS):
        own = (lane >= MEM_HEAD_DIM * h) & (lane < MEM_HEAD_DIM * (h + 1))
        k4_ref[h] = jnp.where(own, k, 0.0).astype(BF16)
    vt_ref[...] = _with_ones_rows(kv[:, MEM_WIDTH:].T.astype(BF16), MEM_HEAD_DIM, MEM_HEADS)


def _memkv(mem, w_mem_kv):
    b, ml, _ = mem.shape
    return pl.pallas_call(
        _memkv_kernel,
        grid=(DEPTH, b),
        in_specs=[pl.BlockSpec((None, ml, D_MODEL), lambda l, i: (i, 0, 0)),
                  pl.BlockSpec((None, D_MODEL, 2 * MEM_WIDTH), lambda l, i: (l, 0, 0))],
        out_specs=[pl.BlockSpec((None, None, MEM_HEADS, ml, MEM_WIDTH), lambda l, i: (l, i, 0, 0, 0)),
                   pl.BlockSpec((None, None, MEM_HEADS * MEM_VROWS, ml), lambda l, i: (l, i, 0, 0))],
        out_shape=[jax.ShapeDtypeStruct((DEPTH, b, MEM_HEADS, ml, MEM_WIDTH), BF16),
                   jax.ShapeDtypeStruct((DEPTH, b, MEM_HEADS * MEM_VROWS, ml), BF16)],
        compiler_params=pltpu.CompilerParams(dimension_semantics=("arbitrary", "arbitrary")),
        name="mem_kv",
    )(mem, w_mem_kv)


def _out_kernel(x_ref, yf_ref, ym_ref, rqt_ref, rzg_ref, k4_ref, vt_ref, wo_ref, g_ref, b_ref,
                o_ref, ob_ref):
    hd = MEM_HEAD_DIM
    o_mem = FOX_WIDTH + ML_WIDTH
    n_sub = OUT_TM // OUT_SUB
    ahead = 2

    def rows(q):
        return slice(q * OUT_SUB, (q + 1) * OUT_SUB)

    def main_dot(q):
        y_main = jnp.concatenate([yf_ref[rows(q), :], ym_ref[rows(q), :]], axis=1)
        return jnp.dot(y_main, wo_ref[0:o_mem, :], preferred_element_type=F32)

    rqt = rqt_ref[...]
    sts = [jnp.dot(k4_ref[h], rqt, preferred_element_type=F32) for h in range(MEM_HEADS)]
    mains = [main_dot(q) for q in range(ahead)]

    outs = []
    for h in range(MEM_HEADS):
        st = sts[h]
        p = jnp.exp(st - jnp.max(st, axis=0, keepdims=True)).astype(BF16)
        acc = jnp.dot(vt_ref[h * MEM_VROWS:(h + 1) * MEM_VROWS, :], p,
                      preferred_element_type=F32)
        outs.append(acc[:hd, :] / acc[hd:hd + 1, :])
    att = jnp.concatenate(outs, axis=0).T
    y_mem = (att * rzg_ref[...].astype(F32)).astype(BF16)

    for q in range(n_sub):
        if q + ahead < n_sub:
            mains.append(main_dot(q + ahead))
        y = mains[q] + jnp.dot(y_mem[rows(q), :], wo_ref[o_mem:, :], preferred_element_type=F32)
        r = DEEPNORM_ALPHA * x_ref[rows(q), :] + y
        mu = jnp.mean(r, axis=1, keepdims=True)
        d = r - mu
        var = jnp.mean(d * d, axis=1, keepdims=True)
        out = d * lax.rsqrt(var + LN_EPS) * g_ref[...] + b_ref[...]
        o_ref[rows(q), :] = out
        ob_ref[rows(q), :] = out.astype(BF16)


def _out_proj(x, y_fox, y_ml, rqt, rzg, mem_k4, mem_vt, layer, w_out, ln_g, ln_b):
    b, s, _ = x.shape
    tm = OUT_TM
    ml = mem_k4.shape[3]
    return pl.pallas_call(
        _out_kernel,
        grid=(b, s // tm),
        in_specs=[
            pl.BlockSpec((None, tm, D_MODEL), lambda i, j: (i, j, 0)),
            pl.BlockSpec((None, tm, FOX_WIDTH), lambda i, j: (i, j, 0)),
            pl.BlockSpec((None, tm, ML_WIDTH), lambda i, j: (i, j, 0)),
            pl.BlockSpec((None, MEM_WIDTH, tm), lambda i, j: (i, 0, j)),
            pl.BlockSpec((None, tm, MEM_WIDTH), lambda i, j: (i, j, 0)),
            pl.BlockSpec((None, None, MEM_HEADS, ml, MEM_WIDTH), lambda i, j: (layer, i, 0, 0, 0)),
            pl.BlockSpec((None, None, MEM_HEADS * MEM_VROWS, ml), lambda i, j: (layer, i, 0, 0)),
            pl.BlockSpec((None, D_MODEL, D_MODEL), lambda i, j: (layer, 0, 0)),
            pl.BlockSpec((None, 1, D_MODEL), lambda i, j: (layer, 0, 0)),
            pl.BlockSpec((None, 1, D_MODEL), lambda i, j: (layer, 0, 0)),
        ],
        out_specs=[pl.BlockSpec((None, tm, D_MODEL), lambda i, j: (i, j, 0)),
                   pl.BlockSpec((None, tm, D_MODEL), lambda i, j: (i, j, 0))],
        out_shape=[jax.ShapeDtypeStruct((b, s, D_MODEL), F32),
                   jax.ShapeDtypeStruct((b, s, D_MODEL), BF16)],
        compiler_params=pltpu.CompilerParams(
            dimension_semantics=("arbitrary", "arbitrary"), vmem_limit_bytes=48 * 1024 * 1024),
        name="out_proj_ln",
    )(x, y_fox, y_ml, rqt, rzg, mem_k4, mem_vt, w_out, ln_g, ln_b)


def kernel(x, mem, w_in, fox_f_bias, mlstm_conv_w, mlstm_conv_b, mlstm_i_bias, mlstm_f_bias,
           mlstm_norm_g, w_mem_kv, w_out, ln_g, ln_b):
    w_re = _relayout_w_in(w_in)
    w_out_b = w_out.astype(BF16)
    bias_rows = _spread_gates(fox_f_bias, mlstm_i_bias, mlstm_f_bias)[:, None, :]
    conv_w = jnp.concatenate([_pad_heads_last(mlstm_conv_w[..., :ML_WIDTH]),
                              _pad_heads_last(mlstm_conv_w[..., ML_WIDTH:])], axis=-1)
    conv_b = jnp.concatenate([_pad_heads_last(mlstm_conv_b[..., :ML_WIDTH]),
                              _pad_heads_last(mlstm_conv_b[..., ML_WIDTH:])], axis=-1)[:, None, :]
    norm_g_b = jnp.broadcast_to(mlstm_norm_g[:, :, None], (DEPTH, ML_WIDTH, ML_CHUNK))
    ln_g3 = ln_g[:, None, :]
    ln_b3 = ln_b[:, None, :]
    sels = _bias_selectors()

    mem_k4, mem_vt = _memkv(mem, w_mem_kv)
    x_mxu = x
    for l in range(DEPTH):
        kc, qct, mvt, ogt, zgt, ufx, rqt, rzg, g, fqt, fvt = _project(x_mxu, w_re, conv_w, conv_b, l)
        kb, qbt, maug, mrow = _gates(g, bias_rows[l], sels)
        y_fox = _fox(fqt, qbt, ufx, kb, fvt)
        y_ml = _mlstm(kc, qct, mvt, ogt, zgt, maug, mrow, norm_g_b[l])
        x, x_mxu = _out_proj(x, y_fox, y_ml, rqt, rzg, mem_k4, mem_vt, l, w_out_b, ln_g3, ln_b3)
    return x
```

```python
import math

import numpy as np
import jax
import jax.numpy as jnp
from jax import lax
from jax.experimental import pallas as pl
from jax.experimental.pallas import tpu as pltpu

F32 = jnp.float32
BF16 = jnp.bfloat16

D_MODEL = 1024
DEPTH = 4
FOX_HEADS = 6
FOX_HEAD_DIM = 64
FOX_WIDTH = FOX_HEADS * FOX_HEAD_DIM
ML_HEADS = 4
ML_HEAD_DIM = 96
ML_WIDTH = ML_HEADS * ML_HEAD_DIM
MEM_HEADS = 4
MEM_HEAD_DIM = 64
MEM_WIDTH = MEM_HEADS * MEM_HEAD_DIM
CONV_WIDTH = 4
LN_EPS = 1e-5
DEEPNORM_ALPHA = (2.0 * DEPTH) ** 0.25
LOG2E = math.log2(math.e)

LANES = 128
SUBLANES = 8
BF16_ROWS = 16
ML_PAD = LANES
ML_PW = ML_HEADS * ML_PAD
ML_VROWS = ML_HEAD_DIM + BF16_ROWS
FOX_VROWS = FOX_HEAD_DIM + BF16_ROWS
MEM_VROWS = MEM_HEAD_DIM + BF16_ROWS

COL_QK = 0
COL_MV = COL_QK + 2 * ML_PW
COL_MZ = COL_MV + 2 * ML_WIDTH
COL_FK = COL_MZ + ML_WIDTH + FOX_WIDTH
COL_FV = COL_FK + 2 * FOX_WIDTH
COL_RQ = COL_FV + FOX_WIDTH + LANES
W_COLS = COL_RQ + 2 * MEM_WIDTH
G_COLS = LANES
GATE_I = 48
GATE_F = 56
N_PIECES = 3
ML_BIAS_ROWS = 2 * BF16_ROWS

ML_CHUNK = 256
ML_NB = 2
FOX_TQ = 512
FOX_TK = 256
PROJ_TM = 512
OUT_TM = 512
OUT_SUB = 128

_SRC = {}
_off = 0
for _name, _w in (("fq", FOX_WIDTH), ("fk", FOX_WIDTH), ("fv", FOX_WIDTH), ("ff", FOX_HEADS),
                  ("fz", FOX_WIDTH), ("mq", ML_WIDTH), ("mk", ML_WIDTH), ("mv", ML_WIDTH),
                  ("mi", ML_HEADS), ("mf", ML_HEADS), ("mo", ML_WIDTH), ("mz", ML_WIDTH),
                  ("rq", MEM_WIDTH), ("rz", MEM_WIDTH)):
    _SRC[_name] = (_off, _w)
    _off += _w


def _sigmoid(x):
    return 1.0 / (1.0 + jnp.exp(-x))


def _silu(x):
    return x * _sigmoid(x)


def _split3(x):
    hi = x.astype(BF16)
    r1 = x - hi.astype(F32)
    mid = r1.astype(BF16)
    lo = (r1 - mid.astype(F32)).astype(BF16)
    return hi, mid, lo


def _with_ones_rows(t, head_dim, heads):
    ones = jnp.ones((BF16_ROWS, t.shape[1]), t.dtype)
    rows = []
    for h in range(heads):
        rows += [t[h * head_dim:(h + 1) * head_dim, :], ones]
    return jnp.concatenate(rows, axis=0)


def _zeros_like_cols(t, n):
    return jnp.zeros(t.shape[:-1] + (n,), t.dtype)


def _pad_heads_last(t):
    parts = []
    for h in range(ML_HEADS):
        parts.append(t[..., h * ML_HEAD_DIM:(h + 1) * ML_HEAD_DIM])
        parts.append(_zeros_like_cols(t, ML_PAD - ML_HEAD_DIM))
    return jnp.concatenate(parts, axis=-1)


def _spread_gates(ff, mi, mf):
    parts = []
    for h in range(FOX_HEADS):
        parts += [ff[..., h:h + 1], _zeros_like_cols(ff, SUBLANES - 1)]
    parts += [mi, _zeros_like_cols(mi, GATE_F - GATE_I - ML_HEADS),
              mf, _zeros_like_cols(mf, G_COLS - GATE_F - ML_HEADS)]
    return jnp.concatenate(parts, axis=-1)


def _w_in_segments():
    segs = []

    def put(name, dst, scale=1.0):
        o, n = _SRC[name]
        segs.append((o, dst, n, scale))
        return dst + n

    def put_padded_heads(name, dst):
        o, _ = _SRC[name]
        for h in range(ML_HEADS):
            segs.append((o + h * ML_HEAD_DIM, dst + h * ML_PAD, ML_HEAD_DIM, 1.0))
        return dst + ML_PW

    c = put_padded_heads("mq", COL_QK)
    c = put_padded_heads("mk", c)
    for name in ("mv", "mo", "mz", "fz", "fk"):
        c = put(name, c)
    c = put("fq", c, FOX_HEAD_DIM ** -0.5 * LOG2E)
    c = put("fv", c)
    o_ff, o_mi, o_mf = _SRC["ff"][0], _SRC["mi"][0], _SRC["mf"][0]
    for h in range(FOX_HEADS):
        segs.append((o_ff + h, c + SUBLANES * h, 1, 1.0))
    segs.append((o_mi, c + GATE_I, ML_HEADS, 1.0))
    segs.append((o_mf, c + GATE_F, ML_HEADS, 1.0))
    c = put("rq", c + G_COLS)
    c = put("rz", c)
    assert c == W_COLS
    return segs


W_PREP_ROWS = 256


def _w_prep_kernel(w_ref, o_ref):
    o_ref[...] = jnp.zeros_like(o_ref)
    for src, dst, n, scale in _w_in_segments():
        seg = w_ref[:, src:src + n]
        if scale != 1.0:
            seg = seg * scale
        o_ref[:, dst:dst + n] = seg.astype(BF16)


def _relayout_w_in(w):
    depth, rows, cols = w.shape
    tr = W_PREP_ROWS
    return pl.pallas_call(
        _w_prep_kernel,
        grid=(depth, rows // tr),
        in_specs=[pl.BlockSpec((None, tr, cols), lambda l, i: (l, i, 0))],
        out_specs=pl.BlockSpec((None, tr, W_COLS), lambda l, i: (l, i, 0)),
        out_shape=jax.ShapeDtypeStruct((depth, rows, W_COLS), BF16),
        compiler_params=pltpu.CompilerParams(dimension_semantics=("arbitrary", "arbitrary")),
        name="w_in_prep",
    )(w)


def _bias_selectors():
    n = N_PIECES * LANES
    selk = np.zeros((n, LANES), np.float32)
    selq = np.zeros((n, LANES), np.float32)
    selm = np.zeros((n, LANES), np.float32)
    onek = np.zeros((1, LANES), np.float32)
    oneq = np.zeros((1, LANES), np.float32)
    onem = np.zeros((1, LANES), np.float32)
    for h in range(FOX_HEADS):
        base = SUBLANES * h
        for j in range(N_PIECES):
            selk[j * LANES + base, base + j] = -1.0
            selq[j * LANES + base, base + N_PIECES + j] = 1.0
            onek[0, base + N_PIECES + j] = 1.0
            oneq[0, base + j] = 1.0
    for h in range(ML_HEADS):
        base = SUBLANES * h
        for j in range(N_PIECES):
            selm[j * LANES + GATE_I + h, base + j] = 1.0
            onem[0, base + N_PIECES + j] = 1.0
    return (jnp.asarray(selk, BF16), jnp.asarray(selq, BF16), jnp.asarray(selm, BF16),
            jnp.asarray(onek), jnp.asarray(oneq), jnp.asarray(onem))


def _proj_kernel(x_ref, w_ref, cw_ref, cb_ref,
                 kc_ref, qct_ref, mvt_ref, ogt_ref, zgt_ref, ufx_ref, rqt_ref, rzg_ref, g_ref,
                 fqt_ref, fvt_ref, tail_sc):
    tm = PROJ_TM

    @pl.when(pl.program_id(1) == 0)
    def _():
        tail_sc[...] = jnp.zeros_like(tail_sc)

    xb = x_ref[...].astype(BF16)

    def mm(c0, n):
        return jnp.dot(xb, w_ref[:, c0:c0 + n], preferred_element_type=F32)

    u = mm(COL_QK, 2 * ML_PW)
    tail = tail_sc[...]
    w = cw_ref[...]
    row8 = lax.broadcasted_iota(jnp.int32, (SUBLANES, 2 * ML_PW), 0)
    y = u * w[CONV_WIDTH - 1:CONV_WIDTH, :] + cb_ref[...]
    for shift in range(1, CONV_WIDTH):
        r = pltpu.roll(u, shift, axis=0)
        rt = pltpu.roll(tail, shift, axis=0)
        first = jnp.where(row8 < shift, rt, r[:SUBLANES])
        xs = jnp.concatenate([first, r[SUBLANES:]], axis=0)
        y = y + xs * w[CONV_WIDTH - 1 - shift:CONV_WIDTH - shift, :]
    tail_sc[...] = u[tm - SUBLANES:, :]
    a = _silu(y)
    qct_ref[...] = a[:, :ML_PW].T.astype(BF16)
    kc_ref[...] = (a[:, ML_PW:] * (ML_HEAD_DIM ** -0.5)).astype(BF16)

    t = mm(COL_MV, 2 * ML_WIDTH)
    mvt_ref[...] = _with_ones_rows(t[:, :ML_WIDTH].T.astype(BF16), ML_HEAD_DIM, ML_HEADS)
    ogt_ref[...] = _sigmoid(t[:, ML_WIDTH:]).T.astype(BF16)

    t = _silu(mm(COL_MZ, ML_WIDTH + FOX_WIDTH))
    zgt_ref[...] = t[:, :ML_WIDTH].T.astype(BF16)
    ufx_ref[:, FOX_WIDTH:] = t[:, ML_WIDTH:].astype(BF16)

    t = mm(COL_FK, 2 * FOX_WIDTH)
    ufx_ref[:, :FOX_WIDTH] = t[:, :FOX_WIDTH].astype(BF16)
    fqt_ref[...] = t[:, FOX_WIDTH:].T.astype(BF16)

    t = mm(COL_FV, FOX_WIDTH + G_COLS)
    fvt_ref[...] = _with_ones_rows(t[:, :FOX_WIDTH].T.astype(BF16), FOX_HEAD_DIM, FOX_HEADS)
    g_ref[...] = t[:, FOX_WIDTH:]

    t = mm(COL_RQ, 2 * MEM_WIDTH)
    rqt_ref[...] = t[:, :MEM_WIDTH].T.astype(BF16)
    rzg_ref[...] = _silu(t[:, MEM_WIDTH:]).astype(BF16)


def _project(x, w_re, conv_w, conv_b, layer):
    b, s, _ = x.shape
    tm = PROJ_TM

    def row(n):
        return pl.BlockSpec((None, tm, n), lambda i, j: (i, j, 0))

    def col(n):
        return pl.BlockSpec((None, n, tm), lambda i, j: (i, 0, j))

    def per_layer(shape):
        return pl.BlockSpec((None,) + shape, lambda i, j: (layer, 0, 0))

    def rows_shape(n):
        return jax.ShapeDtypeStruct((b, s, n), BF16)

    def cols_shape(n):
        return jax.ShapeDtypeStruct((b, n, s), BF16)

    return pl.pallas_call(
        _proj_kernel,
        grid=(b, s // tm),
        in_specs=[row(D_MODEL), per_layer((D_MODEL, W_COLS)),
                  per_layer((CONV_WIDTH, 2 * ML_PW)), per_layer((1, 2 * ML_PW))],
        out_specs=[row(ML_PW), col(ML_PW), col(ML_HEADS * ML_VROWS), col(ML_WIDTH), col(ML_WIDTH),
                   row(2 * FOX_WIDTH), col(MEM_WIDTH), row(MEM_WIDTH), row(G_COLS),
                   col(FOX_WIDTH), col(FOX_HEADS * FOX_VROWS)],
        out_shape=[rows_shape(ML_PW), cols_shape(ML_PW), cols_shape(ML_HEADS * ML_VROWS),
                   cols_shape(ML_WIDTH), cols_shape(ML_WIDTH),
                   rows_shape(2 * FOX_WIDTH), cols_shape(MEM_WIDTH), rows_shape(MEM_WIDTH),
                   jax.ShapeDtypeStruct((b, s, G_COLS), F32),
                   cols_shape(FOX_WIDTH), cols_shape(FOX_HEADS * FOX_VROWS)],
        scratch_shapes=[pltpu.VMEM((SUBLANES, 2 * ML_PW), F32)],
        compiler_params=pltpu.CompilerParams(
            dimension_semantics=("arbitrary", "arbitrary"), vmem_limit_bytes=56 * 1024 * 1024),
        name="in_proj",
    )(x, w_re, conv_w, conv_b)


def _gates_kernel(g_ref, bias_ref, selk_ref, selq_ref, selm_ref, onek_ref, oneq_ref, onem_ref,
                  kb_ref, qbt_ref, maug_ref, mrow_ref, carry_ref):
    @pl.when(pl.program_id(0) == 0)
    def _():
        carry_ref[...] = jnp.zeros_like(carry_ref)

    L = ML_CHUNK
    r = lax.broadcasted_iota(jnp.int32, (L, L), 0)
    c = lax.broadcasted_iota(jnp.int32, (L, L), 1)
    tri = (c <= r).astype(BF16)
    lane = lax.broadcasted_iota(jnp.int32, (L, G_COLS), 1)
    pos = lax.broadcasted_iota(jnp.int32, (SUBLANES, L), 1)

    for bi in range(g_ref.shape[0]):
        x = g_ref[bi] + bias_ref[...]
        log_sig = jnp.minimum(x, 0.0) - jnp.log1p(jnp.exp(-jnp.abs(x)))
        parts = jnp.dot(tri, jnp.concatenate(_split3(log_sig), axis=1),
                        preferred_element_type=F32)
        local = (parts[:, :LANES] + parts[:, LANES:2 * LANES]) + parts[:, 2 * LANES:]
        carry = carry_ref[bi, 0:1, :]
        c_run = local + carry
        carry_ref[bi, 0:1, :] = carry + local[L - 1:L, :]

        pieces = jnp.concatenate(_split3(c_run * LOG2E), axis=1)
        kb_ref[bi] = (jnp.dot(pieces, selk_ref[...], preferred_element_type=F32)
                      + onek_ref[...]).astype(BF16)
        qb = jnp.dot(pieces, selq_ref[...], preferred_element_type=F32) + oneq_ref[...]
        qbt_ref[bi] = qb.T.astype(BF16)

        rc = x - pltpu.roll(local, LANES - (GATE_F - GATE_I), axis=1)
        pieces = jnp.concatenate(_split3(rc), axis=1)
        maug_ref[bi] = (jnp.dot(pieces, selm_ref[...], preferred_element_type=F32)
                        + onem_ref[...]).astype(BF16)
        rows = jnp.where(lane < GATE_F, rc, local).T[GATE_I:GATE_I + 2 * SUBLANES, :]
        cm = rows[:SUBLANES]
        shift = 1
        while shift < L:
            cm = jnp.maximum(cm, jnp.where(pos >= shift, pltpu.roll(cm, shift, axis=1), -jnp.inf))
            shift *= 2
        mrow_ref[bi] = jnp.concatenate([rows, cm], axis=0)


def _gates(g, bias_row, sels):
    b, s, _ = g.shape
    L = ML_CHUNK
    const = lambda shape: pl.BlockSpec(shape, lambda j: (0, 0))
    sel_shape = (N_PIECES * LANES, LANES)
    return pl.pallas_call(
        _gates_kernel,
        grid=(s // L,),
        in_specs=[pl.BlockSpec((b, L, G_COLS), lambda j: (0, j, 0)),
                  const((1, G_COLS)),
                  const(sel_shape), const(sel_shape), const(sel_shape),
                  const((1, LANES)), const((1, LANES)), const((1, LANES))],
        out_specs=[pl.BlockSpec((b, L, LANES), lambda j: (0, j, 0)),
                   pl.BlockSpec((b, LANES, L), lambda j: (0, 0, j)),
                   pl.BlockSpec((b, L, LANES), lambda j: (0, j, 0)),
                   pl.BlockSpec((b, 3 * SUBLANES, L), lambda j: (0, 0, j))],
        out_shape=[jax.ShapeDtypeStruct((b, s, LANES), BF16),
                   jax.ShapeDtypeStruct((b, LANES, s), BF16),
                   jax.ShapeDtypeStruct((b, s, LANES), BF16),
                   jax.ShapeDtypeStruct((b, 3 * SUBLANES, s), F32)],
        scratch_shapes=[pltpu.VMEM((b, SUBLANES, G_COLS), F32)],
        compiler_params=pltpu.CompilerParams(dimension_semantics=("arbitrary",)),
        name="gate_prefix",
    )(g, bias_row, *sels)


def _fox_kernel(qt_ref, qbt_ref, k_ref, kb_ref, vt_ref, zg_ref, o_ref,
                qa_sc, sa_sc, sb_sc, mxa_sc, mxb_sc, m_sc, acc_sc):
    qi = pl.program_id(1)
    tq, tk = FOX_TQ, FOX_TK
    hd = FOX_HEAD_DIM

    brow = lax.broadcasted_iota(jnp.int32, (LANES, tq), 0)
    qbt = qbt_ref[...]
    zslot = jnp.zeros((hd, tq), BF16)
    for h in range(FOX_HEADS):
        qh = qt_ref[h * hd:(h + 1) * hd, :]
        top = [qh, zslot] if h % 2 == 0 else [zslot, qh]
        own = (brow >= SUBLANES * h) & (brow < SUBLANES * (h + 1))
        qa_sc[h] = jnp.concatenate(top + [jnp.where(own, qbt, jnp.zeros_like(qbt))], axis=0)

    m_sc[...] = jnp.full_like(m_sc, -jnp.inf)
    acc_sc[...] = jnp.zeros_like(acc_sc)

    def logits(j, h, dst, mxd, key_offset):
        k0 = pl.multiple_of(j * tk, tk)
        p = h // 2
        kk = jnp.concatenate([k_ref[pl.ds(k0, tk), p * LANES:(p + 1) * LANES],
                              kb_ref[pl.ds(k0, tk), :]], axis=1)
        st = jnp.dot(kk, qa_sc[h], preferred_element_type=F32)
        if key_offset is not None:
            kr = lax.broadcasted_iota(jnp.int32, (tk, tq), 0)
            qc = lax.broadcasted_iota(jnp.int32, (tk, tq), 1)
            st = jnp.where(kr + key_offset <= qc, st, -jnp.inf)
        dst[h] = st
        mxd[h] = jnp.max(st, axis=0, keepdims=True)

    def softmax_pv(j, h, src, mxs):
        k0 = pl.multiple_of(j * tk, tk)
        m_old = m_sc[h]
        m_new = jnp.maximum(m_old, mxs[h])
        alpha = jnp.exp2(m_old - m_new)
        pt = jnp.exp2(src[h] - m_new).astype(BF16)
        vt = vt_ref[h * FOX_VROWS:(h + 1) * FOX_VROWS, pl.ds(k0, tk)]
        acc_sc[h] = alpha * acc_sc[h] + jnp.dot(vt, pt, preferred_element_type=F32)
        m_sc[h] = m_new

    buf_a = (sa_sc, mxa_sc)
    buf_b = (sb_sc, mxb_sc)

    def stage(j_qk, dst, key_offset, j_sm, src):
        if j_qk is not None:
            logits(j_qk, 0, *dst, key_offset)
        for h in range(FOX_HEADS):
            if j_qk is not None and h + 1 < FOX_HEADS:
                logits(j_qk, h + 1, *dst, key_offset)
            if j_sm is not None:
                softmax_pv(j_sm, h, *src)

    stage(2 * qi, buf_a, 0, None, None)
    stage(2 * qi + 1, buf_b, tk, 2 * qi, buf_a)
    stage(0, buf_a, None, 2 * qi + 1, buf_b)

    def body(i, carry):
        stage(2 * i + 1, buf_b, None, 2 * i, buf_a)
        stage(2 * i + 2, buf_a, None, 2 * i + 1, buf_b)
        return carry

    lax.fori_loop(0, qi, body, 0)

    out_t = jnp.concatenate(
        [acc_sc[h, :hd, :] / acc_sc[h, hd:hd + 1, :] for h in range(FOX_HEADS)], axis=0)
    o_ref[...] = (out_t.T * zg_ref[...].astype(F32)).astype(BF16)


def _fox(qt, qbt, ufx, kb, vt):
    b, s, _ = ufx.shape
    tq = FOX_TQ
    return pl.pallas_call(
        _fox_kernel,
        grid=(b, s // tq),
        in_specs=[
            pl.BlockSpec((None, FOX_WIDTH, tq), lambda i, j: (i, 0, j)),
            pl.BlockSpec((None, LANES, tq), lambda i, j: (i, 0, j)),
            pl.BlockSpec((None, s, FOX_WIDTH), lambda i, j: (i, 0, 0)),
            pl.BlockSpec((None, s, LANES), lambda i, j: (i, 0, 0)),
            pl.BlockSpec((None, FOX_HEADS * FOX_VROWS, s), lambda i, j: (i, 0, 0)),
            pl.BlockSpec((None, tq, FOX_WIDTH), lambda i, j: (i, j, 1)),
        ],
        out_specs=pl.BlockSpec((None, tq, FOX_WIDTH), lambda i, j: (i, j, 0)),
        out_shape=jax.ShapeDtypeStruct((b, s, FOX_WIDTH), BF16),
        scratch_shapes=[pltpu.VMEM((FOX_HEADS, 2 * LANES, tq), BF16),
                        pltpu.VMEM((FOX_HEADS, FOX_TK, tq), F32),
                        pltpu.VMEM((FOX_HEADS, FOX_TK, tq), F32),
                        pltpu.VMEM((FOX_HEADS, 1, tq), F32),
                        pltpu.VMEM((FOX_HEADS, 1, tq), F32),
                        pltpu.VMEM((FOX_HEADS, 1, tq), F32),
                        pltpu.VMEM((FOX_HEADS, FOX_VROWS, tq), F32)],
        compiler_params=pltpu.CompilerParams(
            dimension_semantics=("arbitrary", "arbitrary"), vmem_limit_bytes=48 * 1024 * 1024),
        name="fox_attn",
    )(qt, qbt, ufx, kb, vt, ufx)


def _mlstm_kernel(kc_ref, qct_ref, mvt_ref, ogt_ref, zgt_ref, maug_ref, mrow_ref, ng_ref,
                  y_ref, c_sc, m_sc):
    L = ML_CHUNK
    dv = ML_HEAD_DIM

    @pl.when(pl.program_id(1) == 0)
    def _():
        c_sc[...] = jnp.zeros_like(c_sc)
        m_sc[...] = jnp.zeros_like(m_sc)

    kr = lax.broadcasted_iota(jnp.int32, (L, L), 0)
    qc = lax.broadcasted_iota(jnp.int32, (L, L), 1)
    visible = kr <= qc
    brow = lax.broadcasted_iota(jnp.int32, (ML_BIAS_ROWS, L), 0)
    bzero = jnp.zeros((LANES - ML_BIAS_ROWS, L), BF16)
    chains = [(bi, h) for bi in range(ML_NB) for h in range(ML_HEADS)]

    work = []
    for bi, h in chains:
        sc = bi * ML_HEADS + h
        r = mrow_ref[bi, h:h + 1, :]
        b = mrow_ref[bi, SUBLANES + h:SUBLANES + h + 1, :]
        cm = mrow_ref[bi, 2 * SUBLANES + h:2 * SUBLANES + h + 1, :]
        m_prev = m_sc[sc, 0:1, 0:1]
        g = jnp.maximum(m_prev, cm)
        g_last = g[:, L - 1:L]
        g_hi, g_mid, g_lo = (t.astype(F32) for t in _split3(g))
        rel = brow - SUBLANES * h
        blk = jnp.where(rel == N_PIECES, -g_hi,
                        jnp.where(rel == N_PIECES + 1, -g_mid,
                                  jnp.where(rel == N_PIECES + 2, -g_lo,
                                            jnp.where((rel >= 0) & (rel < N_PIECES), 1.0, 0.0))))
        bmat = jnp.concatenate([blk.astype(BF16), bzero], axis=0)
        k_h = kc_ref[bi, :, h * ML_PAD:(h + 1) * ML_PAD]
        q_h = qct_ref[bi, h * ML_PAD:(h + 1) * ML_PAD, :]
        ct = c_sc[sc]
        z = jnp.dot(maug_ref[bi], bmat, preferred_element_type=F32)
        s_raw = jnp.dot(k_h, q_h, preferred_element_type=F32)
        inter = jnp.dot(ct.astype(BF16), q_h, preferred_element_type=F32)
        work.append((r, b, g, g_last, m_prev, k_h, ct, z, s_raw, inter))

    outs = []
    for (bi, h), (r, b, g, g_last, m_prev, k_h, ct, z, s_raw, inter) in zip(chains, work):
        sc = bi * ML_HEADS + h
        d = jnp.exp(jnp.where(visible, z, -jnp.inf))
        p = (s_raw * d).astype(BF16)
        vt = mvt_ref[bi, h * ML_VROWS:(h + 1) * ML_VROWS, :]
        num = jnp.dot(vt, p, preferred_element_type=F32) + jnp.exp(m_prev - g) * inter
        den = num[dv:dv + 1, :]
        hv = num[:dv, :] / jnp.maximum(jnp.abs(den), jnp.exp(-(b + g)))

        wv = (vt.astype(F32) * jnp.exp(r - g_last)).astype(BF16)
        c_sc[sc] = jnp.exp(m_prev - g_last) * ct + jnp.dot(wv, k_h, preferred_element_type=F32)
        m_sc[sc] = jnp.broadcast_to(b[:, L - 1:L] + g_last, (SUBLANES, LANES))

        hv = hv * ogt_ref[bi, h * dv:(h + 1) * dv, :].astype(F32)
        mu = jnp.mean(hv, axis=0, keepdims=True)
        dd = hv - mu
        var = jnp.mean(dd * dd, axis=0, keepdims=True)
        hn = dd * lax.rsqrt(var + LN_EPS) * ng_ref[h * dv:(h + 1) * dv, :]
        outs.append(hn * zgt_ref[bi, h * dv:(h + 1) * dv, :].astype(F32))
        if h == ML_HEADS - 1:
            y_ref[bi] = jnp.concatenate(outs, axis=0).T.astype(BF16)
            outs = []


def _mlstm(kc, qct, mvt, ogt, zgt, maug, mrow, norm_g_b):
    b, s, _ = kc.shape
    L = ML_CHUNK

    def col(n):
        return pl.BlockSpec((ML_NB, n, L), lambda i, j: (i, 0, j))

    def row(n):
        return pl.BlockSpec((ML_NB, L, n), lambda i, j: (i, j, 0))

    return pl.pallas_call(
        _mlstm_kernel,
        grid=(b // ML_NB, s // L),
        in_specs=[row(ML_PW), col(ML_PW), col(ML_HEADS * ML_VROWS), col(ML_WIDTH), col(ML_WIDTH),
                  row(LANES), col(3 * SUBLANES),
                  pl.BlockSpec((ML_WIDTH, L), lambda i, j: (0, 0))],
        out_specs=row(ML_WIDTH),
        out_shape=jax.ShapeDtypeStruct((b, s, ML_WIDTH), BF16),
        scratch_shapes=[pltpu.VMEM((ML_NB * ML_HEADS, ML_VROWS, LANES), F32),
                        pltpu.VMEM((ML_NB * ML_HEADS, SUBLANES, LANES), F32)],
        compiler_params=pltpu.CompilerParams(
            dimension_semantics=("arbitrary", "arbitrary"), vmem_limit_bytes=48 * 1024 * 1024),
        name="mlstm",
    )(kc, qct, mvt, ogt, zgt, maug, mrow, norm_g_b)


def _memkv_kernel(m_ref, w_ref, k4_ref, vt_ref):
    kv = jnp.dot(m_ref[...].astype(BF16), w_ref[...].astype(BF16), preferred_element_type=F32)
    k = kv[:, :MEM_WIDTH] * (MEM_HEAD_DIM ** -0.5)
    lane = lax.broadcasted_iota(jnp.int32, k.shape, 1)
    for h in range(MEM_HEADS):
        own = (lane >= MEM_HEAD_DIM * h) & (lane < MEM_HEAD_DIM * (h + 1))
        k4_ref[h] = jnp.where(own, k, 0.0).astype(BF16)
    vt_ref[...] = _with_ones_rows(kv[:, MEM_WIDTH:].T.astype(BF16), MEM_HEAD_DIM, MEM_HEADS)


def _memkv(mem, w_mem_kv):
    b, ml, _ = mem.shape
    return pl.pallas_call(
        _memkv_kernel,
        grid=(DEPTH, b),
        in_specs=[pl.BlockSpec((None, ml, D_MODEL), lambda l, i: (i, 0, 0)),
                  pl.BlockSpec((None, D_MODEL, 2 * MEM_WIDTH), lambda l, i: (l, 0, 0))],
        out_specs=[pl.BlockSpec((None, None, MEM_HEADS, ml, MEM_WIDTH), lambda l, i: (l, i, 0, 0, 0)),
                   pl.BlockSpec((None, None, MEM_HEADS * MEM_VROWS, ml), lambda l, i: (l, i, 0, 0))],
        out_shape=[jax.ShapeDtypeStruct((DEPTH, b, MEM_HEADS, ml, MEM_WIDTH), BF16),
                   jax.ShapeDtypeStruct((DEPTH, b, MEM_HEADS * MEM_VROWS, ml), BF16)],
        compiler_params=pltpu.CompilerParams(dimension_semantics=("arbitrary", "arbitrary")),
        name="mem_kv",
    )(mem, w_mem_kv)


def _out_kernel(x_ref, yf_ref, ym_ref, rqt_ref, rzg_ref, k4_ref, vt_ref, wo_ref, g_ref, b_ref,
                o_ref, ob_ref):
    hd = MEM_HEAD_DIM
    o_mem = FOX_WIDTH + ML_WIDTH
    n_sub = OUT_TM // OUT_SUB
    ahead = 2

    def rows(q):
        return slice(q * OUT_SUB, (q + 1) * OUT_SUB)

    def main_dot(q):
        y_main = jnp.concatenate([yf_ref[rows(q), :], ym_ref[rows(q), :]], axis=1)
        return jnp.dot(y_main, wo_ref[0:o_mem, :], preferred_element_type=F32)

    rqt = rqt_ref[...]
    sts = [jnp.dot(k4_ref[h], rqt, preferred_element_type=F32) for h in range(MEM_HEADS)]
    mains = [main_dot(q) for q in range(ahead)]

    outs = []
    for h in range(MEM_HEADS):
        st = sts[h]
        p = jnp.exp(st - jnp.max(st, axis=0, keepdims=True)).astype(BF16)
        acc = jnp.dot(vt_ref[h * MEM_VROWS:(h + 1) * MEM_VROWS, :], p,
                      preferred_element_type=F32)
        outs.append(acc[:hd, :] / acc[hd:hd + 1, :])
    att = jnp.concatenate(outs, axis=0).T
    y_mem = (att * rzg_ref[...].astype(F32)).astype(BF16)

    for q in range(n_sub):
        if q + ahead < n_sub:
            mains.append(main_dot(q + ahead))
        y = mains[q] + jnp.dot(y_mem[rows(q), :], wo_ref[o_mem:, :], preferred_element_type=F32)
        r = DEEPNORM_ALPHA * x_ref[rows(q), :] + y
        mu = jnp.mean(r, axis=1, keepdims=True)
        d = r - mu
        var = jnp.mean(d * d, axis=1, keepdims=True)
        out = d * lax.rsqrt(var + LN_EPS) * g_ref[...] + b_ref[...]
        o_ref[rows(q), :] = out
        ob_ref[rows(q), :] = out.astype(BF16)


def _out_proj(x, y_fox, y_ml, rqt, rzg, mem_k4, mem_vt, layer, w_out, ln_g, ln_b):
    b, s, _ = x.shape
    tm = OUT_TM
    ml = mem_k4.shape[3]
    return pl.pallas_call(
        _out_kernel,
        grid=(b, s // tm),
        in_specs=[
            pl.BlockSpec((None, tm, D_MODEL), lambda i, j: (i, j, 0)),
            pl.BlockSpec((None, tm, FOX_WIDTH), lambda i, j: (i, j, 0)),
            pl.BlockSpec((None, tm, ML_WIDTH), lambda i, j: (i, j, 0)),
            pl.BlockSpec((None, MEM_WIDTH, tm), lambda i, j: (i, 0, j)),
            pl.BlockSpec((None, tm, MEM_WIDTH), lambda i, j: (i, j, 0)),
            pl.BlockSpec((None, None, MEM_HEADS, ml, MEM_WIDTH), lambda i, j: (layer, i, 0, 0, 0)),
            pl.BlockSpec((None, None, MEM_HEADS * MEM_VROWS, ml), lambda i, j: (layer, i, 0, 0)),
            pl.BlockSpec((None, D_MODEL, D_MODEL), lambda i, j: (layer, 0, 0)),
            pl.BlockSpec((None, 1, D_MODEL), lambda i, j: (layer, 0, 0)),
            pl.BlockSpec((None, 1, D_MODEL), lambda i, j: (layer, 0, 0)),
        ],
        out_specs=[pl.BlockSpec((None, tm, D_MODEL), lambda i, j: (i, j, 0)),
                   pl.BlockSpec((None, tm, D_MODEL), lambda i, j: (i, j, 0))],
        out_shape=[jax.ShapeDtypeStruct((b, s, D_MODEL), F32),
                   jax.ShapeDtypeStruct((b, s, D_MODEL), BF16)],
        compiler_params=pltpu.CompilerParams(
            dimension_semantics=("arbitrary", "arbitrary"), vmem_limit_bytes=48 * 1024 * 1024),
        name="out_proj_ln",
    )(x, y_fox, y_ml, rqt, rzg, mem_k4, mem_vt, w_out, ln_g, ln_b)


def kernel(x, mem, w_in, fox_f_bias, mlstm_conv_w, mlstm_conv_b, mlstm_i_bias, mlstm_f_bias,
           mlstm_norm_g, w_mem_kv, w_out, ln_g, ln_b):
    w_re = _relayout_w_in(w_in)
    w_out_b = w_out.astype(BF16)
    bias_rows = _spread_gates(fox_f_bias, mlstm_i_bias, mlstm_f_bias)[:, None, :]
    conv_w = jnp.concatenate([_pad_heads_last(mlstm_conv_w[..., :ML_WIDTH]),
                              _pad_heads_last(mlstm_conv_w[..., ML_WIDTH:])], axis=-1)
    conv_b = jnp.concatenate([_pad_heads_last(mlstm_conv_b[..., :ML_WIDTH]),
                              _pad_heads_last(mlstm_conv_b[..., ML_WIDTH:])], axis=-1)[:, None, :]
    norm_g_b = jnp.broadcast_to(mlstm_norm_g[:, :, None], (DEPTH, ML_WIDTH, ML_CHUNK))
    ln_g3 = ln_g[:, None, :]
    ln_b3 = ln_b[:, None, :]
    sels = _bias_selectors()

    mem_k4, mem_vt = _memkv(mem, w_mem_kv)
    x_mxu = x
    for l in range(DEPTH):
        kc, qct, mvt, ogt, zgt, ufx, rqt, rzg, g, fqt, fvt = _project(x_mxu, w_re, conv_w, conv_b, l)
        kb, qbt, maug, mrow = _gates(g, bias_rows[l], sels)
        y_fox = _fox(fqt, qbt, ufx, kb, fvt)
        y_ml = _mlstm(kc, qct, mvt, ogt, zgt, maug, mrow, norm_g_b[l])
        x, x_mxu = _out_proj(x, y_fox, y_ml, rqt, rzg, mem_k4, mem_vt, l, w_out_b, ln_g3, ln_b3)
    return x
```

```python
import math

import numpy as np
import jax
import jax.numpy as jnp
from jax import lax
from jax.experimental import pallas as pl
from jax.experimental.pallas import tpu as pltpu

F32 = jnp.float32
BF16 = jnp.bfloat16

D_MODEL = 1024
DEPTH = 4
FOX_HEADS = 6
FOX_HEAD_DIM = 64
FOX_WIDTH = FOX_HEADS * FOX_HEAD_DIM
ML_HEADS = 4
ML_HEAD_DIM = 96
ML_WIDTH = ML_HEADS * ML_HEAD_DIM
MEM_HEADS = 4
MEM_HEAD_DIM = 64
MEM_WIDTH = MEM_HEADS * MEM_HEAD_DIM
CONV_WIDTH = 4
LN_EPS = 1e-5
DEEPNORM_ALPHA = (2.0 * DEPTH) ** 0.25
LOG2E = math.log2(math.e)

LANES = 128
SUBLANES = 8
BF16_ROWS = 16
ML_PAD = LANES
ML_PW = ML_HEADS * ML_PAD
ML_VROWS = ML_HEAD_DIM + BF16_ROWS
FOX_VROWS = FOX_HEAD_DIM + BF16_ROWS
MEM_VROWS = MEM_HEAD_DIM + BF16_ROWS

COL_QK = 0
COL_MV = COL_QK + 2 * ML_PW
COL_MZ = COL_MV + 2 * ML_WIDTH
COL_FK = COL_MZ + ML_WIDTH + FOX_WIDTH
COL_FV = COL_FK + 2 * FOX_WIDTH
COL_RQ = COL_FV + FOX_WIDTH + LANES
W_COLS = COL_RQ + 2 * MEM_WIDTH
G_COLS = LANES
GATE_I = 48
GATE_F = 56
N_PIECES = 3
ML_BIAS_ROWS = 2 * BF16_ROWS

ML_CHUNK = 256
ML_NB = 4
GATE_CHUNKS = 2
FOX_TQ = 512
FOX_TK = 256
PROJ_TM = 512
OUT_TM = 512
OUT_SUB = 128

_SRC = {}
_off = 0
for _name, _w in (("fq", FOX_WIDTH), ("fk", FOX_WIDTH), ("fv", FOX_WIDTH), ("ff", FOX_HEADS),
                  ("fz", FOX_WIDTH), ("mq", ML_WIDTH), ("mk", ML_WIDTH), ("mv", ML_WIDTH),
                  ("mi", ML_HEADS), ("mf", ML_HEADS), ("mo", ML_WIDTH), ("mz", ML_WIDTH),
                  ("rq", MEM_WIDTH), ("rz", MEM_WIDTH)):
    _SRC[_name] = (_off, _w)
    _off += _w


def _sigmoid(x):
    return 1.0 / (1.0 + jnp.exp2(x * (-LOG2E)))


def _silu(x):
    return x * _sigmoid(x)


def _split3(x):
    hi = x.astype(BF16)
    r1 = x - hi.astype(F32)
    mid = r1.astype(BF16)
    lo = (r1 - mid.astype(F32)).astype(BF16)
    return hi, mid, lo


def _with_ones_rows(t, head_dim, heads):
    ones = jnp.ones((BF16_ROWS, t.shape[1]), t.dtype)
    rows = []
    for h in range(heads):
        rows += [t[h * head_dim:(h + 1) * head_dim, :], ones]
    return jnp.concatenate(rows, axis=0)


def _zeros_like_cols(t, n):
    return jnp.zeros(t.shape[:-1] + (n,), t.dtype)


def _pad_heads_last(t):
    parts = []
    for h in range(ML_HEADS):
        parts.append(t[..., h * ML_HEAD_DIM:(h + 1) * ML_HEAD_DIM])
        parts.append(_zeros_like_cols(t, ML_PAD - ML_HEAD_DIM))
    return jnp.concatenate(parts, axis=-1)


def _spread_gates(ff, mi, mf):
    parts = []
    for h in range(FOX_HEADS):
        parts += [ff[..., h:h + 1], _zeros_like_cols(ff, SUBLANES - 1)]
    parts += [mi, _zeros_like_cols(mi, GATE_F - GATE_I - ML_HEADS),
              mf, _zeros_like_cols(mf, G_COLS - GATE_F - ML_HEADS)]
    return jnp.concatenate(parts, axis=-1)


def _w_in_segments():
    segs = []

    def put(name, dst, scale=1.0):
        o, n = _SRC[name]
        segs.append((o, dst, n, scale))
        return dst + n

    def put_padded_heads(name, dst):
        o, _ = _SRC[name]
        for h in range(ML_HEADS):
            segs.append((o + h * ML_HEAD_DIM, dst + h * ML_PAD, ML_HEAD_DIM, 1.0))
        return dst + ML_PW

    c = put_padded_heads("mq", COL_QK)
    c = put_padded_heads("mk", c)
    for name in ("mv", "mo", "mz", "fz", "fk"):
        c = put(name, c)
    c = put("fq", c, FOX_HEAD_DIM ** -0.5 * LOG2E)
    c = put("fv", c)
    o_ff, o_mi, o_mf = _SRC["ff"][0], _SRC["mi"][0], _SRC["mf"][0]
    for h in range(FOX_HEADS):
        segs.append((o_ff + h, c + SUBLANES * h, 1, 1.0))
    segs.append((o_mi, c + GATE_I, ML_HEADS, 1.0))
    segs.append((o_mf, c + GATE_F, ML_HEADS, 1.0))
    c = put("rq", c + G_COLS)
    c = put("rz", c)
    assert c == W_COLS
    return segs


W_PREP_ROWS = 256


def _w_prep_kernel(w_ref, o_ref):
    o_ref[...] = jnp.zeros_like(o_ref)
    for src, dst, n, scale in _w_in_segments():
        seg = w_ref[:, src:src + n]
        if scale != 1.0:
            seg = seg * scale
        o_ref[:, dst:dst + n] = seg.astype(BF16)


def _relayout_w_in(w):
    depth, rows, cols = w.shape
    tr = W_PREP_ROWS
    return pl.pallas_call(
        _w_prep_kernel,
        grid=(depth, rows // tr),
        in_specs=[pl.BlockSpec((None, tr, cols), lambda l, i: (l, i, 0))],
        out_specs=pl.BlockSpec((None, tr, W_COLS), lambda l, i: (l, i, 0)),
        out_shape=jax.ShapeDtypeStruct((depth, rows, W_COLS), BF16),
        compiler_params=pltpu.CompilerParams(dimension_semantics=("arbitrary", "arbitrary")),
        name="w_in_prep",
    )(w)


def _bias_selectors():
    n = N_PIECES * LANES
    selk = np.zeros((n, LANES), np.float32)
    selq = np.zeros((n, LANES), np.float32)
    selm = np.zeros((n, LANES), np.float32)
    onek = np.zeros((1, LANES), np.float32)
    oneq = np.zeros((1, LANES), np.float32)
    onem = np.zeros((1, LANES), np.float32)
    for h in range(FOX_HEADS):
        base = SUBLANES * h
        for j in range(N_PIECES):
            selk[j * LANES + base, base + j] = -1.0
            selq[j * LANES + base, base + N_PIECES + j] = 1.0
            onek[0, base + N_PIECES + j] = 1.0
            oneq[0, base + j] = 1.0
    for h in range(ML_HEADS):
        base = SUBLANES * h
        for j in range(N_PIECES):
            selm[j * LANES + GATE_I + h, base + j] = 1.0
            onem[0, base + N_PIECES + j] = 1.0
    return (jnp.asarray(selk, BF16), jnp.asarray(selq, BF16), jnp.asarray(selm, BF16),
            jnp.asarray(onek), jnp.asarray(oneq), jnp.asarray(onem))


def _proj_kernel(x_ref, w_ref, cw_ref, cb_ref,
                 kc_ref, qct_ref, mvt_ref, ogt_ref, zgt_ref, ufx_ref, rqt_ref, rzg_ref, g_ref,
                 fqt_ref, fvt_ref, tail_sc):
    tm = PROJ_TM

    @pl.when(pl.program_id(1) == 0)
    def _():
        tail_sc[...] = jnp.zeros_like(tail_sc)

    xb = x_ref[...].astype(BF16)

    def mm(c0, n):
        return jnp.dot(xb, w_ref[:, c0:c0 + n], preferred_element_type=F32)

    u = mm(COL_QK, 2 * ML_PW)
    tail = tail_sc[...]
    w = cw_ref[...]
    row8 = lax.broadcasted_iota(jnp.int32, (SUBLANES, 2 * ML_PW), 0)
    y = u * w[CONV_WIDTH - 1:CONV_WIDTH, :] + cb_ref[...]
    for shift in range(1, CONV_WIDTH):
        r = pltpu.roll(u, shift, axis=0)
        rt = pltpu.roll(tail, shift, axis=0)
        first = jnp.where(row8 < shift, rt, r[:SUBLANES])
        xs = jnp.concatenate([first, r[SUBLANES:]], axis=0)
        y = y + xs * w[CONV_WIDTH - 1 - shift:CONV_WIDTH - shift, :]
    tail_sc[...] = u[tm - SUBLANES:, :]
    a = _silu(y)
    qct_ref[...] = a[:, :ML_PW].T.astype(BF16)
    kc_ref[...] = (a[:, ML_PW:] * (ML_HEAD_DIM ** -0.5)).astype(BF16)

    t = mm(COL_MV, 2 * ML_WIDTH)
    mvt_ref[...] = _with_ones_rows(t[:, :ML_WIDTH].T.astype(BF16), ML_HEAD_DIM, ML_HEADS)
    ogt_ref[...] = _sigmoid(t[:, ML_WIDTH:]).T.astype(BF16)

    t = _silu(mm(COL_MZ, ML_WIDTH + FOX_WIDTH))
    zgt_ref[...] = t[:, :ML_WIDTH].T.astype(BF16)
    ufx_ref[:, FOX_WIDTH:] = t[:, ML_WIDTH:].astype(BF16)

    t = mm(COL_FK, 2 * FOX_WIDTH)
    ufx_ref[:, :FOX_WIDTH] = t[:, :FOX_WIDTH].astype(BF16)
    fqt_ref[...] = t[:, FOX_WIDTH:].T.astype(BF16)

    t = mm(COL_FV, FOX_WIDTH + G_COLS)
    fvt_ref[...] = _with_ones_rows(t[:, :FOX_WIDTH].T.astype(BF16), FOX_HEAD_DIM, FOX_HEADS)
    g_ref[...] = t[:, FOX_WIDTH:]

    t = mm(COL_RQ, 2 * MEM_WIDTH)
    rqt_ref[...] = t[:, :MEM_WIDTH].T.astype(BF16)
    rzg_ref[...] = _silu(t[:, MEM_WIDTH:]).astype(BF16)


def _project(x, w_re, conv_w, conv_b, layer):
    b, s, _ = x.shape
    tm = PROJ_TM

    def row(n):
        return pl.BlockSpec((None, tm, n), lambda i, j: (i, j, 0))

    def col(n):
        return pl.BlockSpec((None, n, tm), lambda i, j: (i, 0, j))

    def per_layer(shape):
        return pl.BlockSpec((None,) + shape, lambda i, j: (layer, 0, 0))

    def rows_shape(n):
        return jax.ShapeDtypeStruct((b, s, n), BF16)

    def cols_shape(n):
        return jax.ShapeDtypeStruct((b, n, s), BF16)

    return pl.pallas_call(
        _proj_kernel,
        grid=(b, s // tm),
        in_specs=[row(D_MODEL), per_layer((D_MODEL, W_COLS)),
                  per_layer((CONV_WIDTH, 2 * ML_PW)), per_layer((1, 2 * ML_PW))],
        out_specs=[row(ML_PW), col(ML_PW), col(ML_HEADS * ML_VROWS), col(ML_WIDTH), col(ML_WIDTH),
                   row(2 * FOX_WIDTH), col(MEM_WIDTH), row(MEM_WIDTH), row(G_COLS),
                   col(FOX_WIDTH), col(FOX_HEADS * FOX_VROWS)],
        out_shape=[rows_shape(ML_PW), cols_shape(ML_PW), cols_shape(ML_HEADS * ML_VROWS),
                   cols_shape(ML_WIDTH), cols_shape(ML_WIDTH),
                   rows_shape(2 * FOX_WIDTH), cols_shape(MEM_WIDTH), rows_shape(MEM_WIDTH),
                   jax.ShapeDtypeStruct((b, s, G_COLS), F32),
                   cols_shape(FOX_WIDTH), cols_shape(FOX_HEADS * FOX_VROWS)],
        scratch_shapes=[pltpu.VMEM((SUBLANES, 2 * ML_PW), F32)],
        compiler_params=pltpu.CompilerParams(
            dimension_semantics=("arbitrary", "arbitrary"), vmem_limit_bytes=56 * 1024 * 1024),
        name="in_proj",
    )(x, w_re, conv_w, conv_b)


def _gates_kernel(g_ref, bias_ref, selk_ref, selq_ref, selm_ref, onek_ref, oneq_ref, onem_ref,
                  kb_ref, qbt_ref, maug_ref, mrow_ref, carry_ref):
    @pl.when(pl.program_id(0) == 0)
    def _():
        carry_ref[...] = jnp.zeros_like(carry_ref)

    L = ML_CHUNK
    r = lax.broadcasted_iota(jnp.int32, (L, L), 0)
    c = lax.broadcasted_iota(jnp.int32, (L, L), 1)
    tri = (c <= r).astype(BF16)
    lane = lax.broadcasted_iota(jnp.int32, (L, G_COLS), 1)
    pos = lax.broadcasted_iota(jnp.int32, (SUBLANES, L), 1)

    chains = [(bi, ci) for bi in range(g_ref.shape[0]) for ci in range(GATE_CHUNKS)]

    def span(ci):
        return slice(ci * L, (ci + 1) * L)

    xs, parts = [], []
    for bi, ci in chains:
        x = g_ref[bi, span(ci), :] + bias_ref[...]
        log_sig = jnp.minimum(x, 0.0) - jnp.log1p(jnp.exp(-jnp.abs(x)))
        xs.append(x)
        parts.append(jnp.dot(tri, jnp.concatenate(_split3(log_sig), axis=1),
                             preferred_element_type=F32))

    kbs, qbs, maugs, rowss = [], [], [], []
    carry = None
    for (bi, ci), x, p in zip(chains, xs, parts):
        if ci == 0:
            carry = carry_ref[bi, 0:1, :]
        local = (p[:, :LANES] + p[:, LANES:2 * LANES]) + p[:, 2 * LANES:]
        c_run = local + carry
        carry = carry + local[L - 1:L, :]
        if ci == GATE_CHUNKS - 1:
            carry_ref[bi, 0:1, :] = carry
        pieces = jnp.concatenate(_split3(c_run * LOG2E), axis=1)
        kbs.append(jnp.dot(pieces, selk_ref[...], preferred_element_type=F32))
        qbs.append(jnp.dot(pieces, selq_ref[...], preferred_element_type=F32))
        rc = x - pltpu.roll(local, LANES - (GATE_F - GATE_I), axis=1)
        pieces = jnp.concatenate(_split3(rc), axis=1)
        maugs.append(jnp.dot(pieces, selm_ref[...], preferred_element_type=F32))
        rowss.append(jnp.where(lane < GATE_F, rc, local))

    for (bi, ci), kb, qb, maug, rows in zip(chains, kbs, qbs, maugs, rowss):
        kb_ref[bi, span(ci), :] = (kb + onek_ref[...]).astype(BF16)
        qbt_ref[bi, :, span(ci)] = (qb + oneq_ref[...]).T.astype(BF16)
        maug_ref[bi, span(ci), :] = (maug + onem_ref[...]).astype(BF16)
        rows = rows.T[GATE_I:GATE_I + 2 * SUBLANES, :]
        cm = rows[:SUBLANES]
        shift = 1
        while shift < L:
            cm = jnp.maximum(cm, jnp.where(pos >= shift, pltpu.roll(cm, shift, axis=1), -jnp.inf))
            shift *= 2
        mrow_ref[bi, :, span(ci)] = jnp.concatenate([rows, cm], axis=0)


def _gates(g, bias_row, sels):
    b, s, _ = g.shape
    L = GATE_CHUNKS * ML_CHUNK
    const = lambda shape: pl.BlockSpec(shape, lambda j: (0, 0))
    sel_shape = (N_PIECES * LANES, LANES)
    return pl.pallas_call(
        _gates_kernel,
        grid=(s // L,),
        in_specs=[pl.BlockSpec((b, L, G_COLS), lambda j: (0, j, 0)),
                  const((1, G_COLS)),
                  const(sel_shape), const(sel_shape), const(sel_shape),
                  const((1, LANES)), const((1, LANES)), const((1, LANES))],
        out_specs=[pl.BlockSpec((b, L, LANES), lambda j: (0, j, 0)),
                   pl.BlockSpec((b, LANES, L), lambda j: (0, 0, j)),
                   pl.BlockSpec((b, L, LANES), lambda j: (0, j, 0)),
                   pl.BlockSpec((b, 3 * SUBLANES, L), lambda j: (0, 0, j))],
        out_shape=[jax.ShapeDtypeStruct((b, s, LANES), BF16),
                   jax.ShapeDtypeStruct((b, LANES, s), BF16),
                   jax.ShapeDtypeStruct((b, s, LANES), BF16),
                   jax.ShapeDtypeStruct((b, 3 * SUBLANES, s), F32)],
        scratch_shapes=[pltpu.VMEM((b, SUBLANES, G_COLS), F32)],
        compiler_params=pltpu.CompilerParams(dimension_semantics=("arbitrary",)),
        name="gate_prefix",
    )(g, bias_row, *sels)


def _fox_kernel(qt_ref, qbt_ref, k_ref, kb_ref, vt_ref, zg_ref, o_ref,
                qa_sc, sa_sc, sb_sc, mxa_sc, mxb_sc, m_sc, acc_sc):
    qi = pl.program_id(1)
    tq, tk = FOX_TQ, FOX_TK
    hd = FOX_HEAD_DIM

    brow = lax.broadcasted_iota(jnp.int32, (LANES, tq), 0)
    qbt = qbt_ref[...]
    zslot = jnp.zeros((hd, tq), BF16)
    for h in range(FOX_HEADS):
        qh = qt_ref[h * hd:(h + 1) * hd, :]
        top = [qh, zslot] if h % 2 == 0 else [zslot, qh]
        own = (brow >= SUBLANES * h) & (brow < SUBLANES * (h + 1))
        qa_sc[h] = jnp.concatenate(top + [jnp.where(own, qbt, jnp.zeros_like(qbt))], axis=0)

    m_sc[...] = jnp.full_like(m_sc, -jnp.inf)
    acc_sc[...] = jnp.zeros_like(acc_sc)

    def logits(j, h, dst, mxd, key_offset):
        k0 = pl.multiple_of(j * tk, tk)
        p = h // 2
        kk = jnp.concatenate([k_ref[pl.ds(k0, tk), p * LANES:(p + 1) * LANES],
                              kb_ref[pl.ds(k0, tk), :]], axis=1)
        st = jnp.dot(kk, qa_sc[h], preferred_element_type=F32)
        if key_offset is not None:
            kr = lax.broadcasted_iota(jnp.int32, (tk, tq), 0)
            qc = lax.broadcasted_iota(jnp.int32, (tk, tq), 1)
            st = jnp.where(kr + key_offset <= qc, st, -jnp.inf)
        dst[h] = st
        mxd[h] = jnp.max(st, axis=0, keepdims=True)

    def softmax_pv(j, h, src, mxs):
        k0 = pl.multiple_of(j * tk, tk)
        m_old = m_sc[h]
        m_new = jnp.maximum(m_old, mxs[h])
        alpha = jnp.exp2(m_old - m_new)
        pt = jnp.exp2(src[h] - m_new).astype(BF16)
        vt = vt_ref[h * FOX_VROWS:(h + 1) * FOX_VROWS, pl.ds(k0, tk)]
        acc_sc[h] = alpha * acc_sc[h] + jnp.dot(vt, pt, preferred_element_type=F32)
        m_sc[h] = m_new

    buf_a = (sa_sc, mxa_sc)
    buf_b = (sb_sc, mxb_sc)

    def stage(j_qk, dst, key_offset, j_sm, src):
        if j_qk is not None:
            logits(j_qk, 0, *dst, key_offset)
        for h in range(FOX_HEADS):
            if j_qk is not None and h + 1 < FOX_HEADS:
                logits(j_qk, h + 1, *dst, key_offset)
            if j_sm is not None:
                softmax_pv(j_sm, h, *src)

    stage(2 * qi, buf_a, 0, None, None)
    stage(2 * qi + 1, buf_b, tk, 2 * qi, buf_a)
    stage(0, buf_a, None, 2 * qi + 1, buf_b)

    def body(i, carry):
        stage(2 * i + 1, buf_b, None, 2 * i, buf_a)
        stage(2 * i + 2, buf_a, None, 2 * i + 1, buf_b)
        return carry

    lax.fori_loop(0, qi, body, 0)

    out_t = jnp.concatenate(
        [acc_sc[h, :hd, :] / acc_sc[h, hd:hd + 1, :] for h in range(FOX_HEADS)], axis=0)
    o_ref[...] = (out_t.T * zg_ref[...].astype(F32)).astype(BF16)


def _fox(qt, qbt, ufx, kb, vt):
    b, s, _ = ufx.shape
    tq = FOX_TQ
    return pl.pallas_call(
        _fox_kernel,
        grid=(b, s // tq),
        in_specs=[
            pl.BlockSpec((None, FOX_WIDTH, tq), lambda i, j: (i, 0, j)),
            pl.BlockSpec((None, LANES, tq), lambda i, j: (i, 0, j)),
            pl.BlockSpec((None, s, FOX_WIDTH), lambda i, j: (i, 0, 0)),
            pl.BlockSpec((None, s, LANES), lambda i, j: (i, 0, 0)),
            pl.BlockSpec((None, FOX_HEADS * FOX_VROWS, s), lambda i, j: (i, 0, 0)),
            pl.BlockSpec((None, tq, FOX_WIDTH), lambda i, j: (i, j, 1)),
        ],
        out_specs=pl.BlockSpec((None, tq, FOX_WIDTH), lambda i, j: (i, j, 0)),
        out_shape=jax.ShapeDtypeStruct((b, s, FOX_WIDTH), BF16),
        scratch_shapes=[pltpu.VMEM((FOX_HEADS, 2 * LANES, tq), BF16),
                        pltpu.VMEM((FOX_HEADS, FOX_TK, tq), F32),
                        pltpu.VMEM((FOX_HEADS, FOX_TK, tq), F32),
                        pltpu.VMEM((FOX_HEADS, 1, tq), F32),
                        pltpu.VMEM((FOX_HEADS, 1, tq), F32),
                        pltpu.VMEM((FOX_HEADS, 1, tq), F32),
                        pltpu.VMEM((FOX_HEADS, FOX_VROWS, tq), F32)],
        compiler_params=pltpu.CompilerParams(
            dimension_semantics=("arbitrary", "arbitrary"), vmem_limit_bytes=48 * 1024 * 1024),
        name="fox_attn",
    )(qt, qbt, ufx, kb, vt, ufx)


def _mlstm_kernel(kc_ref, qct_ref, mvt_ref, ogt_ref, zgt_ref, maug_ref, mrow_ref, ng_ref,
                  y_ref, c_sc, m_sc):
    L = ML_CHUNK
    dv = ML_HEAD_DIM

    @pl.when(pl.program_id(1) == 0)
    def _():
        c_sc[...] = jnp.zeros_like(c_sc)
        m_sc[...] = jnp.zeros_like(m_sc)

    kr = lax.broadcasted_iota(jnp.int32, (L, L), 0)
    qc = lax.broadcasted_iota(jnp.int32, (L, L), 1)
    visible = kr <= qc
    brow = lax.broadcasted_iota(jnp.int32, (ML_BIAS_ROWS, L), 0)
    bzero = jnp.zeros((LANES - ML_BIAS_ROWS, L), BF16)
    chains = [(bi, h) for bi in range(ML_NB) for h in range(ML_HEADS)]

    work = []
    for bi, h in chains:
        sc = bi * ML_HEADS + h
        r = mrow_ref[bi, h:h + 1, :]
        b = mrow_ref[bi, SUBLANES + h:SUBLANES + h + 1, :]
        cm = mrow_ref[bi, 2 * SUBLANES + h:2 * SUBLANES + h + 1, :]
        m_prev = m_sc[sc, 0:1, 0:1]
        g = jnp.maximum(m_prev, cm)
        g_last = g[:, L - 1:L]
        g_hi, g_mid, g_lo = (t.astype(F32) for t in _split3(g))
        rel = brow - SUBLANES * h
        blk = jnp.where(rel == N_PIECES, -g_hi,
                        jnp.where(rel == N_PIECES + 1, -g_mid,
                                  jnp.where(rel == N_PIECES + 2, -g_lo,
                                            jnp.where((rel >= 0) & (rel < N_PIECES), 1.0, 0.0))))
        bmat = jnp.concatenate([blk.astype(BF16), bzero], axis=0)
        k_h = kc_ref[bi, :, h * ML_PAD:(h + 1) * ML_PAD]
        q_h = qct_ref[bi, h * ML_PAD:(h + 1) * ML_PAD, :]
        ct = c_sc[sc]
        z = jnp.dot(maug_ref[bi], bmat, preferred_element_type=F32)
        s_raw = jnp.dot(k_h, q_h, preferred_element_type=F32)
        inter = jnp.dot(ct.astype(BF16), q_h, preferred_element_type=F32)
        work.append((r, b, g, g_last, m_prev, k_h, ct, z, s_raw, inter))

    outs = []
    for (bi, h), (r, b, g, g_last, m_prev, k_h, ct, z, s_raw, inter) in zip(chains, work):
        sc = bi * ML_HEADS + h
        d = jnp.exp(jnp.where(visible, z, -jnp.inf))
        p = (s_raw * d).astype(BF16)
        vt = mvt_ref[bi, h * ML_VROWS:(h + 1) * ML_VROWS, :]
        num = jnp.dot(vt, p, preferred_element_type=F32) + jnp.exp(m_prev - g) * inter
        den = num[dv:dv + 1, :]
        hv = num[:dv, :] / jnp.maximum(jnp.abs(den), jnp.exp(-(b + g)))

        wv = (vt.astype(F32) * jnp.exp(r - g_last)).astype(BF16)
        c_sc[sc] = jnp.exp(m_prev - g_last) * ct + jnp.dot(wv, k_h, preferred_element_type=F32)
        m_sc[sc] = jnp.broadcast_to(b[:, L - 1:L] + g_last, (SUBLANES, LANES))

        hv = hv * ogt_ref[bi, h * dv:(h + 1) * dv, :].astype(F32)
        mu = jnp.mean(hv, axis=0, keepdims=True)
        dd = hv - mu
        var = jnp.mean(dd * dd, axis=0, keepdims=True)
        hn = dd * lax.rsqrt(var + LN_EPS) * ng_ref[h * dv:(h + 1) * dv, :]
        outs.append(hn * zgt_ref[bi, h * dv:(h + 1) * dv, :].astype(F32))
        if h == ML_HEADS - 1:
            y_ref[bi] = jnp.concatenate(outs, axis=0).T.astype(BF16)
            outs = []


def _mlstm(kc, qct, mvt, ogt, zgt, maug, mrow, norm_g_b):
    b, s, _ = kc.shape
    L = ML_CHUNK

    def col(n):
        return pl.BlockSpec((ML_NB, n, L), lambda i, j: (i, 0, j))

    def row(n):
        return pl.BlockSpec((ML_NB, L, n), lambda i, j: (i, j, 0))

    return pl.pallas_call(
        _mlstm_kernel,
        grid=(b // ML_NB, s // L),
        in_specs=[row(ML_PW), col(ML_PW), col(ML_HEADS * ML_VROWS), col(ML_WIDTH), col(ML_WIDTH),
                  row(LANES), col(3 * SUBLANES),
                  pl.BlockSpec((ML_WIDTH, L), lambda i, j: (0, 0))],
        out_specs=row(ML_WIDTH),
        out_shape=jax.ShapeDtypeStruct((b, s, ML_WIDTH), BF16),
        scratch_shapes=[pltpu.VMEM((ML_NB * ML_HEADS, ML_VROWS, LANES), F32),
                        pltpu.VMEM((ML_NB * ML_HEADS, SUBLANES, LANES), F32)],
        compiler_params=pltpu.CompilerParams(
            dimension_semantics=("arbitrary", "arbitrary"), vmem_limit_bytes=48 * 1024 * 1024),
        name="mlstm",
    )(kc, qct, mvt, ogt, zgt, maug, mrow, norm_g_b)


def _memkv_kernel(m_ref, w_ref, k4_ref, vt_ref):
    kv = jnp.dot(m_ref[...].astype(BF16), w_ref[...].astype(BF16), preferred_element_type=F32)
    k = kv[:, :MEM_WIDTH] * (MEM_HEAD_DIM ** -0.5)
    lane = lax.broadcasted_iota(jnp.int32, k.shape, 1)
    for h in range(MEM_HEADS):
        own = (lane >= MEM_HEAD_DIM * h) & (lane < MEM_HEAD_DIM * (h + 1))
        k4_ref[h] = jnp.where(own, k, 0.0).astype(BF16)
    vt_ref[...] = _with_ones_rows(kv[:, MEM_WIDTH:].T.astype(BF16), MEM_HEAD_DIM, MEM_HEADS)


def _memkv(mem, w_mem_kv):
    b, ml, _ = mem.shape
    return pl.pallas_call(
        _memkv_kernel,
        grid=(DEPTH, b),
        in_specs=[pl.BlockSpec((None, ml, D_MODEL), lambda l, i: (i, 0, 0)),
                  pl.BlockSpec((None, D_MODEL, 2 * MEM_WIDTH), lambda l, i: (l, 0, 0))],
        out_specs=[pl.BlockSpec((None, None, MEM_HEADS, ml, MEM_WIDTH), lambda l, i: (l, i, 0, 0, 0)),
                   pl.BlockSpec((None, None, MEM_HEADS * MEM_VROWS, ml), lambda l, i: (l, i, 0, 0))],
        out_shape=[jax.ShapeDtypeStruct((DEPTH, b, MEM_HEADS, ml, MEM_WIDTH), BF16),
                   jax.ShapeDtypeStruct((DEPTH, b, MEM_HEADS * MEM_VROWS, ml), BF16)],
        compiler_params=pltpu.CompilerParams(dimension_semantics=("arbitrary", "arbitrary")),
        name="mem_kv",
    )(mem, w_mem_kv)


def _out_kernel(x_ref, yf_ref, ym_ref, rqt_ref, rzg_ref, k4_ref, vt_ref, wo_ref, g_ref, b_ref,
                o_ref, ob_ref):
    hd = MEM_HEAD_DIM
    o_mem = FOX_WIDTH + ML_WIDTH
    n_sub = OUT_TM // OUT_SUB
    ahead = 2

    def rows(q):
        return slice(q * OUT_SUB, (q + 1) * OUT_SUB)

    def main_dot(q):
        y_main = jnp.concatenate([yf_ref[rows(q), :], ym_ref[rows(q), :]], axis=1)
        return jnp.dot(y_main, wo_ref[0:o_mem, :], preferred_element_type=F32)

    rqt = rqt_ref[...]
    sts = [jnp.dot(k4_ref[h], rqt, preferred_element_type=F32) for h in range(MEM_HEADS)]
    mains = [main_dot(q) for q in range(ahead)]

    outs = []
    for h in range(MEM_HEADS):
        st = sts[h]
        p = jnp.exp(st - jnp.max(st, axis=0, keepdims=True)).astype(BF16)
        acc = jnp.dot(vt_ref[h * MEM_VROWS:(h + 1) * MEM_VROWS, :], p,
                      preferred_element_type=F32)
        outs.append(acc[:hd, :] / acc[hd:hd + 1, :])
    att = jnp.concatenate(outs, axis=0).T
    y_mem = (att * rzg_ref[...].astype(F32)).astype(BF16)

    for q in range(n_sub):
        if q + ahead < n_sub:
            mains.append(main_dot(q + ahead))
        y = mains[q] + jnp.dot(y_mem[rows(q), :], wo_ref[o_mem:, :], preferred_element_type=F32)
        r = DEEPNORM_ALPHA * x_ref[rows(q), :] + y
        mu = jnp.mean(r, axis=1, keepdims=True)
        d = r - mu
        var = jnp.mean(d * d, axis=1, keepdims=True)
        out = d * lax.rsqrt(var + LN_EPS) * g_ref[...] + b_ref[...]
        o_ref[rows(q), :] = out
        ob_ref[rows(q), :] = out.astype(BF16)


def _out_proj(x, y_fox, y_ml, rqt, rzg, mem_k4, mem_vt, layer, w_out, ln_g, ln_b):
    b, s, _ = x.shape
    tm = OUT_TM
    ml = mem_k4.shape[3]
    return pl.pallas_call(
        _out_kernel,
        grid=(b, s // tm),
        in_specs=[
            pl.BlockSpec((None, tm, D_MODEL), lambda i, j: (i, j, 0)),
            pl.BlockSpec((None, tm, FOX_WIDTH), lambda i, j: (i, j, 0)),
            pl.BlockSpec((None, tm, ML_WIDTH), lambda i, j: (i, j, 0)),
            pl.BlockSpec((None, MEM_WIDTH, tm), lambda i, j: (i, 0, j)),
            pl.BlockSpec((None, tm, MEM_WIDTH), lambda i, j: (i, j, 0)),
            pl.BlockSpec((None, None, MEM_HEADS, ml, MEM_WIDTH), lambda i, j: (layer, i, 0, 0, 0)),
            pl.BlockSpec((None, None, MEM_HEADS * MEM_VROWS, ml), lambda i, j: (layer, i, 0, 0)),
            pl.BlockSpec((None, D_MODEL, D_MODEL), lambda i, j: (layer, 0, 0)),
            pl.BlockSpec((None, 1, D_MODEL), lambda i, j: (layer, 0, 0)),
            pl.BlockSpec((None, 1, D_MODEL), lambda i, j: (layer, 0, 0)),
        ],
        out_specs=[pl.BlockSpec((None, tm, D_MODEL), lambda i, j: (i, j, 0)),
                   pl.BlockSpec((None, tm, D_MODEL), lambda i, j: (i, j, 0))],
        out_shape=[jax.ShapeDtypeStruct((b, s, D_MODEL), F32),
                   jax.ShapeDtypeStruct((b, s, D_MODEL), BF16)],
        compiler_params=pltpu.CompilerParams(
            dimension_semantics=("arbitrary", "arbitrary"), vmem_limit_bytes=48 * 1024 * 1024),
        name="out_proj_ln",
    )(x, y_fox, y_ml, rqt, rzg, mem_k4, mem_vt, w_out, ln_g, ln_b)


def kernel(x, mem, w_in, fox_f_bias, mlstm_conv_w, mlstm_conv_b, mlstm_i_bias, mlstm_f_bias,
           mlstm_norm_g, w_mem_kv, w_out, ln_g, ln_b):
    w_re = _relayout_w_in(w_in)
    w_out_b = w_out.astype(BF16)
    bias_rows = _spread_gates(fox_f_bias, mlstm_i_bias, mlstm_f_bias)[:, None, :]
    conv_w = jnp.concatenate([_pad_heads_last(mlstm_conv_w[..., :ML_WIDTH]),
                              _pad_heads_last(mlstm_conv_w[..., ML_WIDTH:])], axis=-1)
    conv_b = jnp.concatenate([_pad_heads_last(mlstm_conv_b[..., :ML_WIDTH]),
                              _pad_heads_last(mlstm_conv_b[..., ML_WIDTH:])], axis=-1)[:, None, :]
    norm_g_b = jnp.broadcast_to(mlstm_norm_g[:, :, None], (DEPTH, ML_WIDTH, ML_CHUNK))
    ln_g3 = ln_g[:, None, :]
    ln_b3 = ln_b[:, None, :]
    sels = _bias_selectors()

    mem_k4, mem_vt = _memkv(mem, w_mem_kv)
    x_mxu = x
    for l in range(DEPTH):
        kc, qct, mvt, ogt, zgt, ufx, rqt, rzg, g, fqt, fvt = _project(x_mxu, w_re, conv_w, conv_b, l)
        kb, qbt, maug, mrow = _gates(g, bias_rows[l], sels)
        y_fox = _fox(fqt, qbt, ufx, kb, fvt)
        y_ml = _mlstm(kc, qct, mvt, ogt, zgt, maug, mrow, norm_g_b[l])
        x, x_mxu = _out_proj(x, y_fox, y_ml, rqt, rzg, mem_k4, mem_vt, l, w_out_b, ln_g3, ln_b3)
    return x
```

```python
import math

import numpy as np
import jax
import jax.numpy as jnp
from jax import lax
from jax.experimental import pallas as pl
from jax.experimental.pallas import tpu as pltpu

F32 = jnp.float32
BF16 = jnp.bfloat16

D_MODEL = 1024
DEPTH = 4
FOX_HEADS = 6
FOX_HEAD_DIM = 64
FOX_WIDTH = FOX_HEADS * FOX_HEAD_DIM
ML_HEADS = 4
ML_HEAD_DIM = 96
ML_WIDTH = ML_HEADS * ML_HEAD_DIM
MEM_HEADS = 4
MEM_HEAD_DIM = 64
MEM_WIDTH = MEM_HEADS * MEM_HEAD_DIM
CONV_WIDTH = 4
LN_EPS = 1e-5
DEEPNORM_ALPHA = (2.0 * DEPTH) ** 0.25
LOG2E = math.log2(math.e)

LANES = 128
SUBLANES = 8
BF16_ROWS = 16
ML_PAD = LANES
ML_PW = ML_HEADS * ML_PAD
ML_VROWS = ML_HEAD_DIM + BF16_ROWS
FOX_VROWS = FOX_HEAD_DIM + BF16_ROWS
MEM_VROWS = MEM_HEAD_DIM + BF16_ROWS

COL_QK = 0
COL_MV = COL_QK + 2 * ML_PW
COL_MZ = COL_MV + 2 * ML_WIDTH
COL_FK = COL_MZ + ML_WIDTH + FOX_WIDTH
COL_FV = COL_FK + 2 * FOX_WIDTH
COL_RQ = COL_FV + FOX_WIDTH + LANES
W_COLS = COL_RQ + 2 * MEM_WIDTH
G_COLS = LANES
GATE_I = 48
GATE_F = 56
N_PIECES = 3
ML_BIAS_ROWS = 2 * BF16_ROWS

ML_CHUNK = 256
ML_NB = 4
GATE_CHUNKS = 2
FOX_TQ = 512
FOX_TK = 256
PROJ_TM = 512
OUT_TM = 1024
OUT_SUB = 128

_SRC = {}
_off = 0
for _name, _w in (("fq", FOX_WIDTH), ("fk", FOX_WIDTH), ("fv", FOX_WIDTH), ("ff", FOX_HEADS),
                  ("fz", FOX_WIDTH), ("mq", ML_WIDTH), ("mk", ML_WIDTH), ("mv", ML_WIDTH),
                  ("mi", ML_HEADS), ("mf", ML_HEADS), ("mo", ML_WIDTH), ("mz", ML_WIDTH),
                  ("rq", MEM_WIDTH), ("rz", MEM_WIDTH)):
    _SRC[_name] = (_off, _w)
    _off += _w


def _sigmoid(x):
    return 1.0 / (1.0 + jnp.exp2(x * (-LOG2E)))


def _silu(x):
    return x * _sigmoid(x)


def _split3(x):
    hi = x.astype(BF16)
    r1 = x - hi.astype(F32)
    mid = r1.astype(BF16)
    lo = (r1 - mid.astype(F32)).astype(BF16)
    return hi, mid, lo


def _with_ones_rows(t, head_dim, heads):
    ones = jnp.ones((BF16_ROWS, t.shape[1]), t.dtype)
    rows = []
    for h in range(heads):
        rows += [t[h * head_dim:(h + 1) * head_dim, :], ones]
    return jnp.concatenate(rows, axis=0)


def _zeros_like_cols(t, n):
    return jnp.zeros(t.shape[:-1] + (n,), t.dtype)


def _pad_heads_last(t):
    parts = []
    for h in range(ML_HEADS):
        parts.append(t[..., h * ML_HEAD_DIM:(h + 1) * ML_HEAD_DIM])
        parts.append(_zeros_like_cols(t, ML_PAD - ML_HEAD_DIM))
    return jnp.concatenate(parts, axis=-1)


def _spread_gates(ff, mi, mf):
    parts = []
    for h in range(FOX_HEADS):
        parts += [ff[..., h:h + 1], _zeros_like_cols(ff, SUBLANES - 1)]
    parts += [mi, _zeros_like_cols(mi, GATE_F - GATE_I - ML_HEADS),
              mf, _zeros_like_cols(mf, G_COLS - GATE_F - ML_HEADS)]
    return jnp.concatenate(parts, axis=-1)


def _w_in_segments():
    segs = []

    def put(name, dst, scale=1.0):
        o, n = _SRC[name]
        segs.append((o, dst, n, scale))
        return dst + n

    def put_padded_heads(name, dst):
        o, _ = _SRC[name]
        for h in range(ML_HEADS):
            segs.append((o + h * ML_HEAD_DIM, dst + h * ML_PAD, ML_HEAD_DIM, 1.0))
        return dst + ML_PW

    c = put_padded_heads("mq", COL_QK)
    c = put_padded_heads("mk", c)
    for name in ("mv", "mo", "mz", "fz", "fk"):
        c = put(name, c)
    c = put("fq", c, FOX_HEAD_DIM ** -0.5 * LOG2E)
    c = put("fv", c)
    o_ff, o_mi, o_mf = _SRC["ff"][0], _SRC["mi"][0], _SRC["mf"][0]
    for h in range(FOX_HEADS):
        segs.append((o_ff + h, c + SUBLANES * h, 1, 1.0))
    segs.append((o_mi, c + GATE_I, ML_HEADS, 1.0))
    segs.append((o_mf, c + GATE_F, ML_HEADS, 1.0))
    c = put("rq", c + G_COLS)
    c = put("rz", c)
    assert c == W_COLS
    return segs


W_PREP_ROWS = 256


def _w_prep_kernel(w_ref, o_ref):
    o_ref[...] = jnp.zeros_like(o_ref)
    for src, dst, n, scale in _w_in_segments():
        seg = w_ref[:, src:src + n]
        if scale != 1.0:
            seg = seg * scale
        o_ref[:, dst:dst + n] = seg.astype(BF16)


def _relayout_w_in(w):
    depth, rows, cols = w.shape
    tr = W_PREP_ROWS
    return pl.pallas_call(
        _w_prep_kernel,
        grid=(depth, rows // tr),
        in_specs=[pl.BlockSpec((None, tr, cols), lambda l, i: (l, i, 0))],
        out_specs=pl.BlockSpec((None, tr, W_COLS), lambda l, i: (l, i, 0)),
        out_shape=jax.ShapeDtypeStruct((depth, rows, W_COLS), BF16),
        compiler_params=pltpu.CompilerParams(dimension_semantics=("arbitrary", "arbitrary")),
        name="w_in_prep",
    )(w)


def _fox_bias_base(h):
    return SUBLANES * (h // 2) + (FOX_HEAD_DIM if h % 2 == 0 else 0)


def _bias_selectors():
    n = N_PIECES * LANES
    selk = np.zeros((n, LANES), np.float32)
    selq = np.zeros((n, LANES), np.float32)
    selm = np.zeros((n, LANES), np.float32)
    onek = np.zeros((1, LANES), np.float32)
    oneq = np.zeros((1, LANES), np.float32)
    onem = np.zeros((1, LANES), np.float32)
    for h in range(FOX_HEADS):
        src = SUBLANES * h
        base = _fox_bias_base(h)
        for j in range(N_PIECES):
            selk[j * LANES + src, base + j] = -1.0
            selq[j * LANES + src, base + N_PIECES + j] = 1.0
            onek[0, base + N_PIECES + j] = 1.0
            oneq[0, base + j] = 1.0
    for h in range(ML_HEADS):
        base = SUBLANES * h
        for j in range(N_PIECES):
            selm[j * LANES + GATE_I + h, base + j] = 1.0
            onem[0, base + N_PIECES + j] = 1.0
    return (jnp.asarray(selk, BF16), jnp.asarray(selq, BF16), jnp.asarray(selm, BF16),
            jnp.asarray(onek), jnp.asarray(oneq), jnp.asarray(onem))


def _proj_kernel(x_ref, w_ref, cw_ref, cb_ref,
                 kc_ref, qct_ref, mvt_ref, ogt_ref, zgt_ref, ufx_ref, rqt_ref, rzg_ref, g_ref,
                 fqt_ref, fvt_ref, tail_sc):
    tm = PROJ_TM

    @pl.when(pl.program_id(1) == 0)
    def _():
        tail_sc[...] = jnp.zeros_like(tail_sc)

    xb = x_ref[...].astype(BF16)

    def mm(c0, n):
        return jnp.dot(xb, w_ref[:, c0:c0 + n], preferred_element_type=F32)

    u = mm(COL_QK, 2 * ML_PW)
    tail = tail_sc[...]
    w = cw_ref[...]
    row8 = lax.broadcasted_iota(jnp.int32, (SUBLANES, 2 * ML_PW), 0)
    y = u * w[CONV_WIDTH - 1:CONV_WIDTH, :] + cb_ref[...]
    for shift in range(1, CONV_WIDTH):
        r = pltpu.roll(u, shift, axis=0)
        rt = pltpu.roll(tail, shift, axis=0)
        first = jnp.where(row8 < shift, rt, r[:SUBLANES])
        xs = jnp.concatenate([first, r[SUBLANES:]], axis=0)
        y = y + xs * w[CONV_WIDTH - 1 - shift:CONV_WIDTH - shift, :]
    tail_sc[...] = u[tm - SUBLANES:, :]
    a = _silu(y)
    qct_ref[...] = a[:, :ML_PW].T.astype(BF16)
    kc_ref[...] = (a[:, ML_PW:] * (ML_HEAD_DIM ** -0.5)).astype(BF16)

    t = mm(COL_MV, 2 * ML_WIDTH)
    mvt_ref[...] = _with_ones_rows(t[:, :ML_WIDTH].T.astype(BF16), ML_HEAD_DIM, ML_HEADS)
    ogt_ref[...] = _sigmoid(t[:, ML_WIDTH:]).T.astype(BF16)

    t = _silu(mm(COL_MZ, ML_WIDTH + FOX_WIDTH))
    zgt_ref[...] = t[:, :ML_WIDTH].T.astype(BF16)
    ufx_ref[:, FOX_WIDTH:] = t[:, ML_WIDTH:].astype(BF16)

    t = mm(COL_FK, 2 * FOX_WIDTH)
    ufx_ref[:, :FOX_WIDTH] = t[:, :FOX_WIDTH].astype(BF16)
    fqt_ref[...] = t[:, FOX_WIDTH:].T.astype(BF16)

    t = mm(COL_FV, FOX_WIDTH + G_COLS)
    fvt_ref[...] = _with_ones_rows(t[:, :FOX_WIDTH].T.astype(BF16), FOX_HEAD_DIM, FOX_HEADS)
    g_ref[...] = t[:, FOX_WIDTH:]

    t = mm(COL_RQ, 2 * MEM_WIDTH)
    rqt_ref[...] = t[:, :MEM_WIDTH].T.astype(BF16)
    rzg_ref[...] = _silu(t[:, MEM_WIDTH:]).astype(BF16)


def _project(x, w_re, conv_w, conv_b, layer):
    b, s, _ = x.shape
    tm = PROJ_TM

    def row(n):
        return pl.BlockSpec((None, tm, n), lambda i, j: (i, j, 0))

    def col(n):
        return pl.BlockSpec((None, n, tm), lambda i, j: (i, 0, j))

    def per_layer(shape):
        return pl.BlockSpec((None,) + shape, lambda i, j: (layer, 0, 0))

    def rows_shape(n):
        return jax.ShapeDtypeStruct((b, s, n), BF16)

    def cols_shape(n):
        return jax.ShapeDtypeStruct((b, n, s), BF16)

    return pl.pallas_call(
        _proj_kernel,
        grid=(b, s // tm),
        in_specs=[row(D_MODEL), per_layer((D_MODEL, W_COLS)),
                  per_layer((CONV_WIDTH, 2 * ML_PW)), per_layer((1, 2 * ML_PW))],
        out_specs=[row(ML_PW), col(ML_PW), col(ML_HEADS * ML_VROWS), col(ML_WIDTH), col(ML_WIDTH),
                   row(2 * FOX_WIDTH), col(MEM_WIDTH), row(MEM_WIDTH), row(G_COLS),
                   col(FOX_WIDTH), col(FOX_HEADS * FOX_VROWS)],
        out_shape=[rows_shape(ML_PW), cols_shape(ML_PW), cols_shape(ML_HEADS * ML_VROWS),
                   cols_shape(ML_WIDTH), cols_shape(ML_WIDTH),
                   rows_shape(2 * FOX_WIDTH), cols_shape(MEM_WIDTH), rows_shape(MEM_WIDTH),
                   jax.ShapeDtypeStruct((b, s, G_COLS), F32),
                   cols_shape(FOX_WIDTH), cols_shape(FOX_HEADS * FOX_VROWS)],
        scratch_shapes=[pltpu.VMEM((SUBLANES, 2 * ML_PW), F32)],
        compiler_params=pltpu.CompilerParams(
            dimension_semantics=("arbitrary", "arbitrary"), vmem_limit_bytes=56 * 1024 * 1024),
        name="in_proj",
    )(x, w_re, conv_w, conv_b)


def _gates_kernel(g_ref, bias_ref, selk_ref, selq_ref, selm_ref, onek_ref, oneq_ref, onem_ref,
                  kb_ref, qbt_ref, maug_ref, mrow_ref, carry_ref):
    @pl.when(pl.program_id(0) == 0)
    def _():
        carry_ref[...] = jnp.zeros_like(carry_ref)

    L = ML_CHUNK
    r = lax.broadcasted_iota(jnp.int32, (L, L), 0)
    c = lax.broadcasted_iota(jnp.int32, (L, L), 1)
    tri = (c <= r).astype(BF16)
    lane = lax.broadcasted_iota(jnp.int32, (L, G_COLS), 1)
    pos = lax.broadcasted_iota(jnp.int32, (SUBLANES, L), 1)

    chains = [(bi, ci) for bi in range(g_ref.shape[0]) for ci in range(GATE_CHUNKS)]

    def span(ci):
        return slice(ci * L, (ci + 1) * L)

    xs, parts = [], []
    for bi, ci in chains:
        x = g_ref[bi, span(ci), :] + bias_ref[...]
        log_sig = jnp.minimum(x, 0.0) - jnp.log1p(jnp.exp(-jnp.abs(x)))
        xs.append(x)
        parts.append(jnp.dot(tri, jnp.concatenate(_split3(log_sig), axis=1),
                             preferred_element_type=F32))

    kbs, qbs, maugs, rowss = [], [], [], []
    carry = None
    for (bi, ci), x, p in zip(chains, xs, parts):
        if ci == 0:
            carry = carry_ref[bi, 0:1, :]
        local = (p[:, :LANES] + p[:, LANES:2 * LANES]) + p[:, 2 * LANES:]
        c_run = local + carry
        carry = carry + local[L - 1:L, :]
        if ci == GATE_CHUNKS - 1:
            carry_ref[bi, 0:1, :] = carry
        pieces = jnp.concatenate(_split3(c_run * LOG2E), axis=1)
        kbs.append(jnp.dot(pieces, selk_ref[...], preferred_element_type=F32))
        qbs.append(jnp.dot(pieces, selq_ref[...], preferred_element_type=F32))
        rc = x - pltpu.roll(local, LANES - (GATE_F - GATE_I), axis=1)
        pieces = jnp.concatenate(_split3(rc), axis=1)
        maugs.append(jnp.dot(pieces, selm_ref[...], preferred_element_type=F32))
        rowss.append(jnp.where(lane < GATE_F, rc, local))

    for (bi, ci), kb, qb, maug, rows in zip(chains, kbs, qbs, maugs, rowss):
        kb_ref[bi, span(ci), :] = (kb + onek_ref[...]).astype(BF16)
        qbt_ref[bi, :, span(ci)] = (qb + oneq_ref[...]).T.astype(BF16)
        maug_ref[bi, span(ci), :] = (maug + onem_ref[...]).astype(BF16)
        rows = rows.T[GATE_I:GATE_I + 2 * SUBLANES, :]
        cm = rows[:SUBLANES]
        shift = 1
        while shift < L:
            cm = jnp.maximum(cm, jnp.where(pos >= shift, pltpu.roll(cm, shift, axis=1), -jnp.inf))
            shift *= 2
        mrow_ref[bi, :, span(ci)] = jnp.concatenate([rows, cm], axis=0)


def _gates(g, bias_row, sels):
    b, s, _ = g.shape
    L = GATE_CHUNKS * ML_CHUNK
    const = lambda shape: pl.BlockSpec(shape, lambda j: (0, 0))
    sel_shape = (N_PIECES * LANES, LANES)
    return pl.pallas_call(
        _gates_kernel,
        grid=(s // L,),
        in_specs=[pl.BlockSpec((b, L, G_COLS), lambda j: (0, j, 0)),
                  const((1, G_COLS)),
                  const(sel_shape), const(sel_shape), const(sel_shape),
                  const((1, LANES)), const((1, LANES)), const((1, LANES))],
        out_specs=[pl.BlockSpec((b, L, LANES), lambda j: (0, j, 0)),
                   pl.BlockSpec((b, LANES, L), lambda j: (0, 0, j)),
                   pl.BlockSpec((b, L, LANES), lambda j: (0, j, 0)),
                   pl.BlockSpec((b, 3 * SUBLANES, L), lambda j: (0, 0, j))],
        out_shape=[jax.ShapeDtypeStruct((b, s, LANES), BF16),
                   jax.ShapeDtypeStruct((b, LANES, s), BF16),
                   jax.ShapeDtypeStruct((b, s, LANES), BF16),
                   jax.ShapeDtypeStruct((b, 3 * SUBLANES, s), F32)],
        scratch_shapes=[pltpu.VMEM((b, SUBLANES, G_COLS), F32)],
        compiler_params=pltpu.CompilerParams(dimension_semantics=("arbitrary",)),
        name="gate_prefix",
    )(g, bias_row, *sels)


def _fox_kernel(qt_ref, qbt_ref, k_ref, kb_ref, vt_ref, zg_ref, o_ref,
                qa_sc, sa_sc, sb_sc, mxa_sc, mxb_sc, m_sc, acc_sc):
    qi = pl.program_id(1)
    tq, tk = FOX_TQ, FOX_TK
    hd = FOX_HEAD_DIM

    brow = lax.broadcasted_iota(jnp.int32, (LANES, tq), 0)
    qbt = qbt_ref[...]
    for h in range(FOX_HEADS):
        base = _fox_bias_base(h)
        own = (brow >= base) & (brow < base + SUBLANES)
        bias = jnp.where(own, qbt, jnp.zeros_like(qbt))
        qh = qt_ref[h * hd:(h + 1) * hd, :]
        if h % 2 == 0:
            qa_sc[h] = jnp.concatenate([qh, bias[hd:, :]], axis=0)
        else:
            qa_sc[h] = jnp.concatenate([bias[:hd, :], qh], axis=0)

    klane = lax.broadcasted_iota(jnp.int32, (tk, LANES), 1)

    m_sc[...] = jnp.full_like(m_sc, -jnp.inf)
    acc_sc[...] = jnp.zeros_like(acc_sc)

    def logits(j, h, dst, mxd, key_offset):
        k0 = pl.multiple_of(j * tk, tk)
        p = h // 2
        kpair = k_ref[pl.ds(k0, tk), p * LANES:(p + 1) * LANES]
        kbias = kb_ref[pl.ds(k0, tk), :]
        own = (klane < hd) if h % 2 == 0 else (klane >= hd)
        kk = jnp.where(own, kpair, kbias)
        st = jnp.dot(kk, qa_sc[h], preferred_element_type=F32)
        if key_offset is not None:
            kr = lax.broadcasted_iota(jnp.int32, (tk, tq), 0)
            qc = lax.broadcasted_iota(jnp.int32, (tk, tq), 1)
            st = jnp.where(kr + key_offset <= qc, st, -jnp.inf)
        dst[h] = st
        mxd[h] = jnp.max(st, axis=0, keepdims=True)

    def softmax_pv(j, h, src, mxs):
        k0 = pl.multiple_of(j * tk, tk)
        m_old = m_sc[h]
        m_new = jnp.maximum(m_old, mxs[h])
        alpha = jnp.exp2(m_old - m_new)
        pt = jnp.exp2(src[h] - m_new).astype(BF16)
        vt = vt_ref[h * FOX_VROWS:(h + 1) * FOX_VROWS, pl.ds(k0, tk)]
        acc_sc[h] = alpha * acc_sc[h] + jnp.dot(vt, pt, preferred_element_type=F32)
        m_sc[h] = m_new

    buf_a = (sa_sc, mxa_sc)
    buf_b = (sb_sc, mxb_sc)

    def stage(j_qk, dst, key_offset, j_sm, src):
        if j_qk is not None:
            logits(j_qk, 0, *dst, key_offset)
        for h in range(FOX_HEADS):
            if j_qk is not None and h + 1 < FOX_HEADS:
                logits(j_qk, h + 1, *dst, key_offset)
            if j_sm is not None:
                softmax_pv(j_sm, h, *src)

    stage(2 * qi, buf_a, 0, None, None)
    stage(2 * qi + 1, buf_b, tk, 2 * qi, buf_a)
    stage(0, buf_a, None, 2 * qi + 1, buf_b)

    def body(i, carry):
        stage(2 * i + 1, buf_b, None, 2 * i, buf_a)
        stage(2 * i + 2, buf_a, None, 2 * i + 1, buf_b)
        return carry

    lax.fori_loop(0, qi, body, 0)

    out_t = jnp.concatenate(
        [acc_sc[h, :hd, :] / acc_sc[h, hd:hd + 1, :] for h in range(FOX_HEADS)], axis=0)
    o_ref[...] = (out_t.T * zg_ref[...].astype(F32)).astype(BF16)


def _fox(qt, qbt, ufx, kb, vt):
    b, s, _ = ufx.shape
    tq = FOX_TQ
    return pl.pallas_call(
        _fox_kernel,
        grid=(b, s // tq),
        in_specs=[
            pl.BlockSpec((None, FOX_WIDTH, tq), lambda i, j: (i, 0, j)),
            pl.BlockSpec((None, LANES, tq), lambda i, j: (i, 0, j)),
            pl.BlockSpec((None, s, FOX_WIDTH), lambda i, j: (i, 0, 0)),
            pl.BlockSpec((None, s, LANES), lambda i, j: (i, 0, 0)),
            pl.BlockSpec((None, FOX_HEADS * FOX_VROWS, s), lambda i, j: (i, 0, 0)),
            pl.BlockSpec((None, tq, FOX_WIDTH), lambda i, j: (i, j, 1)),
        ],
        out_specs=pl.BlockSpec((None, tq, FOX_WIDTH), lambda i, j: (i, j, 0)),
        out_shape=jax.ShapeDtypeStruct((b, s, FOX_WIDTH), BF16),
        scratch_shapes=[pltpu.VMEM((FOX_HEADS, LANES, tq), BF16),
                        pltpu.VMEM((FOX_HEADS, FOX_TK, tq), F32),
                        pltpu.VMEM((FOX_HEADS, FOX_TK, tq), F32),
                        pltpu.VMEM((FOX_HEADS, 1, tq), F32),
                        pltpu.VMEM((FOX_HEADS, 1, tq), F32),
                        pltpu.VMEM((FOX_HEADS, 1, tq), F32),
                        pltpu.VMEM((FOX_HEADS, FOX_VROWS, tq), F32)],
        compiler_params=pltpu.CompilerParams(
            dimension_semantics=("arbitrary", "arbitrary"), vmem_limit_bytes=48 * 1024 * 1024),
        name="fox_attn",
    )(qt, qbt, ufx, kb, vt, ufx)


def _mlstm_kernel(kc_ref, qct_ref, mvt_ref, ogt_ref, zgt_ref, maug_ref, mrow_ref, ng_ref,
                  y_ref, c_sc, m_sc):
    L = ML_CHUNK
    dv = ML_HEAD_DIM

    @pl.when(pl.program_id(1) == 0)
    def _():
        c_sc[...] = jnp.zeros_like(c_sc)
        m_sc[...] = jnp.zeros_like(m_sc)

    kr = lax.broadcasted_iota(jnp.int32, (L, L), 0)
    qc = lax.broadcasted_iota(jnp.int32, (L, L), 1)
    visible = kr <= qc
    brow = lax.broadcasted_iota(jnp.int32, (ML_BIAS_ROWS, L), 0)
    bzero = jnp.zeros((LANES - ML_BIAS_ROWS, L), BF16)
    chains = [(bi, h) for bi in range(ML_NB) for h in range(ML_HEADS)]

    work = []
    for bi, h in chains:
        sc = bi * ML_HEADS + h
        r = mrow_ref[bi, h:h + 1, :]
        b = mrow_ref[bi, SUBLANES + h:SUBLANES + h + 1, :]
        cm = mrow_ref[bi, 2 * SUBLANES + h:2 * SUBLANES + h + 1, :]
        m_prev = m_sc[sc, 0:1, 0:1]
        g = jnp.maximum(m_prev, cm)
        g_last = g[:, L - 1:L]
        g_hi, g_mid, g_lo = (t.astype(F32) for t in _split3(g))
        rel = brow - SUBLANES * h
        blk = jnp.where(rel == N_PIECES, -g_hi,
                        jnp.where(rel == N_PIECES + 1, -g_mid,
                                  jnp.where(rel == N_PIECES + 2, -g_lo,
                                            jnp.where((rel >= 0) & (rel < N_PIECES), 1.0, 0.0))))
        bmat = jnp.concatenate([blk.astype(BF16), bzero], axis=0)
        k_h = kc_ref[bi, :, h * ML_PAD:(h + 1) * ML_PAD]
        q_h = qct_ref[bi, h * ML_PAD:(h + 1) * ML_PAD, :]
        ct = c_sc[sc]
        z = jnp.dot(maug_ref[bi], bmat, preferred_element_type=F32)
        s_raw = jnp.dot(k_h, q_h, preferred_element_type=F32)
        inter = jnp.dot(ct.astype(BF16), q_h, preferred_element_type=F32)
        work.append((r, b, g, g_last, m_prev, k_h, ct, z, s_raw, inter))

    outs = []
    for (bi, h), (r, b, g, g_last, m_prev, k_h, ct, z, s_raw, inter) in zip(chains, work):
        sc = bi * ML_HEADS + h
        d = jnp.exp(jnp.where(visible, z, -jnp.inf))
        p = (s_raw * d).astype(BF16)
        vt = mvt_ref[bi, h * ML_VROWS:(h + 1) * ML_VROWS, :]
        num = jnp.dot(vt, p, preferred_element_type=F32) + jnp.exp(m_prev - g) * inter
        den = num[dv:dv + 1, :]
        hv = num[:dv, :] / jnp.maximum(jnp.abs(den), jnp.exp(-(b + g)))

        wv = (vt.astype(F32) * jnp.exp(r - g_last)).astype(BF16)
        c_sc[sc] = jnp.exp(m_prev - g_last) * ct + jnp.dot(wv, k_h, preferred_element_type=F32)
        m_sc[sc] = jnp.broadcast_to(b[:, L - 1:L] + g_last, (SUBLANES, LANES))

        hv = hv * ogt_ref[bi, h * dv:(h + 1) * dv, :].astype(F32)
        mu = jnp.mean(hv, axis=0, keepdims=True)
        dd = hv - mu
        var = jnp.mean(dd * dd, axis=0, keepdims=True)
        hn = dd * lax.rsqrt(var + LN_EPS) * ng_ref[h * dv:(h + 1) * dv, :]
        outs.append(hn * zgt_ref[bi, h * dv:(h + 1) * dv, :].astype(F32))
        if h == ML_HEADS - 1:
            y_ref[bi] = jnp.concatenate(outs, axis=0).T.astype(BF16)
            outs = []


def _mlstm(kc, qct, mvt, ogt, zgt, maug, mrow, norm_g_b):
    b, s, _ = kc.shape
    L = ML_CHUNK

    def col(n):
        return pl.BlockSpec((ML_NB, n, L), lambda i, j: (i, 0, j))

    def row(n):
        return pl.BlockSpec((ML_NB, L, n), lambda i, j: (i, j, 0))

    return pl.pallas_call(
        _mlstm_kernel,
        grid=(b // ML_NB, s // L),
        in_specs=[row(ML_PW), col(ML_PW), col(ML_HEADS * ML_VROWS), col(ML_WIDTH), col(ML_WIDTH),
                  row(LANES), col(3 * SUBLANES),
                  pl.BlockSpec((ML_WIDTH, L), lambda i, j: (0, 0))],
        out_specs=row(ML_WIDTH),
        out_shape=jax.ShapeDtypeStruct((b, s, ML_WIDTH), BF16),
        scratch_shapes=[pltpu.VMEM((ML_NB * ML_HEADS, ML_VROWS, LANES), F32),
                        pltpu.VMEM((ML_NB * ML_HEADS, SUBLANES, LANES), F32)],
        compiler_params=pltpu.CompilerParams(
            dimension_semantics=("arbitrary", "arbitrary"), vmem_limit_bytes=48 * 1024 * 1024),
        name="mlstm",
    )(kc, qct, mvt, ogt, zgt, maug, mrow, norm_g_b)


def _memkv_kernel(m_ref, w_ref, k4_ref, vt_ref):
    kv = jnp.dot(m_ref[...].astype(BF16), w_ref[...].astype(BF16), preferred_element_type=F32)
    k = kv[:, :MEM_WIDTH] * (MEM_HEAD_DIM ** -0.5)
    lane = lax.broadcasted_iota(jnp.int32, k.shape, 1)
    for h in range(MEM_HEADS):
        own = (lane >= MEM_HEAD_DIM * h) & (lane < MEM_HEAD_DIM * (h + 1))
        k4_ref[h] = jnp.where(own, k, 0.0).astype(BF16)
    vt_ref[...] = _with_ones_rows(kv[:, MEM_WIDTH:].T.astype(BF16), MEM_HEAD_DIM, MEM_HEADS)


def _memkv(mem, w_mem_kv):
    b, ml, _ = mem.shape
    return pl.pallas_call(
        _memkv_kernel,
        grid=(DEPTH, b),
        in_specs=[pl.BlockSpec((None, ml, D_MODEL), lambda l, i: (i, 0, 0)),
                  pl.BlockSpec((None, D_MODEL, 2 * MEM_WIDTH), lambda l, i: (l, 0, 0))],
        out_specs=[pl.BlockSpec((None, None, MEM_HEADS, ml, MEM_WIDTH), lambda l, i: (l, i, 0, 0, 0)),
                   pl.BlockSpec((None, None, MEM_HEADS * MEM_VROWS, ml), lambda l, i: (l, i, 0, 0))],
        out_shape=[jax.ShapeDtypeStruct((DEPTH, b, MEM_HEADS, ml, MEM_WIDTH), BF16),
                   jax.ShapeDtypeStruct((DEPTH, b, MEM_HEADS * MEM_VROWS, ml), BF16)],
        compiler_params=pltpu.CompilerParams(dimension_semantics=("arbitrary", "arbitrary")),
        name="mem_kv",
    )(mem, w_mem_kv)


def _out_kernel(x_ref, yf_ref, ym_ref, rqt_ref, rzg_ref, k4_ref, vt_ref, wo_ref, g_ref, b_ref,
                o_ref, ob_ref):
    hd = MEM_HEAD_DIM
    o_mem = FOX_WIDTH + ML_WIDTH
    n_sub = OUT_TM // OUT_SUB
    ahead = 2

    def rows(q):
        return slice(q * OUT_SUB, (q + 1) * OUT_SUB)

    def main_dot(q):
        y_main = jnp.concatenate([yf_ref[rows(q), :], ym_ref[rows(q), :]], axis=1)
        return jnp.dot(y_main, wo_ref[0:o_mem, :], preferred_element_type=F32)

    rqt = rqt_ref[...]
    sts = [jnp.dot(k4_ref[h], rqt, preferred_element_type=F32) for h in range(MEM_HEADS)]
    mains = [main_dot(q) for q in range(ahead)]

    outs = []
    for h in range(MEM_HEADS):
        st = sts[h]
        p = jnp.exp(st - jnp.max(st, axis=0, keepdims=True)).astype(BF16)
        acc = jnp.dot(vt_ref[h * MEM_VROWS:(h + 1) * MEM_VROWS, :], p,
                      preferred_element_type=F32)
        outs.append(acc[:hd, :] / acc[hd:hd + 1, :])
    att = jnp.concatenate(outs, axis=0).T
    y_mem = (att * rzg_ref[...].astype(F32)).astype(BF16)

    for q in range(n_sub):
        if q + ahead < n_sub:
            mains.append(main_dot(q + ahead))
        y = mains[q] + jnp.dot(y_mem[rows(q), :], wo_ref[o_mem:, :], preferred_element_type=F32)
        r = DEEPNORM_ALPHA * x_ref[rows(q), :] + y
        mu = jnp.mean(r, axis=1, keepdims=True)
        d = r - mu
        var = jnp.mean(d * d, axis=1, keepdims=True)
        out = d * lax.rsqrt(var + LN_EPS) * g_ref[...] + b_ref[...]
        o_ref[rows(q), :] = out
        ob_ref[rows(q), :] = out.astype(BF16)


def _out_proj(x, y_fox, y_ml, rqt, rzg, mem_k4, mem_vt, layer, w_out, ln_g, ln_b):
    b, s, _ = x.shape
    tm = OUT_TM
    ml = mem_k4.shape[3]
    return pl.pallas_call(
        _out_kernel,
        grid=(b, s // tm),
        in_specs=[
            pl.BlockSpec((None, tm, D_MODEL), lambda i, j: (i, j, 0)),
            pl.BlockSpec((None, tm, FOX_WIDTH), lambda i, j: (i, j, 0)),
            pl.BlockSpec((None, tm, ML_WIDTH), lambda i, j: (i, j, 0)),
            pl.BlockSpec((None, MEM_WIDTH, tm), lambda i, j: (i, 0, j)),
            pl.BlockSpec((None, tm, MEM_WIDTH), lambda i, j: (i, j, 0)),
            pl.BlockSpec((None, None, MEM_HEADS, ml, MEM_WIDTH), lambda i, j: (layer, i, 0, 0, 0)),
            pl.BlockSpec((None, None, MEM_HEADS * MEM_VROWS, ml), lambda i, j: (layer, i, 0, 0)),
            pl.BlockSpec((None, D_MODEL, D_MODEL), lambda i, j: (layer, 0, 0)),
            pl.BlockSpec((None, 1, D_MODEL), lambda i, j: (layer, 0, 0)),
            pl.BlockSpec((None, 1, D_MODEL), lambda i, j: (layer, 0, 0)),
        ],
        out_specs=[pl.BlockSpec((None, tm, D_MODEL), lambda i, j: (i, j, 0)),
                   pl.BlockSpec((None, tm, D_MODEL), lambda i, j: (i, j, 0))],
        out_shape=[jax.ShapeDtypeStruct((b, s, D_MODEL), F32),
                   jax.ShapeDtypeStruct((b, s, D_MODEL), BF16)],
        compiler_params=pltpu.CompilerParams(
            dimension_semantics=("arbitrary", "arbitrary"), vmem_limit_bytes=48 * 1024 * 1024),
        name="out_proj_ln",
    )(x, y_fox, y_ml, rqt, rzg, mem_k4, mem_vt, w_out, ln_g, ln_b)


def kernel(x, mem, w_in, fox_f_bias, mlstm_conv_w, mlstm_conv_b, mlstm_i_bias, mlstm_f_bias,
           mlstm_norm_g, w_mem_kv, w_out, ln_g, ln_b):
    w_re = _relayout_w_in(w_in)
    w_out_b = w_out.astype(BF16)
    bias_rows = _spread_gates(fox_f_bias, mlstm_i_bias, mlstm_f_bias)[:, None, :]
    conv_w = jnp.concatenate([_pad_heads_last(mlstm_conv_w[..., :ML_WIDTH]),
                              _pad_heads_last(mlstm_conv_w[..., ML_WIDTH:])], axis=-1)
    conv_b = jnp.concatenate([_pad_heads_last(mlstm_conv_b[..., :ML_WIDTH]),
                              _pad_heads_last(mlstm_conv_b[..., ML_WIDTH:])], axis=-1)[:, None, :]
    norm_g_b = jnp.broadcast_to(mlstm_norm_g[:, :, None], (DEPTH, ML_WIDTH, ML_CHUNK))
    ln_g3 = ln_g[:, None, :]
    ln_b3 = ln_b[:, None, :]
    sels = _bias_selectors()

    mem_k4, mem_vt = _memkv(mem, w_mem_kv)
    x_mxu = x
    for l in range(DEPTH):
        kc, qct, mvt, ogt, zgt, ufx, rqt, rzg, g, fqt, fvt = _project(x_mxu, w_re, conv_w, conv_b, l)
        kb, qbt, maug, mrow = _gates(g, bias_rows[l], sels)
        y_fox = _fox(fqt, qbt, ufx, kb, fvt)
        y_ml = _mlstm(kc, qct, mvt, ogt, zgt, maug, mrow, norm_g_b[l])
        x, x_mxu = _out_proj(x, y_fox, y_ml, rqt, rzg, mem_k4, mem_vt, l, w_out_b, ln_g3, ln_b3)
    return x
```

```python
import math

import numpy as np
import jax
import jax.numpy as jnp
from jax import lax
from jax.experimental import pallas as pl
from jax.experimental.pallas import tpu as pltpu

F32 = jnp.float32
BF16 = jnp.bfloat16

D_MODEL = 1024
DEPTH = 4
FOX_HEADS = 6
FOX_HEAD_DIM = 64
FOX_WIDTH = FOX_HEADS * FOX_HEAD_DIM
ML_HEADS = 4
ML_HEAD_DIM = 96
ML_WIDTH = ML_HEADS * ML_HEAD_DIM
MEM_HEADS = 4
MEM_HEAD_DIM = 64
MEM_WIDTH = MEM_HEADS * MEM_HEAD_DIM
CONV_WIDTH = 4
LN_EPS = 1e-5
DEEPNORM_ALPHA = (2.0 * DEPTH) ** 0.25
LOG2E = math.log2(math.e)

LANES = 128
SUBLANES = 8
BF16_ROWS = 16
ML_PAD = LANES
ML_PW = ML_HEADS * ML_PAD
ML_VROWS = ML_HEAD_DIM + BF16_ROWS
FOX_VROWS = FOX_HEAD_DIM + BF16_ROWS
MEM_VROWS = MEM_HEAD_DIM + BF16_ROWS

COL_QK = 0
COL_MV = COL_QK + 2 * ML_PW
COL_MZ = COL_MV + 2 * ML_WIDTH
COL_FK = COL_MZ + ML_WIDTH + FOX_WIDTH
COL_FV = COL_FK + 2 * FOX_WIDTH
COL_RQ = COL_FV + FOX_WIDTH + LANES
W_COLS = COL_RQ + 2 * MEM_WIDTH
G_COLS = LANES
GATE_I = 48
GATE_F = 56
N_PIECES = 3
ML_BIAS_ROWS = 2 * BF16_ROWS

ML_CHUNK = 256
ML_NB = 4
GATE_CHUNKS = 2
FOX_TQ = 512
FOX_TK = 256
PROJ_TM = 512
OUT_TM = 1024
OUT_SUB = 256

_SRC = {}
_off = 0
for _name, _w in (("fq", FOX_WIDTH), ("fk", FOX_WIDTH), ("fv", FOX_WIDTH), ("ff", FOX_HEADS),
                  ("fz", FOX_WIDTH), ("mq", ML_WIDTH), ("mk", ML_WIDTH), ("mv", ML_WIDTH),
                  ("mi", ML_HEADS), ("mf", ML_HEADS), ("mo", ML_WIDTH), ("mz", ML_WIDTH),
                  ("rq", MEM_WIDTH), ("rz", MEM_WIDTH)):
    _SRC[_name] = (_off, _w)
    _off += _w


def _sigmoid(x):
    return 1.0 / (1.0 + jnp.exp2(x * (-LOG2E)))


def _silu(x):
    return x * _sigmoid(x)


def _split3(x):
    hi = x.astype(BF16)
    r1 = x - hi.astype(F32)
    mid = r1.astype(BF16)
    lo = (r1 - mid.astype(F32)).astype(BF16)
    return hi, mid, lo


def _with_ones_rows(t, head_dim, heads):
    ones = jnp.ones((BF16_ROWS, t.shape[1]), t.dtype)
    rows = []
    for h in range(heads):
        rows += [t[h * head_dim:(h + 1) * head_dim, :], ones]
    return jnp.concatenate(rows, axis=0)


def _zeros_like_cols(t, n):
    return jnp.zeros(t.shape[:-1] + (n,), t.dtype)


def _pad_heads_last(t):
    parts = []
    for h in range(ML_HEADS):
        parts.append(t[..., h * ML_HEAD_DIM:(h + 1) * ML_HEAD_DIM])
        parts.append(_zeros_like_cols(t, ML_PAD - ML_HEAD_DIM))
    return jnp.concatenate(parts, axis=-1)


def _spread_gates(ff, mi, mf):
    parts = []
    for h in range(FOX_HEADS):
        parts += [ff[..., h:h + 1], _zeros_like_cols(ff, SUBLANES - 1)]
    parts += [mi, _zeros_like_cols(mi, GATE_F - GATE_I - ML_HEADS),
              mf, _zeros_like_cols(mf, G_COLS - GATE_F - ML_HEADS)]
    return jnp.concatenate(parts, axis=-1)


def _w_in_segments():
    segs = []

    def put(name, dst, scale=1.0):
        o, n = _SRC[name]
        segs.append((o, dst, n, scale))
        return dst + n

    def put_padded_heads(name, dst):
        o, _ = _SRC[name]
        for h in range(ML_HEADS):
            segs.append((o + h * ML_HEAD_DIM, dst + h * ML_PAD, ML_HEAD_DIM, 1.0))
        return dst + ML_PW

    c = put_padded_heads("mq", COL_QK)
    c = put_padded_heads("mk", c)
    for name in ("mv", "mo", "mz", "fz", "fk"):
        c = put(name, c)
    c = put("fq", c, FOX_HEAD_DIM ** -0.5 * LOG2E)
    c = put("fv", c)
    o_ff, o_mi, o_mf = _SRC["ff"][0], _SRC["mi"][0], _SRC["mf"][0]
    for h in range(FOX_HEADS):
        segs.append((o_ff + h, c + SUBLANES * h, 1, 1.0))
    segs.append((o_mi, c + GATE_I, ML_HEADS, 1.0))
    segs.append((o_mf, c + GATE_F, ML_HEADS, 1.0))
    c = put("rq", c + G_COLS)
    c = put("rz", c)
    assert c == W_COLS
    return segs


W_PREP_ROWS = 256


def _w_prep_kernel(w_ref, o_ref):
    o_ref[...] = jnp.zeros_like(o_ref)
    for src, dst, n, scale in _w_in_segments():
        seg = w_ref[:, src:src + n]
        if scale != 1.0:
            seg = seg * scale
        o_ref[:, dst:dst + n] = seg.astype(BF16)


def _relayout_w_in(w):
    depth, rows, cols = w.shape
    tr = W_PREP_ROWS
    return pl.pallas_call(
        _w_prep_kernel,
        grid=(depth, rows // tr),
        in_specs=[pl.BlockSpec((None, tr, cols), lambda l, i: (l, i, 0))],
        out_specs=pl.BlockSpec((None, tr, W_COLS), lambda l, i: (l, i, 0)),
        out_shape=jax.ShapeDtypeStruct((depth, rows, W_COLS), BF16),
        compiler_params=pltpu.CompilerParams(dimension_semantics=("arbitrary", "arbitrary")),
        name="w_in_prep",
    )(w)


def _fox_bias_base(h):
    return SUBLANES * (h // 2) + (FOX_HEAD_DIM if h % 2 == 0 else 0)


def _bias_selectors():
    n = N_PIECES * LANES
    selk = np.zeros((n, LANES), np.float32)
    selq = np.zeros((n, LANES), np.float32)
    selm = np.zeros((n, LANES), np.float32)
    onek = np.zeros((1, LANES), np.float32)
    oneq = np.zeros((1, LANES), np.float32)
    onem = np.zeros((1, LANES), np.float32)
    for h in range(FOX_HEADS):
        src = SUBLANES * h
        base = _fox_bias_base(h)
        for j in range(N_PIECES):
            selk[j * LANES + src, base + j] = -1.0
            selq[j * LANES + src, base + N_PIECES + j] = 1.0
            onek[0, base + N_PIECES + j] = 1.0
            oneq[0, base + j] = 1.0
    for h in range(ML_HEADS):
        base = SUBLANES * h
        for j in range(N_PIECES):
            selm[j * LANES + GATE_I + h, base + j] = 1.0
            onem[0, base + N_PIECES + j] = 1.0
    return (jnp.asarray(selk, BF16), jnp.asarray(selq, BF16), jnp.asarray(selm, BF16),
            jnp.asarray(onek), jnp.asarray(oneq), jnp.asarray(onem))


def _proj_kernel(x_ref, w_ref, cw_ref, cb_ref,
                 kc_ref, qct_ref, mvt_ref, ogt_ref, zgt_ref, ufx_ref, rqt_ref, rzg_ref, g_ref,
                 fqt_ref, fvt_ref, tail_sc):
    tm = PROJ_TM

    @pl.when(pl.program_id(1) == 0)
    def _():
        tail_sc[...] = jnp.zeros_like(tail_sc)

    xb = x_ref[...].astype(BF16)

    def mm(c0, n):
        return jnp.dot(xb, w_ref[:, c0:c0 + n], preferred_element_type=F32)

    u = mm(COL_QK, 2 * ML_PW)
    tail = tail_sc[...]
    w = cw_ref[...]
    row8 = lax.broadcasted_iota(jnp.int32, (SUBLANES, 2 * ML_PW), 0)
    y = u * w[CONV_WIDTH - 1:CONV_WIDTH, :] + cb_ref[...]
    for shift in range(1, CONV_WIDTH):
        r = pltpu.roll(u, shift, axis=0)
        rt = pltpu.roll(tail, shift, axis=0)
        first = jnp.where(row8 < shift, rt, r[:SUBLANES])
        xs = jnp.concatenate([first, r[SUBLANES:]], axis=0)
        y = y + xs * w[CONV_WIDTH - 1 - shift:CONV_WIDTH - shift, :]
    tail_sc[...] = u[tm - SUBLANES:, :]
    a = _silu(y)
    qct_ref[...] = a[:, :ML_PW].T.astype(BF16)
    kc_ref[...] = (a[:, ML_PW:] * (ML_HEAD_DIM ** -0.5)).astype(BF16)

    t = mm(COL_MV, 2 * ML_WIDTH)
    mvt_ref[...] = _with_ones_rows(t[:, :ML_WIDTH].T.astype(BF16), ML_HEAD_DIM, ML_HEADS)
    ogt_ref[...] = _sigmoid(t[:, ML_WIDTH:]).T.astype(BF16)

    t = _silu(mm(COL_MZ, ML_WIDTH + FOX_WIDTH))
    zgt_ref[...] = t[:, :ML_WIDTH].T.astype(BF16)
    ufx_ref[:, FOX_WIDTH:] = t[:, ML_WIDTH:].astype(BF16)

    t = mm(COL_FK, 2 * FOX_WIDTH)
    ufx_ref[:, :FOX_WIDTH] = t[:, :FOX_WIDTH].astype(BF16)
    fqt_ref[...] = t[:, FOX_WIDTH:].T.astype(BF16)

    t = mm(COL_FV, FOX_WIDTH + G_COLS)
    fvt_ref[...] = _with_ones_rows(t[:, :FOX_WIDTH].T.astype(BF16), FOX_HEAD_DIM, FOX_HEADS)
    g_ref[...] = t[:, FOX_WIDTH:]

    t = mm(COL_RQ, 2 * MEM_WIDTH)
    rqt_ref[...] = t[:, :MEM_WIDTH].T.astype(BF16)
    rzg_ref[...] = _silu(t[:, MEM_WIDTH:]).astype(BF16)


def _project(x, w_re, conv_w, conv_b, layer):
    b, s, _ = x.shape
    tm = PROJ_TM

    def row(n):
        return pl.BlockSpec((None, tm, n), lambda i, j: (i, j, 0))

    def col(n):
        return pl.BlockSpec((None, n, tm), lambda i, j: (i, 0, j))

    def per_layer(shape):
        return pl.BlockSpec((None,) + shape, lambda i, j: (layer, 0, 0))

    def rows_shape(n):
        return jax.ShapeDtypeStruct((b, s, n), BF16)

    def cols_shape(n):
        return jax.ShapeDtypeStruct((b, n, s), BF16)

    return pl.pallas_call(
        _proj_kernel,
        grid=(b, s // tm),
        in_specs=[row(D_MODEL), per_layer((D_MODEL, W_COLS)),
                  per_layer((CONV_WIDTH, 2 * ML_PW)), per_layer((1, 2 * ML_PW))],
        out_specs=[row(ML_PW), col(ML_PW), col(ML_HEADS * ML_VROWS), col(ML_WIDTH), col(ML_WIDTH),
                   row(2 * FOX_WIDTH), col(MEM_WIDTH), row(MEM_WIDTH), row(G_COLS),
                   col(FOX_WIDTH), col(FOX_HEADS * FOX_VROWS)],
        out_shape=[rows_shape(ML_PW), cols_shape(ML_PW), cols_shape(ML_HEADS * ML_VROWS),
                   cols_shape(ML_WIDTH), cols_shape(ML_WIDTH),
                   rows_shape(2 * FOX_WIDTH), cols_shape(MEM_WIDTH), rows_shape(MEM_WIDTH),
                   jax.ShapeDtypeStruct((b, s, G_COLS), F32),
                   cols_shape(FOX_WIDTH), cols_shape(FOX_HEADS * FOX_VROWS)],
        scratch_shapes=[pltpu.VMEM((SUBLANES, 2 * ML_PW), F32)],
        compiler_params=pltpu.CompilerParams(
            dimension_semantics=("arbitrary", "arbitrary"), vmem_limit_bytes=56 * 1024 * 1024),
        name="in_proj",
    )(x, w_re, conv_w, conv_b)


def _gates_kernel(g_ref, bias_ref, selk_ref, selq_ref, selm_ref, onek_ref, oneq_ref, onem_ref,
                  kb_ref, qbt_ref, maug_ref, mrow_ref, carry_ref):
    @pl.when(pl.program_id(0) == 0)
    def _():
        carry_ref[...] = jnp.zeros_like(carry_ref)

    L = ML_CHUNK
    r = lax.broadcasted_iota(jnp.int32, (L, L), 0)
    c = lax.broadcasted_iota(jnp.int32, (L, L), 1)
    tri = (c <= r).astype(BF16)
    lane = lax.broadcasted_iota(jnp.int32, (L, G_COLS), 1)
    pos = lax.broadcasted_iota(jnp.int32, (SUBLANES, L), 1)

    chains = [(bi, ci) for bi in range(g_ref.shape[0]) for ci in range(GATE_CHUNKS)]

    def span(ci):
        return slice(ci * L, (ci + 1) * L)

    xs, parts = [], []
    for bi, ci in chains:
        x = g_ref[bi, span(ci), :] + bias_ref[...]
        log_sig = jnp.minimum(x, 0.0) - jnp.log1p(jnp.exp(-jnp.abs(x)))
        xs.append(x)
        parts.append(jnp.dot(tri, jnp.concatenate(_split3(log_sig), axis=1),
                             preferred_element_type=F32))

    kbs, qbs, maugs, rowss = [], [], [], []
    carry = None
    for (bi, ci), x, p in zip(chains, xs, parts):
        if ci == 0:
            carry = carry_ref[bi, 0:1, :]
        local = (p[:, :LANES] + p[:, LANES:2 * LANES]) + p[:, 2 * LANES:]
        c_run = local + carry
        carry = carry + local[L - 1:L, :]
        if ci == GATE_CHUNKS - 1:
            carry_ref[bi, 0:1, :] = carry
        pieces = jnp.concatenate(_split3(c_run * LOG2E), axis=1)
        kbs.append(jnp.dot(pieces, selk_ref[...], preferred_element_type=F32))
        qbs.append(jnp.dot(pieces, selq_ref[...], preferred_element_type=F32))
        rc = x - pltpu.roll(local, LANES - (GATE_F - GATE_I), axis=1)
        pieces = jnp.concatenate(_split3(rc), axis=1)
        maugs.append(jnp.dot(pieces, selm_ref[...], preferred_element_type=F32))
        rowss.append(jnp.where(lane < GATE_F, rc, local))

    for (bi, ci), kb, qb, maug, rows in zip(chains, kbs, qbs, maugs, rowss):
        kb_ref[bi, span(ci), :] = (kb + onek_ref[...]).astype(BF16)
        qbt_ref[bi, :, span(ci)] = (qb + oneq_ref[...]).T.astype(BF16)
        maug_ref[bi, span(ci), :] = (maug + onem_ref[...]).astype(BF16)
        rows = rows.T[GATE_I:GATE_I + 2 * SUBLANES, :]
        cm = rows[:SUBLANES]
        shift = 1
        while shift < L:
            cm = jnp.maximum(cm, jnp.where(pos >= shift, pltpu.roll(cm, shift, axis=1), -jnp.inf))
            shift *= 2
        mrow_ref[bi, :, span(ci)] = jnp.concatenate([rows, cm], axis=0)


def _gates(g, bias_row, sels):
    b, s, _ = g.shape
    L = GATE_CHUNKS * ML_CHUNK
    const = lambda shape: pl.BlockSpec(shape, lambda j: (0, 0))
    sel_shape = (N_PIECES * LANES, LANES)
    return pl.pallas_call(
        _gates_kernel,
        grid=(s // L,),
        in_specs=[pl.BlockSpec((b, L, G_COLS), lambda j: (0, j, 0)),
                  const((1, G_COLS)),
                  const(sel_shape), const(sel_shape), const(sel_shape),
                  const((1, LANES)), const((1, LANES)), const((1, LANES))],
        out_specs=[pl.BlockSpec((b, L, LANES), lambda j: (0, j, 0)),
                   pl.BlockSpec((b, LANES, L), lambda j: (0, 0, j)),
                   pl.BlockSpec((b, L, LANES), lambda j: (0, j, 0)),
                   pl.BlockSpec((b, 3 * SUBLANES, L), lambda j: (0, 0, j))],
        out_shape=[jax.ShapeDtypeStruct((b, s, LANES), BF16),
                   jax.ShapeDtypeStruct((b, LANES, s), BF16),
                   jax.ShapeDtypeStruct((b, s, LANES), BF16),
                   jax.ShapeDtypeStruct((b, 3 * SUBLANES, s), F32)],
        scratch_shapes=[pltpu.VMEM((b, SUBLANES, G_COLS), F32)],
        compiler_params=pltpu.CompilerParams(dimension_semantics=("arbitrary",)),
        name="gate_prefix",
    )(g, bias_row, *sels)


def _fox_kernel(qt_ref, qbt_ref, k_ref, kb_ref, vt_ref, zg_ref, o_ref,
                qa_sc, kf_sc, sa_sc, sb_sc, mxa_sc, mxb_sc, m_sc, acc_sc):
    qi = pl.program_id(1)
    tq, tk = FOX_TQ, FOX_TK
    hd = FOX_HEAD_DIM

    brow = lax.broadcasted_iota(jnp.int32, (LANES, tq), 0)
    qbt = qbt_ref[...]
    for h in range(FOX_HEADS):
        base = _fox_bias_base(h)
        own = (brow >= base) & (brow < base + SUBLANES)
        bias = jnp.where(own, qbt, jnp.zeros_like(qbt))
        qh = qt_ref[h * hd:(h + 1) * hd, :]
        if h % 2 == 0:
            qa_sc[h] = jnp.concatenate([qh, bias[hd:, :]], axis=0)
        else:
            qa_sc[h] = jnp.concatenate([bias[:hd, :], qh], axis=0)

    @pl.when(qi == 0)
    def _():
        klane = lax.broadcasted_iota(jnp.int32, (kb_ref.shape[0], LANES), 1)
        kbias = kb_ref[...]
        for h in range(FOX_HEADS):
            p = h // 2
            own = (klane < hd) if h % 2 == 0 else (klane >= hd)
            kf_sc[:, h * LANES:(h + 1) * LANES] = jnp.where(
                own, k_ref[:, p * LANES:(p + 1) * LANES], kbias)

    m_sc[...] = jnp.full_like(m_sc, -jnp.inf)
    acc_sc[...] = jnp.zeros_like(acc_sc)

    def logits(j, h, dst, mxd, key_offset):
        k0 = pl.multiple_of(j * tk, tk)
        kk = kf_sc[pl.ds(k0, tk), h * LANES:(h + 1) * LANES]
        st = jnp.dot(kk, qa_sc[h], preferred_element_type=F32)
        if key_offset is not None:
            kr = lax.broadcasted_iota(jnp.int32, (tk, tq), 0)
            qc = lax.broadcasted_iota(jnp.int32, (tk, tq), 1)
            st = jnp.where(kr + key_offset <= qc, st, -jnp.inf)
        dst[h] = st
        mxd[h] = jnp.max(st, axis=0, keepdims=True)

    def softmax_pv(j, h, src, mxs):
        k0 = pl.multiple_of(j * tk, tk)
        m_old = m_sc[h]
        m_new = jnp.maximum(m_old, mxs[h])
        alpha = jnp.exp2(m_old - m_new)
        pt = jnp.exp2(src[h] - m_new).astype(BF16)
        vt = vt_ref[h * FOX_VROWS:(h + 1) * FOX_VROWS, pl.ds(k0, tk)]
        acc_sc[h] = alpha * acc_sc[h] + jnp.dot(vt, pt, preferred_element_type=F32)
        m_sc[h] = m_new

    buf_a = (sa_sc, mxa_sc)
    buf_b = (sb_sc, mxb_sc)

    def stage(j_qk, dst, key_offset, j_sm, src):
        if j_qk is not None:
            logits(j_qk, 0, *dst, key_offset)
        for h in range(FOX_HEADS):
            if j_qk is not None and h + 1 < FOX_HEADS:
                logits(j_qk, h + 1, *dst, key_offset)
            if j_sm is not None:
                softmax_pv(j_sm, h, *src)

    stage(2 * qi, buf_a, 0, None, None)
    stage(2 * qi + 1, buf_b, tk, 2 * qi, buf_a)
    stage(0, buf_a, None, 2 * qi + 1, buf_b)

    def body(i, carry):
        stage(2 * i + 1, buf_b, None, 2 * i, buf_a)
        stage(2 * i + 2, buf_a, None, 2 * i + 1, buf_b)
        return carry

    lax.fori_loop(0, qi, body, 0)

    out_t = jnp.concatenate(
        [acc_sc[h, :hd, :] / acc_sc[h, hd:hd + 1, :] for h in range(FOX_HEADS)], axis=0)
    o_ref[...] = (out_t.T * zg_ref[...].astype(F32)).astype(BF16)


def _fox(qt, qbt, ufx, kb, vt):
    b, s, _ = ufx.shape
    tq = FOX_TQ
    return pl.pallas_call(
        _fox_kernel,
        grid=(b, s // tq),
        in_specs=[
            pl.BlockSpec((None, FOX_WIDTH, tq), lambda i, j: (i, 0, j)),
            pl.BlockSpec((None, LANES, tq), lambda i, j: (i, 0, j)),
            pl.BlockSpec((None, s, FOX_WIDTH), lambda i, j: (i, 0, 0)),
            pl.BlockSpec((None, s, LANES), lambda i, j: (i, 0, 0)),
            pl.BlockSpec((None, FOX_HEADS * FOX_VROWS, s), lambda i, j: (i, 0, 0)),
            pl.BlockSpec((None, tq, FOX_WIDTH), lambda i, j: (i, j, 1)),
        ],
        out_specs=pl.BlockSpec((None, tq, FOX_WIDTH), lambda i, j: (i, j, 0)),
        out_shape=jax.ShapeDtypeStruct((b, s, FOX_WIDTH), BF16),
        scratch_shapes=[pltpu.VMEM((FOX_HEADS, LANES, tq), BF16),
                        pltpu.VMEM((s, FOX_HEADS * LANES), BF16),
                        pltpu.VMEM((FOX_HEADS, FOX_TK, tq), F32),
                        pltpu.VMEM((FOX_HEADS, FOX_TK, tq), F32),
                        pltpu.VMEM((FOX_HEADS, 1, tq), F32),
                        pltpu.VMEM((FOX_HEADS, 1, tq), F32),
                        pltpu.VMEM((FOX_HEADS, 1, tq), F32),
                        pltpu.VMEM((FOX_HEADS, FOX_VROWS, tq), F32)],
        compiler_params=pltpu.CompilerParams(
            dimension_semantics=("arbitrary", "arbitrary"), vmem_limit_bytes=48 * 1024 * 1024),
        name="fox_attn",
    )(qt, qbt, ufx, kb, vt, ufx)


def _mlstm_kernel(kc_ref, qct_ref, mvt_ref, ogt_ref, zgt_ref, maug_ref, mrow_ref, ng_ref,
                  y_ref, c_sc, m_sc):
    L = ML_CHUNK
    dv = ML_HEAD_DIM

    @pl.when(pl.program_id(1) == 0)
    def _():
        c_sc[...] = jnp.zeros_like(c_sc)
        m_sc[...] = jnp.zeros_like(m_sc)

    kr = lax.broadcasted_iota(jnp.int32, (L, L), 0)
    qc = lax.broadcasted_iota(jnp.int32, (L, L), 1)
    visible = kr <= qc
    brow = lax.broadcasted_iota(jnp.int32, (ML_BIAS_ROWS, L), 0)
    bzero = jnp.zeros((LANES - ML_BIAS_ROWS, L), BF16)
    chains = [(bi, h) for bi in range(ML_NB) for h in range(ML_HEADS)]

    work = []
    for bi, h in chains:
        sc = bi * ML_HEADS + h
        r = mrow_ref[bi, h:h + 1, :]
        b = mrow_ref[bi, SUBLANES + h:SUBLANES + h + 1, :]
        cm = mrow_ref[bi, 2 * SUBLANES + h:2 * SUBLANES + h + 1, :]
        m_prev = m_sc[sc, 0:1, 0:1]
        g = jnp.maximum(m_prev, cm)
        g_last = g[:, L - 1:L]
        g_hi, g_mid, g_lo = (t.astype(F32) for t in _split3(g))
        rel = brow - SUBLANES * h
        blk = jnp.where(rel == N_PIECES, -g_hi,
                        jnp.where(rel == N_PIECES + 1, -g_mid,
                                  jnp.where(rel == N_PIECES + 2, -g_lo,
                                            jnp.where((rel >= 0) & (rel < N_PIECES), 1.0, 0.0))))
        bmat = jnp.concatenate([blk.astype(BF16), bzero], axis=0)
        k_h = kc_ref[bi, :, h * ML_PAD:(h + 1) * ML_PAD]
        q_h = qct_ref[bi, h * ML_PAD:(h + 1) * ML_PAD, :]
        ct = c_sc[sc]
        z = jnp.dot(maug_ref[bi], bmat, preferred_element_type=F32)
        s_raw = jnp.dot(k_h, q_h, preferred_element_type=F32)
        inter = jnp.dot(ct.astype(BF16), q_h, preferred_element_type=F32)
        work.append((r, b, g, g_last, m_prev, k_h, ct, z, s_raw, inter))

    outs = []
    for (bi, h), (r, b, g, g_last, m_prev, k_h, ct, z, s_raw, inter) in zip(chains, work):
        sc = bi * ML_HEADS + h
        d = jnp.exp(jnp.where(visible, z, -jnp.inf))
        p = (s_raw * d).astype(BF16)
        vt = mvt_ref[bi, h * ML_VROWS:(h + 1) * ML_VROWS, :]
        num = jnp.dot(vt, p, preferred_element_type=F32) + jnp.exp(m_prev - g) * inter
        den = num[dv:dv + 1, :]
        hv = num[:dv, :] / jnp.maximum(jnp.abs(den), jnp.exp(-(b + g)))

        wv = (vt.astype(F32) * jnp.exp(r - g_last)).astype(BF16)
        c_sc[sc] = jnp.exp(m_prev - g_last) * ct + jnp.dot(wv, k_h, preferred_element_type=F32)
        m_sc[sc] = jnp.broadcast_to(b[:, L - 1:L] + g_last, (SUBLANES, LANES))

        hv = hv * ogt_ref[bi, h * dv:(h + 1) * dv, :].astype(F32)
        mu = jnp.mean(hv, axis=0, keepdims=True)
        dd = hv - mu
        var = jnp.mean(dd * dd, axis=0, keepdims=True)
        hn = dd * lax.rsqrt(var + LN_EPS) * ng_ref[h * dv:(h + 1) * dv, :]
        outs.append(hn * zgt_ref[bi, h * dv:(h + 1) * dv, :].astype(F32))
        if h == ML_HEADS - 1:
            y_ref[bi] = jnp.concatenate(outs, axis=0).T.astype(BF16)
            outs = []


def _mlstm(kc, qct, mvt, ogt, zgt, maug, mrow, norm_g_b):
    b, s, _ = kc.shape
    L = ML_CHUNK

    def col(n):
        return pl.BlockSpec((ML_NB, n, L), lambda i, j: (i, 0, j))

    def row(n):
        return pl.BlockSpec((ML_NB, L, n), lambda i, j: (i, j, 0))

    return pl.pallas_call(
        _mlstm_kernel,
        grid=(b // ML_NB, s // L),
        in_specs=[row(ML_PW), col(ML_PW), col(ML_HEADS * ML_VROWS), col(ML_WIDTH), col(ML_WIDTH),
                  row(LANES), col(3 * SUBLANES),
                  pl.BlockSpec((ML_WIDTH, L), lambda i, j: (0, 0))],
        out_specs=row(ML_WIDTH),
        out_shape=jax.ShapeDtypeStruct((b, s, ML_WIDTH), BF16),
        scratch_shapes=[pltpu.VMEM((ML_NB * ML_HEADS, ML_VROWS, LANES), F32),
                        pltpu.VMEM((ML_NB * ML_HEADS, SUBLANES, LANES), F32)],
        compiler_params=pltpu.CompilerParams(
            dimension_semantics=("arbitrary", "arbitrary"), vmem_limit_bytes=48 * 1024 * 1024),
        name="mlstm",
    )(kc, qct, mvt, ogt, zgt, maug, mrow, norm_g_b)


def _memkv_kernel(m_ref, w_ref, k4_ref, vt_ref):
    w = w_ref[...].astype(BF16)
    for bi in range(m_ref.shape[0]):
        kv = jnp.dot(m_ref[bi].astype(BF16), w, preferred_element_type=F32)
        k = kv[:, :MEM_WIDTH] * (MEM_HEAD_DIM ** -0.5)
        lane = lax.broadcasted_iota(jnp.int32, k.shape, 1)
        for h in range(MEM_HEADS):
            own = (lane >= MEM_HEAD_DIM * h) & (lane < MEM_HEAD_DIM * (h + 1))
            k4_ref[bi, h] = jnp.where(own, k, 0.0).astype(BF16)
        vt_ref[bi] = _with_ones_rows(kv[:, MEM_WIDTH:].T.astype(BF16), MEM_HEAD_DIM, MEM_HEADS)


def _memkv(mem, w_mem_kv):
    b, ml, _ = mem.shape
    return pl.pallas_call(
        _memkv_kernel,
        grid=(DEPTH,),
        in_specs=[pl.BlockSpec((b, ml, D_MODEL), lambda l: (0, 0, 0)),
                  pl.BlockSpec((None, D_MODEL, 2 * MEM_WIDTH), lambda l: (l, 0, 0))],
        out_specs=[pl.BlockSpec((None, b, MEM_HEADS, ml, MEM_WIDTH), lambda l: (l, 0, 0, 0, 0)),
                   pl.BlockSpec((None, b, MEM_HEADS * MEM_VROWS, ml), lambda l: (l, 0, 0, 0))],
        out_shape=[jax.ShapeDtypeStruct((DEPTH, b, MEM_HEADS, ml, MEM_WIDTH), BF16),
                   jax.ShapeDtypeStruct((DEPTH, b, MEM_HEADS * MEM_VROWS, ml), BF16)],
        compiler_params=pltpu.CompilerParams(dimension_semantics=("arbitrary",)),
        name="mem_kv",
    )(mem, w_mem_kv)


def _out_kernel(x_ref, yf_ref, ym_ref, rqt_ref, rzg_ref, k4_ref, vt_ref, wo_ref, g_ref, b_ref,
                o_ref, ob_ref):
    hd = MEM_HEAD_DIM
    o_mem = FOX_WIDTH + ML_WIDTH
    n_sub = OUT_TM // OUT_SUB
    ahead = 2

    def rows(q):
        return slice(q * OUT_SUB, (q + 1) * OUT_SUB)

    def main_dot(q):
        y_main = jnp.concatenate([yf_ref[rows(q), :], ym_ref[rows(q), :]], axis=1)
        return jnp.dot(y_main, wo_ref[0:o_mem, :], preferred_element_type=F32)

    rqt = rqt_ref[...]
    sts = [jnp.dot(k4_ref[h], rqt, preferred_element_type=F32) for h in range(MEM_HEADS)]
    mains = [main_dot(q) for q in range(ahead)]

    outs = []
    for h in range(MEM_HEADS):
        st = sts[h]
        p = jnp.exp(st - jnp.max(st, axis=0, keepdims=True)).astype(BF16)
        acc = jnp.dot(vt_ref[h * MEM_VROWS:(h + 1) * MEM_VROWS, :], p,
                      preferred_element_type=F32)
        outs.append(acc[:hd, :] / acc[hd:hd + 1, :])
    att = jnp.concatenate(outs, axis=0).T
    y_mem = (att * rzg_ref[...].astype(F32)).astype(BF16)

    for q in range(n_sub):
        if q + ahead < n_sub:
            mains.append(main_dot(q + ahead))
        y = mains[q] + jnp.dot(y_mem[rows(q), :], wo_ref[o_mem:, :], preferred_element_type=F32)
        r = DEEPNORM_ALPHA * x_ref[rows(q), :] + y
        mu = jnp.mean(r, axis=1, keepdims=True)
        d = r - mu
        var = jnp.mean(d * d, axis=1, keepdims=True)
        out = d * lax.rsqrt(var + LN_EPS) * g_ref[...] + b_ref[...]
        o_ref[rows(q), :] = out
        ob_ref[rows(q), :] = out.astype(BF16)


def _out_proj(x, y_fox, y_ml, rqt, rzg, mem_k4, mem_vt, layer, w_out, ln_g, ln_b):
    b, s, _ = x.shape
    tm = OUT_TM
    ml = mem_k4.shape[3]
    return pl.pallas_call(
        _out_kernel,
        grid=(b, s // tm),
        in_specs=[
            pl.BlockSpec((None, tm, D_MODEL), lambda i, j: (i, j, 0)),
            pl.BlockSpec((None, tm, FOX_WIDTH), lambda i, j: (i, j, 0)),
            pl.BlockSpec((None, tm, ML_WIDTH), lambda i, j: (i, j, 0)),
            pl.BlockSpec((None, MEM_WIDTH, tm), lambda i, j: (i, 0, j)),
            pl.BlockSpec((None, tm, MEM_WIDTH), lambda i, j: (i, j, 0)),
            pl.BlockSpec((None, None, MEM_HEADS, ml, MEM_WIDTH), lambda i, j: (layer, i, 0, 0, 0)),
            pl.BlockSpec((None, None, MEM_HEADS * MEM_VROWS, ml), lambda i, j: (layer, i, 0, 0)),
            pl.BlockSpec((None, D_MODEL, D_MODEL), lambda i, j: (layer, 0, 0)),
            pl.BlockSpec((None, 1, D_MODEL), lambda i, j: (layer, 0, 0)),
            pl.BlockSpec((None, 1, D_MODEL), lambda i, j: (layer, 0, 0)),
        ],
        out_specs=[pl.BlockSpec((None, tm, D_MODEL), lambda i, j: (i, j, 0)),
                   pl.BlockSpec((None, tm, D_MODEL), lambda i, j: (i, j, 0))],
        out_shape=[jax.ShapeDtypeStruct((b, s, D_MODEL), F32),
                   jax.ShapeDtypeStruct((b, s, D_MODEL), BF16)],
        compiler_params=pltpu.CompilerParams(
            dimension_semantics=("arbitrary", "arbitrary"), vmem_limit_bytes=48 * 1024 * 1024),
        name="out_proj_ln",
    )(x, y_fox, y_ml, rqt, rzg, mem_k4, mem_vt, w_out, ln_g, ln_b)


def kernel(x, mem, w_in, fox_f_bias, mlstm_conv_w, mlstm_conv_b, mlstm_i_bias, mlstm_f_bias,
           mlstm_norm_g, w_mem_kv, w_out, ln_g, ln_b):
    w_re = _relayout_w_in(w_in)
    w_out_b = w_out.astype(BF16)
    bias_rows = _spread_gates(fox_f_bias, mlstm_i_bias, mlstm_f_bias)[:, None, :]
    conv_w = jnp.concatenate([_pad_heads_last(mlstm_conv_w[..., :ML_WIDTH]),
                              _pad_heads_last(mlstm_conv_w[..., ML_WIDTH:])], axis=-1)
    conv_b = jnp.concatenate([_pad_heads_last(mlstm_conv_b[..., :ML_WIDTH]),
                              _pad_heads_last(mlstm_conv_b[..., ML_WIDTH:])], axis=-1)[:, None, :]
    norm_g_b = jnp.broadcast_to(mlstm_norm_g[:, :, None], (DEPTH, ML_WIDTH, ML_CHUNK))
    ln_g3 = ln_g[:, None, :]
    ln_b3 = ln_b[:, None, :]
    sels = _bias_selectors()

    mem_k4, mem_vt = _memkv(mem, w_mem_kv)
    x_mxu = x
    for l in range(DEPTH):
        kc, qct, mvt, ogt, zgt, ufx, rqt, rzg, g, fqt, fvt = _project(x_mxu, w_re, conv_w, conv_b, l)
        kb, qbt, maug, mrow = _gates(g, bias_rows[l], sels)
        y_fox = _fox(fqt, qbt, ufx, kb, fvt)
        y_ml = _mlstm(kc, qct, mvt, ogt, zgt, maug, mrow, norm_g_b[l])
        x, x_mxu = _out_proj(x, y_fox, y_ml, rqt, rzg, mem_k4, mem_vt, l, w_out_b, ln_g3, ln_b3)
    return x
```

```python
import math

import numpy as np
import jax
import jax.numpy as jnp
from jax import lax
from jax.experimental import pallas as pl
from jax.experimental.pallas import tpu as pltpu

F32 = jnp.float32
BF16 = jnp.bfloat16

D_MODEL = 1024
DEPTH = 4
FOX_HEADS = 6
FOX_HEAD_DIM = 64
FOX_WIDTH = FOX_HEADS * FOX_HEAD_DIM
ML_HEADS = 4
ML_HEAD_DIM = 96
ML_WIDTH = ML_HEADS * ML_HEAD_DIM
MEM_HEADS = 4
MEM_HEAD_DIM = 64
MEM_WIDTH = MEM_HEADS * MEM_HEAD_DIM
CONV_WIDTH = 4
LN_EPS = 1e-5
DEEPNORM_ALPHA = (2.0 * DEPTH) ** 0.25
LOG2E = math.log2(math.e)

LANES = 128
SUBLANES = 8
BF16_ROWS = 16
ML_PAD = LANES
ML_PW = ML_HEADS * ML_PAD
ML_VROWS = ML_HEAD_DIM + BF16_ROWS
FOX_VROWS = FOX_HEAD_DIM + BF16_ROWS
MEM_VROWS = MEM_HEAD_DIM + BF16_ROWS

COL_QV = 0
COL_MK = COL_QV + 2 * ML_WIDTH
COL_MO = COL_MK + ML_PW
COL_FZ = COL_MO + 2 * ML_WIDTH
COL_FQ = COL_FZ + 2 * FOX_WIDTH
COL_RQ = COL_FQ + 2 * FOX_WIDTH
W_COLS = COL_RQ + 2 * MEM_WIDTH
G_COLS = LANES
GATE_FOX = ML_HEAD_DIM
GATE_I = GATE_FOX + SUBLANES
GATE_F = GATE_I + SUBLANES
N_PIECES = 3
ML_BIAS_ROWS = 2 * BF16_ROWS

ML_CHUNK = 256
ML_NB = 4
GATE_CHUNKS = 2
FOX_TQ = 512
FOX_TK = 256
PROJ_TM = 512
OUT_TM = 1024
OUT_SUB = 256

_SRC = {}
_off = 0
for _name, _w in (("fq", FOX_WIDTH), ("fk", FOX_WIDTH), ("fv", FOX_WIDTH), ("ff", FOX_HEADS),
                  ("fz", FOX_WIDTH), ("mq", ML_WIDTH), ("mk", ML_WIDTH), ("mv", ML_WIDTH),
                  ("mi", ML_HEADS), ("mf", ML_HEADS), ("mo", ML_WIDTH), ("mz", ML_WIDTH),
                  ("rq", MEM_WIDTH), ("rz", MEM_WIDTH)):
    _SRC[_name] = (_off, _w)
    _off += _w


def _sigmoid(x):
    return 1.0 / (1.0 + jnp.exp2(x * (-LOG2E)))


def _silu(x):
    return x * _sigmoid(x)


def _split3(x):
    hi = x.astype(BF16)
    r1 = x - hi.astype(F32)
    mid = r1.astype(BF16)
    lo = (r1 - mid.astype(F32)).astype(BF16)
    return hi, mid, lo


def _with_ones_rows(t, head_dim, heads):
    ones = jnp.ones((BF16_ROWS, t.shape[1]), t.dtype)
    rows = []
    for h in range(heads):
        rows += [t[h * head_dim:(h + 1) * head_dim, :], ones]
    return jnp.concatenate(rows, axis=0)


def _zeros_like_cols(t, n):
    return jnp.zeros(t.shape[:-1] + (n,), t.dtype)


def _pad_heads_last(t):
    parts = []
    for h in range(ML_HEADS):
        parts.append(t[..., h * ML_HEAD_DIM:(h + 1) * ML_HEAD_DIM])
        parts.append(_zeros_like_cols(t, ML_PAD - ML_HEAD_DIM))
    return jnp.concatenate(parts, axis=-1)


def _spread_gates(ff, mi, mf):
    parts = [_zeros_like_cols(ff, GATE_FOX),
             ff, _zeros_like_cols(ff, GATE_I - GATE_FOX - FOX_HEADS),
             mi, _zeros_like_cols(mi, GATE_F - GATE_I - ML_HEADS),
             mf, _zeros_like_cols(mf, G_COLS - GATE_F - ML_HEADS)]
    return jnp.concatenate(parts, axis=-1)


def _w_in_segments():
    segs = []

    def put(name, dst, scale=1.0):
        o, n = _SRC[name]
        segs.append((o, dst, n, scale))
        return dst + n

    def put_padded_heads(name, dst):
        o, _ = _SRC[name]
        for h in range(ML_HEADS):
            segs.append((o + h * ML_HEAD_DIM, dst + h * ML_PAD, ML_HEAD_DIM, 1.0))
        return dst + ML_PW

    c = put("mq", COL_QV)
    c = put("mv", c)
    c = put_padded_heads("mk", c)
    for name, lane in (("ff", GATE_FOX), ("mi", GATE_I), ("mf", GATE_F)):
        o, n = _SRC[name]
        segs.append((o, COL_MK + lane, n, 1.0))
    for name in ("mo", "mz", "fz", "fk"):
        c = put(name, c)
    c = put("fq", c, FOX_HEAD_DIM ** -0.5 * LOG2E)
    for name in ("fv", "rq", "rz"):
        c = put(name, c)
    assert c == W_COLS
    return segs


W_PREP_ROWS = 256


def _w_prep_kernel(w_ref, o_ref):
    o_ref[...] = jnp.zeros_like(o_ref)
    for src, dst, n, scale in _w_in_segments():
        seg = w_ref[:, src:src + n]
        if scale != 1.0:
            seg = seg * scale
        o_ref[:, dst:dst + n] = seg.astype(BF16)


def _relayout_w_in(w):
    depth, rows, cols = w.shape
    tr = W_PREP_ROWS
    return pl.pallas_call(
        _w_prep_kernel,
        grid=(depth, rows // tr),
        in_specs=[pl.BlockSpec((None, tr, cols), lambda l, i: (l, i, 0))],
        out_specs=pl.BlockSpec((None, tr, W_COLS), lambda l, i: (l, i, 0)),
        out_shape=jax.ShapeDtypeStruct((depth, rows, W_COLS), BF16),
        compiler_params=pltpu.CompilerParams(dimension_semantics=("arbitrary", "arbitrary")),
        name="w_in_prep",
    )(w)


def _fox_bias_base(h):
    return SUBLANES * (h // 2) + (FOX_HEAD_DIM if h % 2 == 0 else 0)


def _bias_selectors():
    n = N_PIECES * LANES
    selk = np.zeros((n, LANES), np.float32)
    selq = np.zeros((n, LANES), np.float32)
    selm = np.zeros((n, LANES), np.float32)
    onek = np.zeros((1, LANES), np.float32)
    oneq = np.zeros((1, LANES), np.float32)
    onem = np.zeros((1, LANES), np.float32)
    for h in range(FOX_HEADS):
        src = GATE_FOX + h
        base = _fox_bias_base(h)
        for j in range(N_PIECES):
            selk[j * LANES + src, base + j] = -1.0
            selq[j * LANES + src, base + N_PIECES + j] = 1.0
            onek[0, base + N_PIECES + j] = 1.0
            oneq[0, base + j] = 1.0
    for h in range(ML_HEADS):
        base = SUBLANES * h
        for j in range(N_PIECES):
            selm[j * LANES + GATE_I + h, base + j] = 1.0
            onem[0, base + N_PIECES + j] = 1.0
    return (jnp.asarray(selk, BF16), jnp.asarray(selq, BF16), jnp.asarray(selm, BF16),
            jnp.asarray(onek), jnp.asarray(oneq), jnp.asarray(onem))


def _causal_conv(u, w_ref, b_ref, tail_sc):
    n = u.shape[0]
    tail = tail_sc[...]
    w = w_ref[...]
    row8 = lax.broadcasted_iota(jnp.int32, tail.shape, 0)
    y = u * w[CONV_WIDTH - 1:CONV_WIDTH, :] + b_ref[...]
    for shift in range(1, CONV_WIDTH):
        r = pltpu.roll(u, shift, axis=0)
        rt = pltpu.roll(tail, shift, axis=0)
        first = jnp.where(row8 < shift, rt, r[:SUBLANES])
        xs = jnp.concatenate([first, r[SUBLANES:]], axis=0)
        y = y + xs * w[CONV_WIDTH - 1 - shift:CONV_WIDTH - shift, :]
    tail_sc[...] = u[n - SUBLANES:, :]
    return y


def _proj_kernel(x_ref, w_ref, cwq_ref, cbq_ref, cwk_ref, cbk_ref,
                 kc_ref, qct_ref, mvt_ref, ogt_ref, zgt_ref, ufx_ref, rqt_ref, rzg_ref, g_ref,
                 fqt_ref, fvt_ref, tailq_sc, tailk_sc):
    @pl.when(pl.program_id(1) == 0)
    def _():
        tailq_sc[...] = jnp.zeros_like(tailq_sc)
        tailk_sc[...] = jnp.zeros_like(tailk_sc)

    xb = x_ref[...].astype(BF16)

    def mm(c0, n):
        return jnp.dot(xb, w_ref[:, c0:c0 + n], preferred_element_type=F32)

    t = mm(COL_QV, 2 * ML_WIDTH)
    qct_ref[...] = _silu(_causal_conv(t[:, :ML_WIDTH], cwq_ref, cbq_ref, tailq_sc)).T.astype(BF16)
    mvt_ref[...] = _with_ones_rows(t[:, ML_WIDTH:].T.astype(BF16), ML_HEAD_DIM, ML_HEADS)

    t = mm(COL_MK, ML_PW)
    g_ref[...] = t[:, :G_COLS]
    kc_ref[...] = (_silu(_causal_conv(t, cwk_ref, cbk_ref, tailk_sc))
                   * (ML_HEAD_DIM ** -0.5)).astype(BF16)

    t = mm(COL_MO, 2 * ML_WIDTH)
    ogt_ref[...] = _sigmoid(t[:, :ML_WIDTH]).T.astype(BF16)
    zgt_ref[...] = _silu(t[:, ML_WIDTH:]).T.astype(BF16)

    t = mm(COL_FZ, 2 * FOX_WIDTH)
    ufx_ref[:, FOX_WIDTH:] = _silu(t[:, :FOX_WIDTH]).astype(BF16)
    ufx_ref[:, :FOX_WIDTH] = t[:, FOX_WIDTH:].astype(BF16)

    t = mm(COL_FQ, 2 * FOX_WIDTH)
    fqt_ref[...] = t[:, :FOX_WIDTH].T.astype(BF16)
    fvt_ref[...] = _with_ones_rows(t[:, FOX_WIDTH:].T.astype(BF16), FOX_HEAD_DIM, FOX_HEADS)

    t = mm(COL_RQ, 2 * MEM_WIDTH)
    rqt_ref[...] = t[:, :MEM_WIDTH].T.astype(BF16)
    rzg_ref[...] = _silu(t[:, MEM_WIDTH:]).astype(BF16)


def _project(x, w_re, conv_q, conv_k, layer):
    b, s, _ = x.shape
    tm = PROJ_TM

    def row(n):
        return pl.BlockSpec((None, tm, n), lambda i, j: (i, j, 0))

    def col(n):
        return pl.BlockSpec((None, n, tm), lambda i, j: (i, 0, j))

    def per_layer(shape):
        return pl.BlockSpec((None,) + shape, lambda i, j: (layer, 0, 0))

    def rows_shape(n):
        return jax.ShapeDtypeStruct((b, s, n), BF16)

    def cols_shape(n):
        return jax.ShapeDtypeStruct((b, n, s), BF16)

    return pl.pallas_call(
        _proj_kernel,
        grid=(b, s // tm),
        in_specs=[row(D_MODEL), per_layer((D_MODEL, W_COLS)),
                  per_layer((CONV_WIDTH, ML_WIDTH)), per_layer((1, ML_WIDTH)),
                  per_layer((CONV_WIDTH, ML_PW)), per_layer((1, ML_PW))],
        out_specs=[row(ML_PW), col(ML_WIDTH), col(ML_HEADS * ML_VROWS), col(ML_WIDTH),
                   col(ML_WIDTH),
                   row(2 * FOX_WIDTH), col(MEM_WIDTH), row(MEM_WIDTH), row(G_COLS),
                   col(FOX_WIDTH), col(FOX_HEADS * FOX_VROWS)],
        out_shape=[rows_shape(ML_PW), cols_shape(ML_WIDTH), cols_shape(ML_HEADS * ML_VROWS),
                   cols_shape(ML_WIDTH), cols_shape(ML_WIDTH),
                   rows_shape(2 * FOX_WIDTH), cols_shape(MEM_WIDTH), rows_shape(MEM_WIDTH),
                   jax.ShapeDtypeStruct((b, s, G_COLS), F32),
                   cols_shape(FOX_WIDTH), cols_shape(FOX_HEADS * FOX_VROWS)],
        scratch_shapes=[pltpu.VMEM((SUBLANES, ML_WIDTH), F32),
                        pltpu.VMEM((SUBLANES, ML_PW), F32)],
        compiler_params=pltpu.CompilerParams(
            dimension_semantics=("arbitrary", "arbitrary"), vmem_limit_bytes=56 * 1024 * 1024),
        name="in_proj",
    )(x, w_re, *conv_q, *conv_k)


def _gates_kernel(g_ref, bias_ref, selk_ref, selq_ref, selm_ref, onek_ref, oneq_ref, onem_ref,
                  kb_ref, qbt_ref, maug_ref, mrow_ref, carry_ref):
    @pl.when(pl.program_id(0) == 0)
    def _():
        carry_ref[...] = jnp.zeros_like(carry_ref)

    L = ML_CHUNK
    r = lax.broadcasted_iota(jnp.int32, (L, L), 0)
    c = lax.broadcasted_iota(jnp.int32, (L, L), 1)
    tri = (c <= r).astype(BF16)
    lane = lax.broadcasted_iota(jnp.int32, (L, G_COLS), 1)
    pos = lax.broadcasted_iota(jnp.int32, (SUBLANES, L), 1)

    chains = [(bi, ci) for bi in range(g_ref.shape[0]) for ci in range(GATE_CHUNKS)]

    def span(ci):
        return slice(ci * L, (ci + 1) * L)

    xs, parts = [], []
    for bi, ci in chains:
        x = g_ref[bi, span(ci), :] + bias_ref[...]
        log_sig = jnp.minimum(x, 0.0) - jnp.log1p(jnp.exp(-jnp.abs(x)))
        xs.append(x)
        parts.append(jnp.dot(tri, jnp.concatenate(_split3(log_sig), axis=1),
                             preferred_element_type=F32))

    kbs, qbs, maugs, rowss = [], [], [], []
    carry = None
    for (bi, ci), x, p in zip(chains, xs, parts):
        if ci == 0:
            carry = carry_ref[bi, 0:1, :]
        local = (p[:, :LANES] + p[:, LANES:2 * LANES]) + p[:, 2 * LANES:]
        c_run = local + carry
        carry = carry + local[L - 1:L, :]
        if ci == GATE_CHUNKS - 1:
            carry_ref[bi, 0:1, :] = carry
        pieces = jnp.concatenate(_split3(c_run * LOG2E), axis=1)
        kbs.append(jnp.dot(pieces, selk_ref[...], preferred_element_type=F32))
        qbs.append(jnp.dot(pieces, selq_ref[...], preferred_element_type=F32))
        rc = x - pltpu.roll(local, LANES - (GATE_F - GATE_I), axis=1)
        pieces = jnp.concatenate(_split3(rc), axis=1)
        maugs.append(jnp.dot(pieces, selm_ref[...], preferred_element_type=F32))
        rowss.append(jnp.where(lane < GATE_F, rc, local))

    for (bi, ci), kb, qb, maug, rows in zip(chains, kbs, qbs, maugs, rowss):
        kb_ref[bi, span(ci), :] = (kb + onek_ref[...]).astype(BF16)
        qbt_ref[bi, :, span(ci)] = (qb + oneq_ref[...]).T.astype(BF16)
        maug_ref[bi, span(ci), :] = (maug + onem_ref[...]).astype(BF16)
        rows = rows.T[GATE_I:GATE_I + 2 * SUBLANES, :]
        cm = rows[:SUBLANES]
        shift = 1
        while shift < L:
            cm = jnp.maximum(cm, jnp.where(pos >= shift, pltpu.roll(cm, shift, axis=1), -jnp.inf))
            shift *= 2
        mrow_ref[bi, :, span(ci)] = jnp.concatenate([rows, cm], axis=0)


def _gates(g, bias_row, sels):
    b, s, _ = g.shape
    L = GATE_CHUNKS * ML_CHUNK
    const = lambda shape: pl.BlockSpec(shape, lambda j: (0, 0))
    sel_shape = (N_PIECES * LANES, LANES)
    return pl.pallas_call(
        _gates_kernel,
        grid=(s // L,),
        in_specs=[pl.BlockSpec((b, L, G_COLS), lambda j: (0, j, 0)),
                  const((1, G_COLS)),
                  const(sel_shape), const(sel_shape), const(sel_shape),
                  const((1, LANES)), const((1, LANES)), const((1, LANES))],
        out_specs=[pl.BlockSpec((b, L, LANES), lambda j: (0, j, 0)),
                   pl.BlockSpec((b, LANES, L), lambda j: (0, 0, j)),
                   pl.BlockSpec((b, L, LANES), lambda j: (0, j, 0)),
                   pl.BlockSpec((b, 3 * SUBLANES, L), lambda j: (0, 0, j))],
        out_shape=[jax.ShapeDtypeStruct((b, s, LANES), BF16),
                   jax.ShapeDtypeStruct((b, LANES, s), BF16),
                   jax.ShapeDtypeStruct((b, s, LANES), BF16),
                   jax.ShapeDtypeStruct((b, 3 * SUBLANES, s), F32)],
        scratch_shapes=[pltpu.VMEM((b, SUBLANES, G_COLS), F32)],
        compiler_params=pltpu.CompilerParams(dimension_semantics=("arbitrary",)),
        name="gate_prefix",
    )(g, bias_row, *sels)


def _fox_kernel(qt_ref, qbt_ref, k_ref, kb_ref, vt_ref, zg_ref, o_ref,
                qa_sc, kf_sc, sa_sc, sb_sc, mxa_sc, mxb_sc, m_sc, acc_sc):
    qi = pl.program_id(1)
    tq, tk = FOX_TQ, FOX_TK
    hd = FOX_HEAD_DIM

    brow = lax.broadcasted_iota(jnp.int32, (LANES, tq), 0)
    qbt = qbt_ref[...]
    for h in range(FOX_HEADS):
        base = _fox_bias_base(h)
        own = (brow >= base) & (brow < base + SUBLANES)
        bias = jnp.where(own, qbt, jnp.zeros_like(qbt))
        qh = qt_ref[h * hd:(h + 1) * hd, :]
        if h % 2 == 0:
            qa_sc[h] = jnp.concatenate([qh, bias[hd:, :]], axis=0)
        else:
            qa_sc[h] = jnp.concatenate([bias[:hd, :], qh], axis=0)

    @pl.when(qi == 0)
    def _():
        klane = lax.broadcasted_iota(jnp.int32, (kb_ref.shape[0], LANES), 1)
        kbias = kb_ref[...]
        for h in range(FOX_HEADS):
            p = h // 2
            own = (klane < hd) if h % 2 == 0 else (klane >= hd)
            kf_sc[:, h * LANES:(h + 1) * LANES] = jnp.where(
                own, k_ref[:, p * LANES:(p + 1) * LANES], kbias)

    m_sc[...] = jnp.full_like(m_sc, -jnp.inf)
    acc_sc[...] = jnp.zeros_like(acc_sc)

    def logits(j, h, dst, mxd, key_offset):
        k0 = pl.multiple_of(j * tk, tk)
        kk = kf_sc[pl.ds(k0, tk), h * LANES:(h + 1) * LANES]
        st = jnp.dot(kk, qa_sc[h], preferred_element_type=F32)
        if key_offset is not None:
            kr = lax.broadcasted_iota(jnp.int32, (tk, tq), 0)
            qc = lax.broadcasted_iota(jnp.int32, (tk, tq), 1)
            st = jnp.where(kr + key_offset <= qc, st, -jnp.inf)
        dst[h] = st
        mxd[h] = jnp.max(st, axis=0, keepdims=True)

    def softmax_pv(j, h, src, mxs):
        k0 = pl.multiple_of(j * tk, tk)
        m_old = m_sc[h]
        m_new = jnp.maximum(m_old, mxs[h])
        alpha = jnp.exp2(m_old - m_new)
        pt = jnp.exp2(src[h] - m_new).astype(BF16)
        vt = vt_ref[h * FOX_VROWS:(h + 1) * FOX_VROWS, pl.ds(k0, tk)]
        acc_sc[h] = alpha * acc_sc[h] + jnp.dot(vt, pt, preferred_element_type=F32)
        m_sc[h] = m_new

    buf_a = (sa_sc, mxa_sc)
    buf_b = (sb_sc, mxb_sc)

    def stage(j_qk, dst, key_offset, j_sm, src):
        if j_qk is not None:
            logits(j_qk, 0, *dst, key_offset)
        for h in range(FOX_HEADS):
            if j_qk is not None and h + 1 < FOX_HEADS:
                logits(j_qk, h + 1, *dst, key_offset)
            if j_sm is not None:
                softmax_pv(j_sm, h, *src)

    stage(2 * qi, buf_a, 0, None, None)
    stage(2 * qi + 1, buf_b, tk, 2 * qi, buf_a)
    stage(0, buf_a, None, 2 * qi + 1, buf_b)

    def body(i, carry):
        stage(2 * i + 1, buf_b, None, 2 * i, buf_a)
        stage(2 * i + 2, buf_a, None, 2 * i + 1, buf_b)
        return carry

    lax.fori_loop(0, qi, body, 0)

    out_t = jnp.concatenate(
        [acc_sc[h, :hd, :] / acc_sc[h, hd:hd + 1, :] for h in range(FOX_HEADS)], axis=0)
    o_ref[...] = (out_t.T * zg_ref[...].astype(F32)).astype(BF16)


def _fox(qt, qbt, ufx, kb, vt):
    b, s, _ = ufx.shape
    tq = FOX_TQ
    return pl.pallas_call(
        _fox_kernel,
        grid=(b, s // tq),
        in_specs=[
            pl.BlockSpec((None, FOX_WIDTH, tq), lambda i, j: (i, 0, j)),
            pl.BlockSpec((None, LANES, tq), lambda i, j: (i, 0, j)),
            pl.BlockSpec((None, s, FOX_WIDTH), lambda i, j: (i, 0, 0)),
            pl.BlockSpec((None, s, LANES), lambda i, j: (i, 0, 0)),
            pl.BlockSpec((None, FOX_HEADS * FOX_VROWS, s), lambda i, j: (i, 0, 0)),
            pl.BlockSpec((None, tq, FOX_WIDTH), lambda i, j: (i, j, 1)),
        ],
        out_specs=pl.BlockSpec((None, tq, FOX_WIDTH), lambda i, j: (i, j, 0)),
        out_shape=jax.ShapeDtypeStruct((b, s, FOX_WIDTH), BF16),
        scratch_shapes=[pltpu.VMEM((FOX_HEADS, LANES, tq), BF16),
                        pltpu.VMEM((s, FOX_HEADS * LANES), BF16),
                        pltpu.VMEM((FOX_HEADS, FOX_TK, tq), F32),
                        pltpu.VMEM((FOX_HEADS, FOX_TK, tq), F32),
                        pltpu.VMEM((FOX_HEADS, 1, tq), F32),
                        pltpu.VMEM((FOX_HEADS, 1, tq), F32),
                        pltpu.VMEM((FOX_HEADS, 1, tq), F32),
                        pltpu.VMEM((FOX_HEADS, FOX_VROWS, tq), F32)],
        compiler_params=pltpu.CompilerParams(
            dimension_semantics=("arbitrary", "arbitrary"), vmem_limit_bytes=48 * 1024 * 1024),
        name="fox_attn",
    )(qt, qbt, ufx, kb, vt, ufx)


def _mlstm_kernel(kc_ref, qct_ref, mvt_ref, ogt_ref, zgt_ref, maug_ref, mrow_ref, ng_ref,
                  y_ref, c_sc, m_sc):
    L = ML_CHUNK
    dv = ML_HEAD_DIM

    @pl.when(pl.program_id(1) == 0)
    def _():
        c_sc[...] = jnp.zeros_like(c_sc)
        m_sc[...] = jnp.zeros_like(m_sc)

    kr = lax.broadcasted_iota(jnp.int32, (L, L), 0)
    qc = lax.broadcasted_iota(jnp.int32, (L, L), 1)
    visible = kr <= qc
    brow = lax.broadcasted_iota(jnp.int32, (ML_BIAS_ROWS, L), 0)
    bzero = jnp.zeros((LANES - ML_BIAS_ROWS, L), BF16)
    qzero = jnp.zeros((ML_PAD - dv, L), BF16)
    chains = [(bi, h) for bi in range(ML_NB) for h in range(ML_HEADS)]

    work = []
    for bi, h in chains:
        sc = bi * ML_HEADS + h
        r = mrow_ref[bi, h:h + 1, :]
        b = mrow_ref[bi, SUBLANES + h:SUBLANES + h + 1, :]
        cm = mrow_ref[bi, 2 * SUBLANES + h:2 * SUBLANES + h + 1, :]
        m_prev = m_sc[sc, 0:1, 0:1]
        g = jnp.maximum(m_prev, cm)
        g_last = g[:, L - 1:L]
        g_hi, g_mid, g_lo = (t.astype(F32) for t in _split3(g))
        rel = brow - SUBLANES * h
        blk = jnp.where(rel == N_PIECES, -g_hi,
                        jnp.where(rel == N_PIECES + 1, -g_mid,
                                  jnp.where(rel == N_PIECES + 2, -g_lo,
                                            jnp.where((rel >= 0) & (rel < N_PIECES), 1.0, 0.0))))
        bmat = jnp.concatenate([blk.astype(BF16), bzero], axis=0)
        k_h = kc_ref[bi, :, h * ML_PAD:(h + 1) * ML_PAD]
        q_h = jnp.concatenate([qct_ref[bi, h * dv:(h + 1) * dv, :], qzero], axis=0)
        ct = c_sc[sc]
        z = jnp.dot(maug_ref[bi], bmat, preferred_element_type=F32)
        s_raw = jnp.dot(k_h, q_h, preferred_element_type=F32)
        inter = jnp.dot(ct.astype(BF16), q_h, preferred_element_type=F32)
        work.append((r, b, g, g_last, m_prev, k_h, ct, z, s_raw, inter))

    outs = []
    for (bi, h), (r, b, g, g_last, m_prev, k_h, ct, z, s_raw, inter) in zip(chains, work):
        sc = bi * ML_HEADS + h
        d = jnp.exp(jnp.where(visible, z, -jnp.inf))
        p = (s_raw * d).astype(BF16)
        vt = mvt_ref[bi, h * ML_VROWS:(h + 1) * ML_VROWS, :]
        num = jnp.dot(vt, p, preferred_element_type=F32) + jnp.exp(m_prev - g) * inter
        den = num[dv:dv + 1, :]
        hv = num[:dv, :] / jnp.maximum(jnp.abs(den), jnp.exp(-(b + g)))

        wv = (vt.astype(F32) * jnp.exp(r - g_last)).astype(BF16)
        c_sc[sc] = jnp.exp(m_prev - g_last) * ct + jnp.dot(wv, k_h, preferred_element_type=F32)
        m_sc[sc] = jnp.broadcast_to(b[:, L - 1:L] + g_last, (SUBLANES, LANES))

        hv = hv * ogt_ref[bi, h * dv:(h + 1) * dv, :].astype(F32)
        mu = jnp.mean(hv, axis=0, keepdims=True)
        dd = hv - mu
        var = jnp.mean(dd * dd, axis=0, keepdims=True)
        hn = dd * lax.rsqrt(var + LN_EPS) * ng_ref[h * dv:(h + 1) * dv, :]
        outs.append(hn * zgt_ref[bi, h * dv:(h + 1) * dv, :].astype(F32))
        if h == ML_HEADS - 1:
            y_ref[bi] = jnp.concatenate(outs, axis=0).T.astype(BF16)
            outs = []


def _mlstm(kc, qct, mvt, ogt, zgt, maug, mrow, norm_g_b):
    b, s, _ = kc.shape
    L = ML_CHUNK

    def col(n):
        return pl.BlockSpec((ML_NB, n, L), lambda i, j: (i, 0, j))

    def row(n):
        return pl.BlockSpec((ML_NB, L, n), lambda i, j: (i, j, 0))

    return pl.pallas_call(
        _mlstm_kernel,
        grid=(b // ML_NB, s // L),
        in_specs=[row(ML_PW), col(ML_WIDTH), col(ML_HEADS * ML_VROWS), col(ML_WIDTH), col(ML_WIDTH),
                  row(LANES), col(3 * SUBLANES),
                  pl.BlockSpec((ML_WIDTH, L), lambda i, j: (0, 0))],
        out_specs=row(ML_WIDTH),
        out_shape=jax.ShapeDtypeStruct((b, s, ML_WIDTH), BF16),
        scratch_shapes=[pltpu.VMEM((ML_NB * ML_HEADS, ML_VROWS, LANES), F32),
                        pltpu.VMEM((ML_NB * ML_HEADS, SUBLANES, LANES), F32)],
        compiler_params=pltpu.CompilerParams(
            dimension_semantics=("arbitrary", "arbitrary"), vmem_limit_bytes=48 * 1024 * 1024),
        name="mlstm",
    )(kc, qct, mvt, ogt, zgt, maug, mrow, norm_g_b)


def _memkv_kernel(m_ref, w_ref, k4_ref, vt_ref):
    w = w_ref[...].astype(BF16)
    for bi in range(m_ref.shape[0]):
        kv = jnp.dot(m_ref[bi].astype(BF16), w, preferred_element_type=F32)
        k = kv[:, :MEM_WIDTH] * (MEM_HEAD_DIM ** -0.5)
        lane = lax.broadcasted_iota(jnp.int32, k.shape, 1)
        for h in range(MEM_HEADS):
            own = (lane >= MEM_HEAD_DIM * h) & (lane < MEM_HEAD_DIM * (h + 1))
            k4_ref[bi, h] = jnp.where(own, k, 0.0).astype(BF16)
        vt_ref[bi] = _with_ones_rows(kv[:, MEM_WIDTH:].T.astype(BF16), MEM_HEAD_DIM, MEM_HEADS)


def _memkv(mem, w_mem_kv):
    b, ml, _ = mem.shape
    return pl.pallas_call(
        _memkv_kernel,
        grid=(DEPTH,),
        in_specs=[pl.BlockSpec((b, ml, D_MODEL), lambda l: (0, 0, 0)),
                  pl.BlockSpec((None, D_MODEL, 2 * MEM_WIDTH), lambda l: (l, 0, 0))],
        out_specs=[pl.BlockSpec((None, b, MEM_HEADS, ml, MEM_WIDTH), lambda l: (l, 0, 0, 0, 0)),
                   pl.BlockSpec((None, b, MEM_HEADS * MEM_VROWS, ml), lambda l: (l, 0, 0, 0))],
        out_shape=[jax.ShapeDtypeStruct((DEPTH, b, MEM_HEADS, ml, MEM_WIDTH), BF16),
                   jax.ShapeDtypeStruct((DEPTH, b, MEM_HEADS * MEM_VROWS, ml), BF16)],
        compiler_params=pltpu.CompilerParams(dimension_semantics=("arbitrary",)),
        name="mem_kv",
    )(mem, w_mem_kv)


def _out_kernel(x_ref, yf_ref, ym_ref, rqt_ref, rzg_ref, k4_ref, vt_ref, wo_ref, g_ref, b_ref,
                o_ref, ob_ref):
    hd = MEM_HEAD_DIM
    o_mem = FOX_WIDTH + ML_WIDTH
    n_sub = OUT_TM // OUT_SUB
    ahead = 2

    def rows(q):
        return slice(q * OUT_SUB, (q + 1) * OUT_SUB)

    def main_dot(q):
        y_main = jnp.concatenate([yf_ref[rows(q), :], ym_ref[rows(q), :]], axis=1)
        return jnp.dot(y_main, wo_ref[0:o_mem, :], preferred_element_type=F32)

    rqt = rqt_ref[...]
    sts = [jnp.dot(k4_ref[h], rqt, preferred_element_type=F32) for h in range(MEM_HEADS)]
    mains = [main_dot(q) for q in range(ahead)]

    outs = []
    for h in range(MEM_HEADS):
        st = sts[h]
        p = jnp.exp(st - jnp.max(st, axis=0, keepdims=True)).astype(BF16)
        acc = jnp.dot(vt_ref[h * MEM_VROWS:(h + 1) * MEM_VROWS, :], p,
                      preferred_element_type=F32)
        outs.append(acc[:hd, :] / acc[hd:hd + 1, :])
    att = jnp.concatenate(outs, axis=0).T
    y_mem = (att * rzg_ref[...].astype(F32)).astype(BF16)

    for q in range(n_sub):
        if q + ahead < n_sub:
            mains.append(main_dot(q + ahead))
        y = mains[q] + jnp.dot(y_mem[rows(q), :], wo_ref[o_mem:, :], preferred_element_type=F32)
        r = DEEPNORM_ALPHA * x_ref[rows(q), :] + y
        mu = jnp.mean(r, axis=1, keepdims=True)
        d = r - mu
        var = jnp.mean(d * d, axis=1, keepdims=True)
        out = d * lax.rsqrt(var + LN_EPS) * g_ref[...] + b_ref[...]
        o_ref[rows(q), :] = out
        ob_ref[rows(q), :] = out.astype(BF16)


def _out_proj(x, y_fox, y_ml, rqt, rzg, mem_k4, mem_vt, layer, w_out, ln_g, ln_b):
    b, s, _ = x.shape
    tm = OUT_TM
    ml = mem_k4.shape[3]
    return pl.pallas_call(
        _out_kernel,
        grid=(b, s // tm),
        in_specs=[
            pl.BlockSpec((None, tm, D_MODEL), lambda i, j: (i, j, 0)),
            pl.BlockSpec((None, tm, FOX_WIDTH), lambda i, j: (i, j, 0)),
            pl.BlockSpec((None, tm, ML_WIDTH), lambda i, j: (i, j, 0)),
            pl.BlockSpec((None, MEM_WIDTH, tm), lambda i, j: (i, 0, j)),
            pl.BlockSpec((None, tm, MEM_WIDTH), lambda i, j: (i, j, 0)),
            pl.BlockSpec((None, None, MEM_HEADS, ml, MEM_WIDTH), lambda i, j: (layer, i, 0, 0, 0)),
            pl.BlockSpec((None, None, MEM_HEADS * MEM_VROWS, ml), lambda i, j: (layer, i, 0, 0)),
            pl.BlockSpec((None, D_MODEL, D_MODEL), lambda i, j: (layer, 0, 0)),
            pl.BlockSpec((None, 1, D_MODEL), lambda i, j: (layer, 0, 0)),
            pl.BlockSpec((None, 1, D_MODEL), lambda i, j: (layer, 0, 0)),
        ],
        out_specs=[pl.BlockSpec((None, tm, D_MODEL), lambda i, j: (i, j, 0)),
                   pl.BlockSpec((None, tm, D_MODEL), lambda i, j: (i, j, 0))],
        out_shape=[jax.ShapeDtypeStruct((b, s, D_MODEL), F32),
                   jax.ShapeDtypeStruct((b, s, D_MODEL), BF16)],
        compiler_params=pltpu.CompilerParams(
            dimension_semantics=("arbitrary", "arbitrary"), vmem_limit_bytes=48 * 1024 * 1024),
        name="out_proj_ln",
    )(x, y_fox, y_ml, rqt, rzg, mem_k4, mem_vt, w_out, ln_g, ln_b)


def kernel(x, mem, w_in, fox_f_bias, mlstm_conv_w, mlstm_conv_b, mlstm_i_bias, mlstm_f_bias,
           mlstm_norm_g, w_mem_kv, w_out, ln_g, ln_b):
    w_re = _relayout_w_in(w_in)
    w_out_b = w_out.astype(BF16)
    bias_rows = _spread_gates(fox_f_bias, mlstm_i_bias, mlstm_f_bias)[:, None, :]
    conv_q = (mlstm_conv_w[..., :ML_WIDTH], mlstm_conv_b[:, None, :ML_WIDTH])
    conv_k = (_pad_heads_last(mlstm_conv_w[..., ML_WIDTH:]),
              _pad_heads_last(mlstm_conv_b[..., ML_WIDTH:])[:, None, :])
    norm_g_b = jnp.broadcast_to(mlstm_norm_g[:, :, None], (DEPTH, ML_WIDTH, ML_CHUNK))
    ln_g3 = ln_g[:, None, :]
    ln_b3 = ln_b[:, None, :]
    sels = _bias_selectors()

    mem_k4, mem_vt = _memkv(mem, w_mem_kv)
    x_mxu = x
    for l in range(DEPTH):
        kc, qct, mvt, ogt, zgt, ufx, rqt, rzg, g, fqt, fvt = _project(x_mxu, w_re, conv_q, conv_k, l)
        kb, qbt, maug, mrow = _gates(g, bias_rows[l], sels)
        y_fox = _fox(fqt, qbt, ufx, kb, fvt)
        y_ml = _mlstm(kc, qct, mvt, ogt, zgt, maug, mrow, norm_g_b[l])
        x, x_mxu = _out_proj(x, y_fox, y_ml, rqt, rzg, mem_k4, mem_vt, l, w_out_b, ln_g3, ln_b3)
    return x
```

```python
import math

import numpy as np
import jax
import jax.numpy as jnp
from jax import lax
from jax.experimental import pallas as pl
from jax.experimental.pallas import tpu as pltpu

F32 = jnp.float32
BF16 = jnp.bfloat16

D_MODEL = 1024
DEPTH = 4
FOX_HEADS = 6
FOX_HEAD_DIM = 64
FOX_WIDTH = FOX_HEADS * FOX_HEAD_DIM
ML_HEADS = 4
ML_HEAD_DIM = 96
ML_WIDTH = ML_HEADS * ML_HEAD_DIM
MEM_HEADS = 4
MEM_HEAD_DIM = 64
MEM_WIDTH = MEM_HEADS * MEM_HEAD_DIM
CONV_WIDTH = 4
LN_EPS = 1e-5
DEEPNORM_ALPHA = (2.0 * DEPTH) ** 0.25
LOG2E = math.log2(math.e)

LANES = 128
SUBLANES = 8
BF16_ROWS = 16
ML_PAD = LANES
ML_PW = ML_HEADS * ML_PAD
ML_VROWS = ML_HEAD_DIM + BF16_ROWS
FOX_VROWS = FOX_HEAD_DIM + BF16_ROWS
MEM_VROWS = MEM_HEAD_DIM + BF16_ROWS

COL_QK = 0
COL_MV = COL_QK + 2 * ML_PW
COL_MZ = COL_MV + 2 * ML_WIDTH
COL_FK = COL_MZ + ML_WIDTH + FOX_WIDTH
COL_FV = COL_FK + 2 * FOX_WIDTH
COL_RQ = COL_FV + FOX_WIDTH + LANES
W_COLS = COL_RQ + 2 * MEM_WIDTH
G_COLS = LANES
GATE_I = 48
GATE_F = 56
N_PIECES = 3
ML_BIAS_ROWS = 2 * BF16_ROWS

ML_CHUNK = 256
ML_NB = 4
GATE_CHUNKS = 2
FOX_TQ = 512
FOX_TK = 256
PROJ_TM = 512
OUT_TM = 1024
OUT_SUB = 256

_SRC = {}
_off = 0
for _name, _w in (("fq", FOX_WIDTH), ("fk", FOX_WIDTH), ("fv", FOX_WIDTH), ("ff", FOX_HEADS),
                  ("fz", FOX_WIDTH), ("mq", ML_WIDTH), ("mk", ML_WIDTH), ("mv", ML_WIDTH),
                  ("mi", ML_HEADS), ("mf", ML_HEADS), ("mo", ML_WIDTH), ("mz", ML_WIDTH),
                  ("rq", MEM_WIDTH), ("rz", MEM_WIDTH)):
    _SRC[_name] = (_off, _w)
    _off += _w


def _sigmoid(x):
    return 1.0 / (1.0 + jnp.exp2(x * (-LOG2E)))


def _silu(x):
    return x * _sigmoid(x)


def _split3(x):
    hi = x.astype(BF16)
    r1 = x - hi.astype(F32)
    mid = r1.astype(BF16)
    lo = (r1 - mid.astype(F32)).astype(BF16)
    return hi, mid, lo


def _with_ones_rows(t, head_dim, heads):
    ones = jnp.ones((BF16_ROWS, t.shape[1]), t.dtype)
    rows = []
    for h in range(heads):
        rows += [t[h * head_dim:(h + 1) * head_dim, :], ones]
    return jnp.concatenate(rows, axis=0)


def _zeros_like_cols(t, n):
    return jnp.zeros(t.shape[:-1] + (n,), t.dtype)


def _pad_heads_last(t):
    parts = []
    for h in range(ML_HEADS):
        parts.append(t[..., h * ML_HEAD_DIM:(h + 1) * ML_HEAD_DIM])
        parts.append(_zeros_like_cols(t, ML_PAD - ML_HEAD_DIM))
    return jnp.concatenate(parts, axis=-1)


def _spread_gates(ff, mi, mf):
    parts = []
    for h in range(FOX_HEADS):
        parts += [ff[..., h:h + 1], _zeros_like_cols(ff, SUBLANES - 1)]
    parts += [mi, _zeros_like_cols(mi, GATE_F - GATE_I - ML_HEADS),
              mf, _zeros_like_cols(mf, G_COLS - GATE_F - ML_HEADS)]
    return jnp.concatenate(parts, axis=-1)


def _w_in_segments():
    segs = []

    def put(name, dst, scale=1.0):
        o, n = _SRC[name]
        segs.append((o, dst, n, scale))
        return dst + n

    def put_padded_heads(name, dst):
        o, _ = _SRC[name]
        for h in range(ML_HEADS):
            segs.append((o + h * ML_HEAD_DIM, dst + h * ML_PAD, ML_HEAD_DIM, 1.0))
        return dst + ML_PW

    c = put_padded_heads("mq", COL_QK)
    c = put_padded_heads("mk", c)
    for name in ("mv", "mo", "mz", "fz", "fk"):
        c = put(name, c)
    c = put("fq", c, FOX_HEAD_DIM ** -0.5 * LOG2E)
    c = put("fv", c)
    o_ff, o_mi, o_mf = _SRC["ff"][0], _SRC["mi"][0], _SRC["mf"][0]
    for h in range(FOX_HEADS):
        segs.append((o_ff + h, c + SUBLANES * h, 1, 1.0))
    segs.append((o_mi, c + GATE_I, ML_HEADS, 1.0))
    segs.append((o_mf, c + GATE_F, ML_HEADS, 1.0))
    c = put("rq", c + G_COLS)
    c = put("rz", c)
    assert c == W_COLS
    return segs


W_PREP_ROWS = 256


def _w_prep_kernel(w_ref, o_ref):
    o_ref[...] = jnp.zeros_like(o_ref)
    for src, dst, n, scale in _w_in_segments():
        seg = w_ref[:, src:src + n]
        if scale != 1.0:
            seg = seg * scale
        o_ref[:, dst:dst + n] = seg.astype(BF16)


def _relayout_w_in(w):
    depth, rows, cols = w.shape
    tr = W_PREP_ROWS
    return pl.pallas_call(
        _w_prep_kernel,
        grid=(depth, rows // tr),
        in_specs=[pl.BlockSpec((None, tr, cols), lambda l, i: (l, i, 0))],
        out_specs=pl.BlockSpec((None, tr, W_COLS), lambda l, i: (l, i, 0)),
        out_shape=jax.ShapeDtypeStruct((depth, rows, W_COLS), BF16),
        compiler_params=pltpu.CompilerParams(dimension_semantics=("arbitrary", "arbitrary")),
        name="w_in_prep",
    )(w)


def _fox_bias_base(h):
    return SUBLANES * (h // 2) + (FOX_HEAD_DIM if h % 2 == 0 else 0)


def _bias_selectors():
    n = N_PIECES * LANES
    selk = np.zeros((n, LANES), np.float32)
    selq = np.zeros((n, LANES), np.float32)
    selm = np.zeros((n, LANES), np.float32)
    onek = np.zeros((1, LANES), np.float32)
    oneq = np.zeros((1, LANES), np.float32)
    onem = np.zeros((1, LANES), np.float32)
    for h in range(FOX_HEADS):
        src = SUBLANES * h
        base = _fox_bias_base(h)
        for j in range(N_PIECES):
            selk[j * LANES + src, base + j] = -1.0
            selq[j * LANES + src, base + N_PIECES + j] = 1.0
            onek[0, base + N_PIECES + j] = 1.0
            oneq[0, base + j] = 1.0
    for h in range(ML_HEADS):
        base = SUBLANES * h
        for j in range(N_PIECES):
            selm[j * LANES + GATE_I + h, base + j] = 1.0
            onem[0, base + N_PIECES + j] = 1.0
    return (jnp.asarray(selk, BF16), jnp.asarray(selq, BF16), jnp.asarray(selm, BF16),
            jnp.asarray(onek), jnp.asarray(oneq), jnp.asarray(onem))


def _proj_kernel(x_ref, w_ref, cw_ref, cb_ref,
                 kc_ref, qct_ref, mvt_ref, ogt_ref, zgt_ref, ufx_ref, rqt_ref, rzg_ref, g_ref,
                 fqt_ref, fvt_ref, tail_sc):
    tm = PROJ_TM

    @pl.when(pl.program_id(1) == 0)
    def _():
        tail_sc[...] = jnp.zeros_like(tail_sc)

    xb = x_ref[...].astype(BF16)

    def mm(c0, n):
        return jnp.dot(xb, w_ref[:, c0:c0 + n], preferred_element_type=F32)

    u = mm(COL_QK, 2 * ML_PW)
    tail = tail_sc[...]
    w = cw_ref[...]
    row8 = lax.broadcasted_iota(jnp.int32, (SUBLANES, 2 * ML_PW), 0)
    y = u * w[CONV_WIDTH - 1:CONV_WIDTH, :] + cb_ref[...]
    for shift in range(1, CONV_WIDTH):
        r = pltpu.roll(u, shift, axis=0)
        rt = pltpu.roll(tail, shift, axis=0)
        first = jnp.where(row8 < shift, rt, r[:SUBLANES])
        xs = jnp.concatenate([first, r[SUBLANES:]], axis=0)
        y = y + xs * w[CONV_WIDTH - 1 - shift:CONV_WIDTH - shift, :]
    tail_sc[...] = u[tm - SUBLANES:, :]
    a = _silu(y)
    qct_ref[...] = a[:, :ML_PW].T.astype(BF16)
    kc_ref[...] = (a[:, ML_PW:] * (ML_HEAD_DIM ** -0.5)).astype(BF16)

    t = mm(COL_MV, 2 * ML_WIDTH)
    mvt_ref[...] = _with_ones_rows(t[:, :ML_WIDTH].T.astype(BF16), ML_HEAD_DIM, ML_HEADS)
    ogt_ref[...] = _sigmoid(t[:, ML_WIDTH:]).T.astype(BF16)

    t = _silu(mm(COL_MZ, ML_WIDTH + FOX_WIDTH))
    zgt_ref[...] = t[:, :ML_WIDTH].T.astype(BF16)
    ufx_ref[:, FOX_WIDTH:] = t[:, ML_WIDTH:].astype(BF16)

    t = mm(COL_FK, 2 * FOX_WIDTH)
    ufx_ref[:, :FOX_WIDTH] = t[:, :FOX_WIDTH].astype(BF16)
    fqt_ref[...] = t[:, FOX_WIDTH:].T.astype(BF16)

    t = mm(COL_FV, FOX_WIDTH + G_COLS)
    fvt_ref[...] = _with_ones_rows(t[:, :FOX_WIDTH].T.astype(BF16), FOX_HEAD_DIM, FOX_HEADS)
    g_ref[...] = t[:, FOX_WIDTH:]

    t = mm(COL_RQ, 2 * MEM_WIDTH)
    rqt_ref[...] = t[:, :MEM_WIDTH].T.astype(BF16)
    rzg_ref[...] = _silu(t[:, MEM_WIDTH:]).astype(BF16)


def _project(x, w_re, conv_w, conv_b, layer):
    b, s, _ = x.shape
    tm = PROJ_TM

    def row(n):
        return pl.BlockSpec((None, tm, n), lambda i, j: (i, j, 0))

    def col(n):
        return pl.BlockSpec((None, n, tm), lambda i, j: (i, 0, j))

    def per_layer(shape):
        return pl.BlockSpec((None,) + shape, lambda i, j: (layer, 0, 0))

    def rows_shape(n):
        return jax.ShapeDtypeStruct((b, s, n), BF16)

    def cols_shape(n):
        return jax.ShapeDtypeStruct((b, n, s), BF16)

    return pl.pallas_call(
        _proj_kernel,
        grid=(b, s // tm),
        in_specs=[row(D_MODEL), per_layer((D_MODEL, W_COLS)),
                  per_layer((CONV_WIDTH, 2 * ML_PW)), per_layer((1, 2 * ML_PW))],
        out_specs=[row(ML_PW), col(ML_PW), col(ML_HEADS * ML_VROWS), col(ML_WIDTH), col(ML_WIDTH),
                   row(2 * FOX_WIDTH), col(MEM_WIDTH), row(MEM_WIDTH), row(G_COLS),
                   col(FOX_WIDTH), col(FOX_HEADS * FOX_VROWS)],
        out_shape=[rows_shape(ML_PW), cols_shape(ML_PW), cols_shape(ML_HEADS * ML_VROWS),
                   cols_shape(ML_WIDTH), cols_shape(ML_WIDTH),
                   rows_shape(2 * FOX_WIDTH), cols_shape(MEM_WIDTH), rows_shape(MEM_WIDTH),
                   jax.ShapeDtypeStruct((b, s, G_COLS), F32),
                   cols_shape(FOX_WIDTH), cols_shape(FOX_HEADS * FOX_VROWS)],
        scratch_shapes=[pltpu.VMEM((SUBLANES, 2 * ML_PW), F32)],
        compiler_params=pltpu.CompilerParams(
            dimension_semantics=("arbitrary", "arbitrary"), vmem_limit_bytes=56 * 1024 * 1024),
        name="in_proj",
    )(x, w_re, conv_w, conv_b)


def _gates_kernel(g_ref, bias_ref, selk_ref, selq_ref, selm_ref, onek_ref, oneq_ref, onem_ref,
                  kb_ref, qbt_ref, maug_ref, mrow_ref, carry_ref):
    @pl.when(pl.program_id(0) == 0)
    def _():
        carry_ref[...] = jnp.zeros_like(carry_ref)

    L = ML_CHUNK
    r = lax.broadcasted_iota(jnp.int32, (L, L), 0)
    c = lax.broadcasted_iota(jnp.int32, (L, L), 1)
    tri = (c <= r).astype(BF16)
    lane = lax.broadcasted_iota(jnp.int32, (L, G_COLS), 1)
    pos = lax.broadcasted_iota(jnp.int32, (SUBLANES, L), 1)

    chains = [(bi, ci) for bi in range(g_ref.shape[0]) for ci in range(GATE_CHUNKS)]

    def span(ci):
        return slice(ci * L, (ci + 1) * L)

    xs, parts = [], []
    for bi, ci in chains:
        x = g_ref[bi, span(ci), :] + bias_ref[...]
        log_sig = jnp.minimum(x, 0.0) - jnp.log1p(jnp.exp(-jnp.abs(x)))
        xs.append(x)
        parts.append(jnp.dot(tri, jnp.concatenate(_split3(log_sig), axis=1),
                             preferred_element_type=F32))

    kbs, qbs, maugs, rowss = [], [], [], []
    carry = None
    for (bi, ci), x, p in zip(chains, xs, parts):
        if ci == 0:
            carry = carry_ref[bi, 0:1, :]
        local = (p[:, :LANES] + p[:, LANES:2 * LANES]) + p[:, 2 * LANES:]
        c_run = local + carry
        carry = carry + local[L - 1:L, :]
        if ci == GATE_CHUNKS - 1:
            carry_ref[bi, 0:1, :] = carry
        pieces = jnp.concatenate(_split3(c_run * LOG2E), axis=1)
        kbs.append(jnp.dot(pieces, selk_ref[...], preferred_element_type=F32))
        qbs.append(jnp.dot(pieces, selq_ref[...], preferred_element_type=F32))
        rc = x - pltpu.roll(local, LANES - (GATE_F - GATE_I), axis=1)
        pieces = jnp.concatenate(_split3(rc), axis=1)
        maugs.append(jnp.dot(pieces, selm_ref[...], preferred_element_type=F32))
        rowss.append(jnp.where(lane < GATE_F, rc, local))

    for (bi, ci), kb, qb, maug, rows in zip(chains, kbs, qbs, maugs, rowss):
        kb_ref[bi, span(ci), :] = (kb + onek_ref[...]).astype(BF16)
        qbt_ref[bi, :, span(ci)] = (qb + oneq_ref[...]).T.astype(BF16)
        maug_ref[bi, span(ci), :] = (maug + onem_ref[...]).astype(BF16)
        rows = rows.T[GATE_I:GATE_I + 2 * SUBLANES, :]
        cm = rows[:SUBLANES]
        shift = 1
        while shift < L:
            cm = jnp.maximum(cm, jnp.where(pos >= shift, pltpu.roll(cm, shift, axis=1), -jnp.inf))
            shift *= 2
        mrow_ref[bi, :, span(ci)] = jnp.concatenate([rows, cm], axis=0)


def _gates(g, bias_row, sels):
    b, s, _ = g.shape
    L = GATE_CHUNKS * ML_CHUNK
    const = lambda shape: pl.BlockSpec(shape, lambda j: (0, 0))
    sel_shape = (N_PIECES * LANES, LANES)
    return pl.pallas_call(
        _gates_kernel,
        grid=(s // L,),
        in_specs=[pl.BlockSpec((b, L, G_COLS), lambda j: (0, j, 0)),
                  const((1, G_COLS)),
                  const(sel_shape), const(sel_shape), const(sel_shape),
                  const((1, LANES)), const((1, LANES)), const((1, LANES))],
        out_specs=[pl.BlockSpec((b, L, LANES), lambda j: (0, j, 0)),
                   pl.BlockSpec((b, LANES, L), lambda j: (0, 0, j)),
                   pl.BlockSpec((b, L, LANES), lambda j: (0, j, 0)),
                   pl.BlockSpec((b, 3 * SUBLANES, L), lambda j: (0, 0, j))],
        out_shape=[jax.ShapeDtypeStruct((b, s, LANES), BF16),
                   jax.ShapeDtypeStruct((b, LANES, s), BF16),
                   jax.ShapeDtypeStruct((b, s, LANES), BF16),
                   jax.ShapeDtypeStruct((b, 3 * SUBLANES, s), F32)],
        scratch_shapes=[pltpu.VMEM((b, SUBLANES, G_COLS), F32)],
        compiler_params=pltpu.CompilerParams(dimension_semantics=("arbitrary",)),
        name="gate_prefix",
    )(g, bias_row, *sels)


def _fox_kernel(qt_ref, qbt_ref, k_ref, kb_ref, vt_ref, zg_ref, o_ref,
                qa_sc, kf_sc, sa_sc, sb_sc, mxa_sc, mxb_sc, m_sc, acc_sc):
    qi = pl.program_id(1)
    tq, tk = FOX_TQ, FOX_TK
    hd = FOX_HEAD_DIM

    brow = lax.broadcasted_iota(jnp.int32, (LANES, tq), 0)
    qbt = qbt_ref[...]
    for h in range(FOX_HEADS):
        base = _fox_bias_base(h)
        own = (brow >= base) & (brow < base + SUBLANES)
        bias = jnp.where(own, qbt, jnp.zeros_like(qbt))
        qh = qt_ref[h * hd:(h + 1) * hd, :]
        if h % 2 == 0:
            qa_sc[h] = jnp.concatenate([qh, bias[hd:, :]], axis=0)
        else:
            qa_sc[h] = jnp.concatenate([bias[:hd, :], qh], axis=0)

    @pl.when(qi == 0)
    def _():
        klane = lax.broadcasted_iota(jnp.int32, (kb_ref.shape[0], LANES), 1)
        kbias = kb_ref[...]
        for h in range(FOX_HEADS):
            p = h // 2
            own = (klane < hd) if h % 2 == 0 else (klane >= hd)
            kf_sc[:, h * LANES:(h + 1) * LANES] = jnp.where(
                own, k_ref[:, p * LANES:(p + 1) * LANES], kbias)

    m_sc[...] = jnp.full_like(m_sc, -jnp.inf)
    acc_sc[...] = jnp.zeros_like(acc_sc)

    def logits(j, h, dst, mxd, key_offset, q0=0):
        k0 = pl.multiple_of(j * tk, tk)
        kk = kf_sc[pl.ds(k0, tk), h * LANES:(h + 1) * LANES]
        st = jnp.dot(kk, qa_sc[h, :, q0:], preferred_element_type=F32)
        if key_offset is not None:
            kr = lax.broadcasted_iota(jnp.int32, st.shape, 0)
            qc = lax.broadcasted_iota(jnp.int32, st.shape, 1)
            st = jnp.where(kr + (key_offset - q0) <= qc, st, -jnp.inf)
        dst[h, :, q0:] = st
        mxd[h, :, q0:] = jnp.max(st, axis=0, keepdims=True)

    def softmax_pv(j, h, src, mxs, q0=0):
        k0 = pl.multiple_of(j * tk, tk)
        m_old = m_sc[h, :, q0:]
        m_new = jnp.maximum(m_old, mxs[h, :, q0:])
        alpha = jnp.exp2(m_old - m_new)
        pt = jnp.exp2(src[h, :, q0:] - m_new).astype(BF16)
        vt = vt_ref[h * FOX_VROWS:(h + 1) * FOX_VROWS, pl.ds(k0, tk)]
        acc_sc[h, :, q0:] = alpha * acc_sc[h, :, q0:] + jnp.dot(vt, pt, preferred_element_type=F32)
        m_sc[h, :, q0:] = m_new

    buf_a = (sa_sc, mxa_sc)
    buf_b = (sb_sc, mxb_sc)

    def stage(j_qk, dst, key_offset, j_sm, src, q0_qk=0, q0_sm=0):
        if j_qk is not None:
            logits(j_qk, 0, *dst, key_offset, q0_qk)
        for h in range(FOX_HEADS):
            if j_qk is not None and h + 1 < FOX_HEADS:
                logits(j_qk, h + 1, *dst, key_offset, q0_qk)
            if j_sm is not None:
                softmax_pv(j_sm, h, *src, q0_sm)

    stage(2 * qi, buf_a, 0, None, None)
    stage(2 * qi + 1, buf_b, tk, 2 * qi, buf_a, q0_qk=tk)
    stage(0, buf_a, None, 2 * qi + 1, buf_b, q0_sm=tk)

    def body(i, carry):
        stage(2 * i + 1, buf_b, None, 2 * i, buf_a)
        stage(2 * i + 2, buf_a, None, 2 * i + 1, buf_b)
        return carry

    lax.fori_loop(0, qi, body, 0)

    out_t = jnp.concatenate(
        [acc_sc[h, :hd, :] / acc_sc[h, hd:hd + 1, :] for h in range(FOX_HEADS)], axis=0)
    o_ref[...] = (out_t.T * zg_ref[...].astype(F32)).astype(BF16)


def _fox(qt, qbt, ufx, kb, vt):
    b, s, _ = ufx.shape
    tq = FOX_TQ
    return pl.pallas_call(
        _fox_kernel,
        grid=(b, s // tq),
        in_specs=[
            pl.BlockSpec((None, FOX_WIDTH, tq), lambda i, j: (i, 0, j)),
            pl.BlockSpec((None, LANES, tq), lambda i, j: (i, 0, j)),
            pl.BlockSpec((None, s, FOX_WIDTH), lambda i, j: (i, 0, 0)),
            pl.BlockSpec((None, s, LANES), lambda i, j: (i, 0, 0)),
            pl.BlockSpec((None, FOX_HEADS * FOX_VROWS, s), lambda i, j: (i, 0, 0)),
            pl.BlockSpec((None, tq, FOX_WIDTH), lambda i, j: (i, j, 1)),
        ],
        out_specs=pl.BlockSpec((None, tq, FOX_WIDTH), lambda i, j: (i, j, 0)),
        out_shape=jax.ShapeDtypeStruct((b, s, FOX_WIDTH), BF16),
        scratch_shapes=[pltpu.VMEM((FOX_HEADS, LANES, tq), BF16),
                        pltpu.VMEM((s, FOX_HEADS * LANES), BF16),
                        pltpu.VMEM((FOX_HEADS, FOX_TK, tq), F32),
                        pltpu.VMEM((FOX_HEADS, FOX_TK, tq), F32),
                        pltpu.VMEM((FOX_HEADS, 1, tq), F32),
                        pltpu.VMEM((FOX_HEADS, 1, tq), F32),
                        pltpu.VMEM((FOX_HEADS, 1, tq), F32),
                        pltpu.VMEM((FOX_HEADS, FOX_VROWS, tq), F32)],
        compiler_params=pltpu.CompilerParams(
            dimension_semantics=("arbitrary", "arbitrary"), vmem_limit_bytes=48 * 1024 * 1024),
        name="fox_attn",
    )(qt, qbt, ufx, kb, vt, ufx)


def _mlstm_kernel(kc_ref, qct_ref, mvt_ref, ogt_ref, zgt_ref, maug_ref, mrow_ref, ng_ref,
                  y_ref, c_sc, m_sc):
    L = ML_CHUNK
    dv = ML_HEAD_DIM

    @pl.when(pl.program_id(1) == 0)
    def _():
        c_sc[...] = jnp.zeros_like(c_sc)
        m_sc[...] = jnp.zeros_like(m_sc)

    kr = lax.broadcasted_iota(jnp.int32, (L, L), 0)
    qc = lax.broadcasted_iota(jnp.int32, (L, L), 1)
    visible = kr <= qc
    brow = lax.broadcasted_iota(jnp.int32, (ML_BIAS_ROWS, L), 0)
    bzero = jnp.zeros((LANES - ML_BIAS_ROWS, L), BF16)
    chains = [(bi, h) for bi in range(ML_NB) for h in range(ML_HEADS)]

    work = []
    for bi, h in chains:
        sc = bi * ML_HEADS + h
        r = mrow_ref[bi, h:h + 1, :]
        b = mrow_ref[bi, SUBLANES + h:SUBLANES + h + 1, :]
        cm = mrow_ref[bi, 2 * SUBLANES + h:2 * SUBLANES + h + 1, :]
        m_prev = m_sc[sc, 0:1, 0:1]
        g = jnp.maximum(m_prev, cm)
        g_last = g[:, L - 1:L]
        g_hi, g_mid, g_lo = (t.astype(F32) for t in _split3(g))
        rel = brow - SUBLANES * h
        blk = jnp.where(rel == N_PIECES, -g_hi,
                        jnp.where(rel == N_PIECES + 1, -g_mid,
                                  jnp.where(rel == N_PIECES + 2, -g_lo,
                                            jnp.where((rel >= 0) & (rel < N_PIECES), 1.0, 0.0))))
        bmat = jnp.concatenate([blk.astype(BF16), bzero], axis=0)
        k_h = kc_ref[bi, :, h * ML_PAD:(h + 1) * ML_PAD]
        q_h = qct_ref[bi, h * ML_PAD:(h + 1) * ML_PAD, :]
        ct = c_sc[sc]
        z = jnp.dot(maug_ref[bi], bmat, preferred_element_type=F32)
        s_raw = jnp.dot(k_h, q_h, preferred_element_type=F32)
        inter = jnp.dot(ct.astype(BF16), q_h, preferred_element_type=F32)
        work.append((r, b, g, g_last, m_prev, k_h, ct, z, s_raw, inter))

    outs = []
    for (bi, h), (r, b, g, g_last, m_prev, k_h, ct, z, s_raw, inter) in zip(chains, work):
        sc = bi * ML_HEADS + h
        d = jnp.exp(jnp.where(visible, z, -jnp.inf))
        p = (s_raw * d).astype(BF16)
        vt = mvt_ref[bi, h * ML_VROWS:(h + 1) * ML_VROWS, :]
        num = jnp.dot(vt, p, preferred_element_type=F32) + jnp.exp(m_prev - g) * inter
        den = num[dv:dv + 1, :]
        hv = num[:dv, :] / jnp.maximum(jnp.abs(den), jnp.exp(-(b + g)))

        wv = (vt.astype(F32) * jnp.exp(r - g_last)).astype(BF16)
        c_sc[sc] = jnp.exp(m_prev - g_last) * ct + jnp.dot(wv, k_h, preferred_element_type=F32)
        m_sc[sc] = jnp.broadcast_to(b[:, L - 1:L] + g_last, (SUBLANES, LANES))

        hv = hv * ogt_ref[bi, h * dv:(h + 1) * dv, :].astype(F32)
        mu = jnp.mean(hv, axis=0, keepdims=True)
        dd = hv - mu
        var = jnp.mean(dd * dd, axis=0, keepdims=True)
        hn = dd * lax.rsqrt(var + LN_EPS) * ng_ref[h * dv:(h + 1) * dv, :]
        outs.append(hn * zgt_ref[bi, h * dv:(h + 1) * dv, :].astype(F32))
        if h == ML_HEADS - 1:
            y_ref[bi] = jnp.concatenate(outs, axis=0).T.astype(BF16)
            outs = []


def _mlstm(kc, qct, mvt, ogt, zgt, maug, mrow, norm_g_b):
    b, s, _ = kc.shape
    L = ML_CHUNK

    def col(n):
        return pl.BlockSpec((ML_NB, n, L), lambda i, j: (i, 0, j))

    def row(n):
        return pl.BlockSpec((ML_NB, L, n), lambda i, j: (i, j, 0))

    return pl.pallas_call(
        _mlstm_kernel,
        grid=(b // ML_NB, s // L),
        in_specs=[row(ML_PW), col(ML_PW), col(ML_HEADS * ML_VROWS), col(ML_WIDTH), col(ML_WIDTH),
                  row(LANES), col(3 * SUBLANES),
                  pl.BlockSpec((ML_WIDTH, L), lambda i, j: (0, 0))],
        out_specs=row(ML_WIDTH),
        out_shape=jax.ShapeDtypeStruct((b, s, ML_WIDTH), BF16),
        scratch_shapes=[pltpu.VMEM((ML_NB * ML_HEADS, ML_VROWS, LANES), F32),
                        pltpu.VMEM((ML_NB * ML_HEADS, SUBLANES, LANES), F32)],
        compiler_params=pltpu.CompilerParams(
            dimension_semantics=("arbitrary", "arbitrary"), vmem_limit_bytes=48 * 1024 * 1024),
        name="mlstm",
    )(kc, qct, mvt, ogt, zgt, maug, mrow, norm_g_b)


def _memkv_kernel(m_ref, w_ref, k4_ref, vt_ref):
    w = w_ref[...].astype(BF16)
    for bi in range(m_ref.shape[0]):
        kv = jnp.dot(m_ref[bi].astype(BF16), w, preferred_element_type=F32)
        k = kv[:, :MEM_WIDTH] * (MEM_HEAD_DIM ** -0.5)
        lane = lax.broadcasted_iota(jnp.int32, k.shape, 1)
        for h in range(MEM_HEADS):
            own = (lane >= MEM_HEAD_DIM * h) & (lane < MEM_HEAD_DIM * (h + 1))
            k4_ref[bi, h] = jnp.where(own, k, 0.0).astype(BF16)
        vt_ref[bi] = _with_ones_rows(kv[:, MEM_WIDTH:].T.astype(BF16), MEM_HEAD_DIM, MEM_HEADS)


def _memkv(mem, w_mem_kv):
    b, ml, _ = mem.shape
    return pl.pallas_call(
        _memkv_kernel,
        grid=(DEPTH,),
        in_specs=[pl.BlockSpec((b, ml, D_MODEL), lambda l: (0, 0, 0)),
                  pl.BlockSpec((None, D_MODEL, 2 * MEM_WIDTH), lambda l: (l, 0, 0))],
        out_specs=[pl.BlockSpec((None, b, MEM_HEADS, ml, MEM_WIDTH), lambda l: (l, 0, 0, 0, 0)),
                   pl.BlockSpec((None, b, MEM_HEADS * MEM_VROWS, ml), lambda l: (l, 0, 0, 0))],
        out_shape=[jax.ShapeDtypeStruct((DEPTH, b, MEM_HEADS, ml, MEM_WIDTH), BF16),
                   jax.ShapeDtypeStruct((DEPTH, b, MEM_HEADS * MEM_VROWS, ml), BF16)],
        compiler_params=pltpu.CompilerParams(dimension_semantics=("arbitrary",)),
        name="mem_kv",
    )(mem, w_mem_kv)


def _out_kernel(x_ref, yf_ref, ym_ref, rqt_ref, rzg_ref, k4_ref, vt_ref, wo_ref, g_ref, b_ref,
                o_ref, ob_ref):
    hd = MEM_HEAD_DIM
    o_mem = FOX_WIDTH + ML_WIDTH
    n_sub = OUT_TM // OUT_SUB
    ahead = 2

    def rows(q):
        return slice(q * OUT_SUB, (q + 1) * OUT_SUB)

    def main_dot(q):
        y_main = jnp.concatenate([yf_ref[rows(q), :], ym_ref[rows(q), :]], axis=1)
        return jnp.dot(y_main, wo_ref[0:o_mem, :], preferred_element_type=F32)

    rqt = rqt_ref[...]
    sts = [jnp.dot(k4_ref[h], rqt, preferred_element_type=F32) for h in range(MEM_HEADS)]
    mains = [main_dot(q) for q in range(ahead)]

    outs = []
    for h in range(MEM_HEADS):
        st = sts[h]
        p = jnp.exp(st - jnp.max(st, axis=0, keepdims=True)).astype(BF16)
        acc = jnp.dot(vt_ref[h * MEM_VROWS:(h + 1) * MEM_VROWS, :], p,
                      preferred_element_type=F32)
        outs.append(acc[:hd, :] / acc[hd:hd + 1, :])
    att = jnp.concatenate(outs, axis=0).T
    y_mem = (att * rzg_ref[...].astype(F32)).astype(BF16)

    for q in range(n_sub):
        if q + ahead < n_sub:
            mains.append(main_dot(q + ahead))
        y = mains[q] + jnp.dot(y_mem[rows(q), :], wo_ref[o_mem:, :], preferred_element_type=F32)
        r = DEEPNORM_ALPHA * x_ref[rows(q), :] + y
        mu = jnp.mean(r, axis=1, keepdims=True)
        d = r - mu
        var = jnp.mean(d * d, axis=1, keepdims=True)
        out = d * lax.rsqrt(var + LN_EPS) * g_ref[...] + b_ref[...]
        o_ref[rows(q), :] = out
        ob_ref[rows(q), :] = out.astype(BF16)


def _out_proj(x, y_fox, y_ml, rqt, rzg, mem_k4, mem_vt, layer, w_out, ln_g, ln_b):
    b, s, _ = x.shape
    tm = OUT_TM
    ml = mem_k4.shape[3]
    return pl.pallas_call(
        _out_kernel,
        grid=(b, s // tm),
        in_specs=[
            pl.BlockSpec((None, tm, D_MODEL), lambda i, j: (i, j, 0)),
            pl.BlockSpec((None, tm, FOX_WIDTH), lambda i, j: (i, j, 0)),
            pl.BlockSpec((None, tm, ML_WIDTH), lambda i, j: (i, j, 0)),
            pl.BlockSpec((None, MEM_WIDTH, tm), lambda i, j: (i, 0, j)),
            pl.BlockSpec((None, tm, MEM_WIDTH), lambda i, j: (i, j, 0)),
            pl.BlockSpec((None, None, MEM_HEADS, ml, MEM_WIDTH), lambda i, j: (layer, i, 0, 0, 0)),
            pl.BlockSpec((None, None, MEM_HEADS * MEM_VROWS, ml), lambda i, j: (layer, i, 0, 0)),
            pl.BlockSpec((None, D_MODEL, D_MODEL), lambda i, j: (layer, 0, 0)),
            pl.BlockSpec((None, 1, D_MODEL), lambda i, j: (layer, 0, 0)),
            pl.BlockSpec((None, 1, D_MODEL), lambda i, j: (layer, 0, 0)),
        ],
        out_specs=[pl.BlockSpec((None, tm, D_MODEL), lambda i, j: (i, j, 0)),
                   pl.BlockSpec((None, tm, D_MODEL), lambda i, j: (i, j, 0))],
        out_shape=[jax.ShapeDtypeStruct((b, s, D_MODEL), F32),
                   jax.ShapeDtypeStruct((b, s, D_MODEL), BF16)],
        compiler_params=pltpu.CompilerParams(
            dimension_semantics=("arbitrary", "arbitrary"), vmem_limit_bytes=48 * 1024 * 1024),
        name="out_proj_ln",
    )(x, y_fox, y_ml, rqt, rzg, mem_k4, mem_vt, w_out, ln_g, ln_b)


def kernel(x, mem, w_in, fox_f_bias, mlstm_conv_w, mlstm_conv_b, mlstm_i_bias, mlstm_f_bias,
           mlstm_norm_g, w_mem_kv, w_out, ln_g, ln_b):
    w_re = _relayout_w_in(w_in)
    w_out_b = w_out.astype(BF16)
    bias_rows = _spread_gates(fox_f_bias, mlstm_i_bias, mlstm_f_bias)[:, None, :]
    conv_w = jnp.concatenate([_pad_heads_last(mlstm_conv_w[..., :ML_WIDTH]),
                              _pad_heads_last(mlstm_conv_w[..., ML_WIDTH:])], axis=-1)
    conv_b = jnp.concatenate([_pad_heads_last(mlstm_conv_b[..., :ML_WIDTH]),
                              _pad_heads_last(mlstm_conv_b[..., ML_WIDTH:])], axis=-1)[:, None, :]
    norm_g_b = jnp.broadcast_to(mlstm_norm_g[:, :, None], (DEPTH, ML_WIDTH, ML_CHUNK))
    ln_g3 = ln_g[:, None, :]
    ln_b3 = ln_b[:, None, :]
    sels = _bias_selectors()

    mem_k4, mem_vt = _memkv(mem, w_mem_kv)
    x_mxu = x
    for l in range(DEPTH):
        kc, qct, mvt, ogt, zgt, ufx, rqt, rzg, g, fqt, fvt = _project(x_mxu, w_re, conv_w, conv_b, l)
        kb, qbt, maug, mrow = _gates(g, bias_rows[l], sels)
        y_fox = _fox(fqt, qbt, ufx, kb, fvt)
        y_ml = _mlstm(kc, qct, mvt, ogt, zgt, maug, mrow, norm_g_b[l])
        x, x_mxu = _out_proj(x, y_fox, y_ml, rqt, rzg, mem_k4, mem_vt, l, w_out_b, ln_g3, ln_b3)
    return x
```

```python
import math

import numpy as np
import jax
import jax.numpy as jnp
from jax import lax
from jax.experimental import pallas as pl
from jax.experimental.pallas import tpu as pltpu

F32 = jnp.float32
BF16 = jnp.bfloat16

D_MODEL = 1024
DEPTH = 4
FOX_HEADS = 6
FOX_HEAD_DIM = 64
FOX_WIDTH = FOX_HEADS * FOX_HEAD_DIM
ML_HEADS = 4
ML_HEAD_DIM = 96
ML_WIDTH = ML_HEADS * ML_HEAD_DIM
MEM_HEADS = 4
MEM_HEAD_DIM = 64
MEM_WIDTH = MEM_HEADS * MEM_HEAD_DIM
CONV_WIDTH = 4
LN_EPS = 1e-5
DEEPNORM_ALPHA = (2.0 * DEPTH) ** 0.25
LOG2E = math.log2(math.e)

LANES = 128
SUBLANES = 8
BF16_ROWS = 16
ML_PAD = LANES
ML_PW = ML_HEADS * ML_PAD
ML_VROWS = ML_HEAD_DIM + BF16_ROWS
FOX_VROWS = FOX_HEAD_DIM + BF16_ROWS
MEM_VROWS = MEM_HEAD_DIM + BF16_ROWS

COL_QK = 0
COL_MV = COL_QK + 2 * ML_PW
COL_MZ = COL_MV + 2 * ML_WIDTH
COL_FK = COL_MZ + ML_WIDTH + FOX_WIDTH
COL_FV = COL_FK + 2 * FOX_WIDTH
COL_RQ = COL_FV + FOX_WIDTH + LANES
W_COLS = COL_RQ + 2 * MEM_WIDTH
G_COLS = LANES
GATE_I = 48
GATE_F = 56
N_PIECES = 3
ML_BIAS_ROWS = 2 * BF16_ROWS

ML_CHUNK = 256
ML_NB = 4
GATE_CHUNKS = 2
FOX_TQ = 512
FOX_TK = 256
PROJ_TM = 512
OUT_TM = 1024
OUT_SUB = 256

_SRC = {}
_off = 0
for _name, _w in (("fq", FOX_WIDTH), ("fk", FOX_WIDTH), ("fv", FOX_WIDTH), ("ff", FOX_HEADS),
                  ("fz", FOX_WIDTH), ("mq", ML_WIDTH), ("mk", ML_WIDTH), ("mv", ML_WIDTH),
                  ("mi", ML_HEADS), ("mf", ML_HEADS), ("mo", ML_WIDTH), ("mz", ML_WIDTH),
                  ("rq", MEM_WIDTH), ("rz", MEM_WIDTH)):
    _SRC[_name] = (_off, _w)
    _off += _w


def _sigmoid(x):
    return 1.0 / (1.0 + jnp.exp2(x * (-LOG2E)))


def _silu(x):
    return x * _sigmoid(x)


def _split3(x):
    hi = x.astype(BF16)
    r1 = x - hi.astype(F32)
    mid = r1.astype(BF16)
    lo = (r1 - mid.astype(F32)).astype(BF16)
    return hi, mid, lo


def _with_ones_rows(t, head_dim, heads):
    ones = jnp.ones((BF16_ROWS, t.shape[1]), t.dtype)
    rows = []
    for h in range(heads):
        rows += [t[h * head_dim:(h + 1) * head_dim, :], ones]
    return jnp.concatenate(rows, axis=0)


def _zeros_like_cols(t, n):
    return jnp.zeros(t.shape[:-1] + (n,), t.dtype)


def _pad_heads_last(t):
    parts = []
    for h in range(ML_HEADS):
        parts.append(t[..., h * ML_HEAD_DIM:(h + 1) * ML_HEAD_DIM])
        parts.append(_zeros_like_cols(t, ML_PAD - ML_HEAD_DIM))
    return jnp.concatenate(parts, axis=-1)


def _spread_gates(ff, mi, mf):
    parts = []
    for h in range(FOX_HEADS):
        parts += [ff[..., h:h + 1], _zeros_like_cols(ff, SUBLANES - 1)]
    parts += [mi, _zeros_like_cols(mi, GATE_F - GATE_I - ML_HEADS),
              mf, _zeros_like_cols(mf, G_COLS - GATE_F - ML_HEADS)]
    return jnp.concatenate(parts, axis=-1)


def _w_in_segments():
    segs = []

    def put(name, dst, scale=1.0):
        o, n = _SRC[name]
        segs.append((o, dst, n, scale))
        return dst + n

    def put_padded_heads(name, dst):
        o, _ = _SRC[name]
        for h in range(ML_HEADS):
            segs.append((o + h * ML_HEAD_DIM, dst + h * ML_PAD, ML_HEAD_DIM, 1.0))
        return dst + ML_PW

    c = put_padded_heads("mq", COL_QK)
    c = put_padded_heads("mk", c)
    for name in ("mv", "mo", "mz", "fz", "fk"):
        c = put(name, c)
    c = put("fq", c, FOX_HEAD_DIM ** -0.5 * LOG2E)
    c = put("fv", c)
    o_ff, o_mi, o_mf = _SRC["ff"][0], _SRC["mi"][0], _SRC["mf"][0]
    for h in range(FOX_HEADS):
        segs.append((o_ff + h, c + SUBLANES * h, 1, 1.0))
    segs.append((o_mi, c + GATE_I, ML_HEADS, 1.0))
    segs.append((o_mf, c + GATE_F, ML_HEADS, 1.0))
    c = put("rq", c + G_COLS)
    c = put("rz", c)
    assert c == W_COLS
    return segs


W_PREP_ROWS = 256


def _w_prep_kernel(w_ref, o_ref):
    o_ref[...] = jnp.zeros_like(o_ref)
    for src, dst, n, scale in _w_in_segments():
        seg = w_ref[:, src:src + n]
        if scale != 1.0:
            seg = seg * scale
        o_ref[:, dst:dst + n] = seg.astype(BF16)


def _relayout_w_in(w):
    depth, rows, cols = w.shape
    tr = W_PREP_ROWS
    return pl.pallas_call(
        _w_prep_kernel,
        grid=(depth, rows // tr),
        in_specs=[pl.BlockSpec((None, tr, cols), lambda l, i: (l, i, 0))],
        out_specs=pl.BlockSpec((None, tr, W_COLS), lambda l, i: (l, i, 0)),
        out_shape=jax.ShapeDtypeStruct((depth, rows, W_COLS), BF16),
        compiler_params=pltpu.CompilerParams(dimension_semantics=("arbitrary", "arbitrary")),
        name="w_in_prep",
    )(w)


def _fox_bias_base(h):
    return SUBLANES * (h // 2) + (FOX_HEAD_DIM if h % 2 == 0 else 0)


def _bias_selectors():
    n = N_PIECES * LANES
    selk = np.zeros((n, LANES), np.float32)
    selq = np.zeros((n, LANES), np.float32)
    selm = np.zeros((n, LANES), np.float32)
    onek = np.zeros((1, LANES), np.float32)
    oneq = np.zeros((1, LANES), np.float32)
    onem = np.zeros((1, LANES), np.float32)
    for h in range(FOX_HEADS):
        src = SUBLANES * h
        base = _fox_bias_base(h)
        for j in range(N_PIECES):
            selk[j * LANES + src, base + j] = -1.0
            selq[j * LANES + src, base + N_PIECES + j] = 1.0
            onek[0, base + N_PIECES + j] = 1.0
            oneq[0, base + j] = 1.0
    for h in range(ML_HEADS):
        base = SUBLANES * h
        for j in range(N_PIECES):
            selm[j * LANES + GATE_I + h, base + j] = 1.0
            onem[0, base + N_PIECES + j] = 1.0
    return (jnp.asarray(selk, BF16), jnp.asarray(selq, BF16), jnp.asarray(selm, BF16),
            jnp.asarray(onek), jnp.asarray(oneq), jnp.asarray(onem))


def _proj_kernel(x_ref, w_ref, cw_ref, cb_ref,
                 kc_ref, qct_ref, mvt_ref, ogt_ref, zgt_ref, ufx_ref, rqt_ref, rzg_ref, g_ref,
                 fqt_ref, fvt_ref, tail_sc):
    tm = PROJ_TM

    @pl.when(pl.program_id(1) == 0)
    def _():
        tail_sc[...] = jnp.zeros_like(tail_sc)

    xb = x_ref[...].astype(BF16)

    def mm(c0, n):
        return jnp.dot(xb, w_ref[:, c0:c0 + n], preferred_element_type=F32)

    u = mm(COL_QK, 2 * ML_PW)
    tail = tail_sc[...]
    w = cw_ref[...]
    row8 = lax.broadcasted_iota(jnp.int32, (SUBLANES, 2 * ML_PW), 0)
    y = u * w[CONV_WIDTH - 1:CONV_WIDTH, :] + cb_ref[...]
    for shift in range(1, CONV_WIDTH):
        r = pltpu.roll(u, shift, axis=0)
        rt = pltpu.roll(tail, shift, axis=0)
        first = jnp.where(row8 < shift, rt, r[:SUBLANES])
        xs = jnp.concatenate([first, r[SUBLANES:]], axis=0)
        y = y + xs * w[CONV_WIDTH - 1 - shift:CONV_WIDTH - shift, :]
    tail_sc[...] = u[tm - SUBLANES:, :]
    a = _silu(y)
    qct_ref[...] = a[:, :ML_PW].T.astype(BF16)
    kc_ref[...] = (a[:, ML_PW:] * (ML_HEAD_DIM ** -0.5)).astype(BF16)

    t = mm(COL_MV, 2 * ML_WIDTH)
    mvt_ref[...] = _with_ones_rows(t[:, :ML_WIDTH].T.astype(BF16), ML_HEAD_DIM, ML_HEADS)
    ogt_ref[...] = _sigmoid(t[:, ML_WIDTH:]).T.astype(BF16)

    t = _silu(mm(COL_MZ, ML_WIDTH + FOX_WIDTH))
    zgt_ref[...] = t[:, :ML_WIDTH].T.astype(BF16)
    ufx_ref[:, FOX_WIDTH:] = t[:, ML_WIDTH:].astype(BF16)

    t = mm(COL_FK, 2 * FOX_WIDTH)
    ufx_ref[:, :FOX_WIDTH] = t[:, :FOX_WIDTH].astype(BF16)
    fqt_ref[...] = t[:, FOX_WIDTH:].T.astype(BF16)

    t = mm(COL_FV, FOX_WIDTH + G_COLS)
    fvt_ref[...] = _with_ones_rows(t[:, :FOX_WIDTH].T.astype(BF16), FOX_HEAD_DIM, FOX_HEADS)
    g_ref[...] = t[:, FOX_WIDTH:]

    t = mm(COL_RQ, 2 * MEM_WIDTH)
    rqt_ref[...] = t[:, :MEM_WIDTH].T.astype(BF16)
    rzg_ref[...] = _silu(t[:, MEM_WIDTH:]).astype(BF16)


def _project(x, w_re, conv_w, conv_b, layer):
    b, s, _ = x.shape
    tm = PROJ_TM

    def row(n):
        return pl.BlockSpec((None, tm, n), lambda i, j: (i, j, 0))

    def col(n):
        return pl.BlockSpec((None, n, tm), lambda i, j: (i, 0, j))

    def per_layer(shape):
        return pl.BlockSpec((None,) + shape, lambda i, j: (layer, 0, 0))

    def rows_shape(n):
        return jax.ShapeDtypeStruct((b, s, n), BF16)

    def cols_shape(n):
        return jax.ShapeDtypeStruct((b, n, s), BF16)

    return pl.pallas_call(
        _proj_kernel,
        grid=(b, s // tm),
        in_specs=[row(D_MODEL), per_layer((D_MODEL, W_COLS)),
                  per_layer((CONV_WIDTH, 2 * ML_PW)), per_layer((1, 2 * ML_PW))],
        out_specs=[row(ML_PW), col(ML_PW), col(ML_HEADS * ML_VROWS), col(ML_WIDTH), col(ML_WIDTH),
                   row(2 * FOX_WIDTH), col(MEM_WIDTH), row(MEM_WIDTH), row(G_COLS),
                   col(FOX_WIDTH), col(FOX_HEADS * FOX_VROWS)],
        out_shape=[rows_shape(ML_PW), cols_shape(ML_PW), cols_shape(ML_HEADS * ML_VROWS),
                   cols_shape(ML_WIDTH), cols_shape(ML_WIDTH),
                   rows_shape(2 * FOX_WIDTH), cols_shape(MEM_WIDTH), rows_shape(MEM_WIDTH),
                   jax.ShapeDtypeStruct((b, s, G_COLS), F32),
                   cols_shape(FOX_WIDTH), cols_shape(FOX_HEADS * FOX_VROWS)],
        scratch_shapes=[pltpu.VMEM((SUBLANES, 2 * ML_PW), F32)],
        compiler_params=pltpu.CompilerParams(
            dimension_semantics=("arbitrary", "arbitrary"), vmem_limit_bytes=56 * 1024 * 1024),
        name="in_proj",
    )(x, w_re, conv_w, conv_b)


def _gates_kernel(g_ref, bias_ref, selk_ref, selq_ref, selm_ref, onek_ref, oneq_ref, onem_ref,
                  kb_ref, qbt_ref, maug_ref, mrow_ref, carry_ref):
    @pl.when(pl.program_id(0) == 0)
    def _():
        carry_ref[...] = jnp.zeros_like(carry_ref)

    L = ML_CHUNK
    r = lax.broadcasted_iota(jnp.int32, (L, L), 0)
    c = lax.broadcasted_iota(jnp.int32, (L, L), 1)
    tri = (c <= r).astype(BF16)
    lane = lax.broadcasted_iota(jnp.int32, (L, G_COLS), 1)
    pos = lax.broadcasted_iota(jnp.int32, (SUBLANES, L), 1)

    chains = [(bi, ci) for bi in range(g_ref.shape[0]) for ci in range(GATE_CHUNKS)]

    def span(ci):
        return slice(ci * L, (ci + 1) * L)

    xs, parts = [], []
    for bi, ci in chains:
        x = g_ref[bi, span(ci), :] + bias_ref[...]
        log_sig = jnp.minimum(x, 0.0) - jnp.log1p(jnp.exp(-jnp.abs(x)))
        xs.append(x)
        parts.append(jnp.dot(tri, jnp.concatenate(_split3(log_sig), axis=1),
                             preferred_element_type=F32))

    kbs, qbs, maugs, rowss = [], [], [], []
    carry = None
    for (bi, ci), x, p in zip(chains, xs, parts):
        if ci == 0:
            carry = carry_ref[bi, 0:1, :]
        local = (p[:, :LANES] + p[:, LANES:2 * LANES]) + p[:, 2 * LANES:]
        c_run = local + carry
        carry = carry + local[L - 1:L, :]
        if ci == GATE_CHUNKS - 1:
            carry_ref[bi, 0:1, :] = carry
        pieces = jnp.concatenate(_split3(c_run * LOG2E), axis=1)
        kbs.append(jnp.dot(pieces, selk_ref[...], preferred_element_type=F32))
        qbs.append(jnp.dot(pieces, selq_ref[...], preferred_element_type=F32))
        rc = x - pltpu.roll(local, LANES - (GATE_F - GATE_I), axis=1)
        pieces = jnp.concatenate(_split3(rc), axis=1)
        maugs.append(jnp.dot(pieces, selm_ref[...], preferred_element_type=F32))
        rowss.append(jnp.where(lane < GATE_F, rc, local))

    for (bi, ci), kb, qb, maug, rows in zip(chains, kbs, qbs, maugs, rowss):
        kb_ref[bi, span(ci), :] = (kb + onek_ref[...]).astype(BF16)
        qbt_ref[bi, :, span(ci)] = (qb + oneq_ref[...]).T.astype(BF16)
        maug_ref[bi, span(ci), :] = (maug + onem_ref[...]).astype(BF16)
        rows = rows.T[GATE_I:GATE_I + 2 * SUBLANES, :]
        cm = rows[:SUBLANES]
        shift = 1
        while shift < L:
            cm = jnp.maximum(cm, jnp.where(pos >= shift, pltpu.roll(cm, shift, axis=1), -jnp.inf))
            shift *= 2
        mrow_ref[bi, :, span(ci)] = jnp.concatenate([rows, cm], axis=0)


def _gates(g, bias_row, sels):
    b, s, _ = g.shape
    L = GATE_CHUNKS * ML_CHUNK
    const = lambda shape: pl.BlockSpec(shape, lambda j: (0, 0))
    sel_shape = (N_PIECES * LANES, LANES)
    return pl.pallas_call(
        _gates_kernel,
        grid=(s // L,),
        in_specs=[pl.BlockSpec((b, L, G_COLS), lambda j: (0, j, 0)),
                  const((1, G_COLS)),
                  const(sel_shape), const(sel_shape), const(sel_shape),
                  const((1, LANES)), const((1, LANES)), const((1, LANES))],
        out_specs=[pl.BlockSpec((b, L, LANES), lambda j: (0, j, 0)),
                   pl.BlockSpec((b, LANES, L), lambda j: (0, 0, j)),
                   pl.BlockSpec((b, L, LANES), lambda j: (0, j, 0)),
                   pl.BlockSpec((b, 3 * SUBLANES, L), lambda j: (0, 0, j))],
        out_shape=[jax.ShapeDtypeStruct((b, s, LANES), BF16),
                   jax.ShapeDtypeStruct((b, LANES, s), BF16),
                   jax.ShapeDtypeStruct((b, s, LANES), BF16),
                   jax.ShapeDtypeStruct((b, 3 * SUBLANES, s), F32)],
        scratch_shapes=[pltpu.VMEM((b, SUBLANES, G_COLS), F32)],
        compiler_params=pltpu.CompilerParams(dimension_semantics=("arbitrary",)),
        name="gate_prefix",
    )(g, bias_row, *sels)


def _fox_kernel(qt_ref, qbt_ref, k_ref, kb_ref, vt_ref, zg_ref, o_ref,
                qa_sc, kf_sc, sa_sc, sb_sc, mxa_sc, mxb_sc, m_sc, acc_sc):
    qi = pl.program_id(1)
    tq, tk = FOX_TQ, FOX_TK
    hd = FOX_HEAD_DIM

    brow = lax.broadcasted_iota(jnp.int32, (LANES, tq), 0)
    qbt = qbt_ref[...]
    for h in range(FOX_HEADS):
        base = _fox_bias_base(h)
        own = (brow >= base) & (brow < base + SUBLANES)
        bias = jnp.where(own, qbt, jnp.zeros_like(qbt))
        qh = qt_ref[h * hd:(h + 1) * hd, :]
        if h % 2 == 0:
            qa_sc[h] = jnp.concatenate([qh, bias[hd:, :]], axis=0)
        else:
            qa_sc[h] = jnp.concatenate([bias[:hd, :], qh], axis=0)

    @pl.when(qi == 0)
    def _():
        klane = lax.broadcasted_iota(jnp.int32, (kb_ref.shape[0], LANES), 1)
        kbias = kb_ref[...]
        for h in range(FOX_HEADS):
            p = h // 2
            own = (klane < hd) if h % 2 == 0 else (klane >= hd)
            kf_sc[:, h * LANES:(h + 1) * LANES] = jnp.where(
                own, k_ref[:, p * LANES:(p + 1) * LANES], kbias)

    m_sc[...] = jnp.full_like(m_sc, -jnp.inf)
    acc_sc[...] = jnp.zeros_like(acc_sc)

    def logits(j, h, dst, mxd, key_offset, q0=0):
        k0 = pl.multiple_of(j * tk, tk)
        kk = kf_sc[pl.ds(k0, tk), h * LANES:(h + 1) * LANES]
        st = jnp.dot(kk, qa_sc[h, :, q0:], preferred_element_type=F32)
        if key_offset is not None:
            kr = lax.broadcasted_iota(jnp.int32, st.shape, 0)
            qc = lax.broadcasted_iota(jnp.int32, st.shape, 1)
            st = jnp.where(kr + (key_offset - q0) <= qc, st, -jnp.inf)
        dst[h, :, q0:] = st
        mxd[h, :, q0:] = jnp.max(st, axis=0, keepdims=True)

    def softmax_pv(j, h, src, mxs, q0=0):
        k0 = pl.multiple_of(j * tk, tk)
        m_old = m_sc[h, :, q0:]
        m_new = jnp.maximum(m_old, mxs[h, :, q0:])
        alpha = jnp.exp2(m_old - m_new)
        pt = jnp.exp2(src[h, :, q0:] - m_new).astype(BF16)
        vt = vt_ref[h * FOX_VROWS:(h + 1) * FOX_VROWS, pl.ds(k0, tk)]
        acc_sc[h, :, q0:] = alpha * acc_sc[h, :, q0:] + jnp.dot(vt, pt, preferred_element_type=F32)
        m_sc[h, :, q0:] = m_new

    buf_a = (sa_sc, mxa_sc)
    buf_b = (sb_sc, mxb_sc)

    def stage(j_qk, dst, key_offset, j_sm, src, q0_qk=0, q0_sm=0):
        if j_qk is not None:
            logits(j_qk, 0, *dst, key_offset, q0_qk)
        for h in range(FOX_HEADS):
            if j_qk is not None and h + 1 < FOX_HEADS:
                logits(j_qk, h + 1, *dst, key_offset, q0_qk)
            if j_sm is not None:
                softmax_pv(j_sm, h, *src, q0_sm)

    stage(2 * qi + 1, buf_b, tk, None, None, q0_qk=tk)
    stage(2 * qi, buf_a, 0, 2 * qi + 1, buf_b, q0_sm=tk)
    stage(0, buf_b, None, 2 * qi, buf_a)

    def body(i, carry):
        stage(2 * i + 1, buf_a, None, 2 * i, buf_b)
        stage(2 * i + 2, buf_b, None, 2 * i + 1, buf_a)
        return carry

    lax.fori_loop(0, qi, body, 0)

    out_t = jnp.concatenate(
        [acc_sc[h, :hd, :] / acc_sc[h, hd:hd + 1, :] for h in range(FOX_HEADS)], axis=0)
    o_ref[...] = (out_t.T * zg_ref[...].astype(F32)).astype(BF16)


def _fox(qt, qbt, ufx, kb, vt):
    b, s, _ = ufx.shape
    tq = FOX_TQ
    return pl.pallas_call(
        _fox_kernel,
        grid=(b, s // tq),
        in_specs=[
            pl.BlockSpec((None, FOX_WIDTH, tq), lambda i, j: (i, 0, j)),
            pl.BlockSpec((None, LANES, tq), lambda i, j: (i, 0, j)),
            pl.BlockSpec((None, s, FOX_WIDTH), lambda i, j: (i, 0, 0)),
            pl.BlockSpec((None, s, LANES), lambda i, j: (i, 0, 0)),
            pl.BlockSpec((None, FOX_HEADS * FOX_VROWS, s), lambda i, j: (i, 0, 0)),
            pl.BlockSpec((None, tq, FOX_WIDTH), lambda i, j: (i, j, 1)),
        ],
        out_specs=pl.BlockSpec((None, tq, FOX_WIDTH), lambda i, j: (i, j, 0)),
        out_shape=jax.ShapeDtypeStruct((b, s, FOX_WIDTH), BF16),
        scratch_shapes=[pltpu.VMEM((FOX_HEADS, LANES, tq), BF16),
                        pltpu.VMEM((s, FOX_HEADS * LANES), BF16),
                        pltpu.VMEM((FOX_HEADS, FOX_TK, tq), F32),
                        pltpu.VMEM((FOX_HEADS, FOX_TK, tq), F32),
                        pltpu.VMEM((FOX_HEADS, 1, tq), F32),
                        pltpu.VMEM((FOX_HEADS, 1, tq), F32),
                        pltpu.VMEM((FOX_HEADS, 1, tq), F32),
                        pltpu.VMEM((FOX_HEADS, FOX_VROWS, tq), F32)],
        compiler_params=pltpu.CompilerParams(
            dimension_semantics=("arbitrary", "arbitrary"), vmem_limit_bytes=48 * 1024 * 1024),
        name="fox_attn",
    )(qt, qbt, ufx, kb, vt, ufx)


def _mlstm_kernel(kc_ref, qct_ref, mvt_ref, ogt_ref, zgt_ref, maug_ref, mrow_ref, ng_ref,
                  y_ref, c_sc, m_sc):
    L = ML_CHUNK
    dv = ML_HEAD_DIM

    @pl.when(pl.program_id(1) == 0)
    def _():
        c_sc[...] = jnp.zeros_like(c_sc)
        m_sc[...] = jnp.zeros_like(m_sc)

    kr = lax.broadcasted_iota(jnp.int32, (L, L), 0)
    qc = lax.broadcasted_iota(jnp.int32, (L, L), 1)
    visible = kr <= qc
    brow = lax.broadcasted_iota(jnp.int32, (ML_BIAS_ROWS, L), 0)
    bzero = jnp.zeros((LANES - ML_BIAS_ROWS, L), BF16)
    chains = [(bi, h) for bi in range(ML_NB) for h in range(ML_HEADS)]

    work = []
    for bi, h in chains:
        sc = bi * ML_HEADS + h
        r = mrow_ref[bi, h:h + 1, :]
        b = mrow_ref[bi, SUBLANES + h:SUBLANES + h + 1, :]
        cm = mrow_ref[bi, 2 * SUBLANES + h:2 * SUBLANES + h + 1, :]
        m_prev = m_sc[sc, 0:1, 0:1]
        g = jnp.maximum(m_prev, cm)
        g_last = g[:, L - 1:L]
        g_hi, g_mid, g_lo = (t.astype(F32) for t in _split3(g))
        rel = brow - SUBLANES * h
        blk = jnp.where(rel == N_PIECES, -g_hi,
                        jnp.where(rel == N_PIECES + 1, -g_mid,
                                  jnp.where(rel == N_PIECES + 2, -g_lo,
                                            jnp.where((rel >= 0) & (rel < N_PIECES), 1.0, 0.0))))
        bmat = jnp.concatenate([blk.astype(BF16), bzero], axis=0)
        k_h = kc_ref[bi, :, h * ML_PAD:(h + 1) * ML_PAD]
        q_h = qct_ref[bi, h * ML_PAD:(h + 1) * ML_PAD, :]
        ct = c_sc[sc]
        z = jnp.dot(maug_ref[bi], bmat, preferred_element_type=F32)
        s_raw = jnp.dot(k_h, q_h, preferred_element_type=F32)
        inter = jnp.dot(ct.astype(BF16), q_h, preferred_element_type=F32)
        work.append((r, b, g, g_last, m_prev, k_h, ct, z, s_raw, inter))

    outs = []
    for (bi, h), (r, b, g, g_last, m_prev, k_h, ct, z, s_raw, inter) in zip(chains, work):
        sc = bi * ML_HEADS + h
        d = jnp.exp(jnp.where(visible, z, -jnp.inf))
        p = (s_raw * d).astype(BF16)
        vt = mvt_ref[bi, h * ML_VROWS:(h + 1) * ML_VROWS, :]
        num = jnp.dot(vt, p, preferred_element_type=F32) + jnp.exp(m_prev - g) * inter
        den = num[dv:dv + 1, :]
        hv = num[:dv, :] / jnp.maximum(jnp.abs(den), jnp.exp(-(b + g)))

        wv = (vt.astype(F32) * jnp.exp(r - g_last)).astype(BF16)
        c_sc[sc] = jnp.exp(m_prev - g_last) * ct + jnp.dot(wv, k_h, preferred_element_type=F32)
        m_sc[sc] = jnp.broadcast_to(b[:, L - 1:L] + g_last, (SUBLANES, LANES))

        hv = hv * ogt_ref[bi, h * dv:(h + 1) * dv, :].astype(F32)
        mu = jnp.mean(hv, axis=0, keepdims=True)
        dd = hv - mu
        var = jnp.mean(dd * dd, axis=0, keepdims=True)
        hn = dd * lax.rsqrt(var + LN_EPS) * ng_ref[h * dv:(h + 1) * dv, :]
        outs.append(hn * zgt_ref[bi, h * dv:(h + 1) * dv, :].astype(F32))
        if h == ML_HEADS - 1:
            y_ref[bi] = jnp.concatenate(outs, axis=0).T.astype(BF16)
            outs = []


def _mlstm(kc, qct, mvt, ogt, zgt, maug, mrow, norm_g_b):
    b, s, _ = kc.shape
    L = ML_CHUNK

    def col(n):
        return pl.BlockSpec((ML_NB, n, L), lambda i, j: (i, 0, j))

    def row(n):
        return pl.BlockSpec((ML_NB, L, n), lambda i, j: (i, j, 0))

    return pl.pallas_call(
        _mlstm_kernel,
        grid=(b // ML_NB, s // L),
        in_specs=[row(ML_PW), col(ML_PW), col(ML_HEADS * ML_VROWS), col(ML_WIDTH), col(ML_WIDTH),
                  row(LANES), col(3 * SUBLANES),
                  pl.BlockSpec((ML_WIDTH, L), lambda i, j: (0, 0))],
        out_specs=row(ML_WIDTH),
        out_shape=jax.ShapeDtypeStruct((b, s, ML_WIDTH), BF16),
        scratch_shapes=[pltpu.VMEM((ML_NB * ML_HEADS, ML_VROWS, LANES), F32),
                        pltpu.VMEM((ML_NB * ML_HEADS, SUBLANES, LANES), F32)],
        compiler_params=pltpu.CompilerParams(
            dimension_semantics=("arbitrary", "arbitrary"), vmem_limit_bytes=48 * 1024 * 1024),
        name="mlstm",
    )(kc, qct, mvt, ogt, zgt, maug, mrow, norm_g_b)


def _memkv_kernel(m_ref, w_ref, k4_ref, vt_ref):
    w = w_ref[...].astype(BF16)
    for bi in range(m_ref.shape[0]):
        kv = jnp.dot(m_ref[bi].astype(BF16), w, preferred_element_type=F32)
        k = kv[:, :MEM_WIDTH] * (MEM_HEAD_DIM ** -0.5)
        lane = lax.broadcasted_iota(jnp.int32, k.shape, 1)
        for h in range(MEM_HEADS):
            own = (lane >= MEM_HEAD_DIM * h) & (lane < MEM_HEAD_DIM * (h + 1))
            k4_ref[bi, h] = jnp.where(own, k, 0.0).astype(BF16)
        vt_ref[bi] = _with_ones_rows(kv[:, MEM_WIDTH:].T.astype(BF16), MEM_HEAD_DIM, MEM_HEADS)


def _memkv(mem, w_mem_kv):
    b, ml, _ = mem.shape
    return pl.pallas_call(
        _memkv_kernel,
        grid=(DEPTH,),
        in_specs=[pl.BlockSpec((b, ml, D_MODEL), lambda l: (0, 0, 0)),
                  pl.BlockSpec((None, D_MODEL, 2 * MEM_WIDTH), lambda l: (l, 0, 0))],
        out_specs=[pl.BlockSpec((None, b, MEM_HEADS, ml, MEM_WIDTH), lambda l: (l, 0, 0, 0, 0)),
                   pl.BlockSpec((None, b, MEM_HEADS * MEM_VROWS, ml), lambda l: (l, 0, 0, 0))],
        out_shape=[jax.ShapeDtypeStruct((DEPTH, b, MEM_HEADS, ml, MEM_WIDTH), BF16),
                   jax.ShapeDtypeStruct((DEPTH, b, MEM_HEADS * MEM_VROWS, ml), BF16)],
        compiler_params=pltpu.CompilerParams(dimension_semantics=("arbitrary",)),
        name="mem_kv",
    )(mem, w_mem_kv)


def _out_kernel(x_ref, yf_ref, ym_ref, rqt_ref, rzg_ref, k4_ref, vt_ref, wo_ref, g_ref, b_ref,
                o_ref, ob_ref):
    hd = MEM_HEAD_DIM
    o_mem = FOX_WIDTH + ML_WIDTH
    n_sub = OUT_TM // OUT_SUB
    ahead = 2

    def rows(q):
        return slice(q * OUT_SUB, (q + 1) * OUT_SUB)

    def main_dot(q):
        y_main = jnp.concatenate([yf_ref[rows(q), :], ym_ref[rows(q), :]], axis=1)
        return jnp.dot(y_main, wo_ref[0:o_mem, :], preferred_element_type=F32)

    rqt = rqt_ref[...]
    sts = [jnp.dot(k4_ref[h], rqt, preferred_element_type=F32) for h in range(MEM_HEADS)]
    mains = [main_dot(q) for q in range(ahead)]

    outs = []
    for h in range(MEM_HEADS):
        st = sts[h]
        p = jnp.exp(st - jnp.max(st, axis=0, keepdims=True)).astype(BF16)
        acc = jnp.dot(vt_ref[h * MEM_VROWS:(h + 1) * MEM_VROWS, :], p,
                      preferred_element_type=F32)
        outs.append(acc[:hd, :] / acc[hd:hd + 1, :])
    att = jnp.concatenate(outs, axis=0).T
    y_mem = (att * rzg_ref[...].astype(F32)).astype(BF16)

    for q in range(n_sub):
        if q + ahead < n_sub:
            mains.append(main_dot(q + ahead))
        y = mains[q] + jnp.dot(y_mem[rows(q), :], wo_ref[o_mem:, :], preferred_element_type=F32)
        r = DEEPNORM_ALPHA * x_ref[rows(q), :] + y
        mu = jnp.mean(r, axis=1, keepdims=True)
        d = r - mu
        var = jnp.mean(d * d, axis=1, keepdims=True)
        out = d * lax.rsqrt(var + LN_EPS) * g_ref[...] + b_ref[...]
        o_ref[rows(q), :] = out
        ob_ref[rows(q), :] = out.astype(BF16)


def _out_proj(x, y_fox, y_ml, rqt, rzg, mem_k4, mem_vt, layer, w_out, ln_g, ln_b):
    b, s, _ = x.shape
    tm = OUT_TM
    ml = mem_k4.shape[3]
    return pl.pallas_call(
        _out_kernel,
        grid=(b, s // tm),
        in_specs=[
            pl.BlockSpec((None, tm, D_MODEL), lambda i, j: (i, j, 0)),
            pl.BlockSpec((None, tm, FOX_WIDTH), lambda i, j: (i, j, 0)),
            pl.BlockSpec((None, tm, ML_WIDTH), lambda i, j: (i, j, 0)),
            pl.BlockSpec((None, MEM_WIDTH, tm), lambda i, j: (i, 0, j)),
            pl.BlockSpec((None, tm, MEM_WIDTH), lambda i, j: (i, j, 0)),
            pl.BlockSpec((None, None, MEM_HEADS, ml, MEM_WIDTH), lambda i, j: (layer, i, 0, 0, 0)),
            pl.BlockSpec((None, None, MEM_HEADS * MEM_VROWS, ml), lambda i, j: (layer, i, 0, 0)),
            pl.BlockSpec((None, D_MODEL, D_MODEL), lambda i, j: (layer, 0, 0)),
            pl.BlockSpec((None, 1, D_MODEL), lambda i, j: (layer, 0, 0)),
            pl.BlockSpec((None, 1, D_MODEL), lambda i, j: (layer, 0, 0)),
        ],
        out_specs=[pl.BlockSpec((None, tm, D_MODEL), lambda i, j: (i, j, 0)),
                   pl.BlockSpec((None, tm, D_MODEL), lambda i, j: (i, j, 0))],
        out_shape=[jax.ShapeDtypeStruct((b, s, D_MODEL), F32),
                   jax.ShapeDtypeStruct((b, s, D_MODEL), BF16)],
        compiler_params=pltpu.CompilerParams(
            dimension_semantics=("arbitrary", "arbitrary"), vmem_limit_bytes=48 * 1024 * 1024),
        name="out_proj_ln",
    )(x, y_fox, y_ml, rqt, rzg, mem_k4, mem_vt, w_out, ln_g, ln_b)


def kernel(x, mem, w_in, fox_f_bias, mlstm_conv_w, mlstm_conv_b, mlstm_i_bias, mlstm_f_bias,
           mlstm_norm_g, w_mem_kv, w_out, ln_g, ln_b):
    w_re = _relayout_w_in(w_in)
    w_out_b = w_out.astype(BF16)
    bias_rows = _spread_gates(fox_f_bias, mlstm_i_bias, mlstm_f_bias)[:, None, :]
    conv_w = jnp.concatenate([_pad_heads_last(mlstm_conv_w[..., :ML_WIDTH]),
                              _pad_heads_last(mlstm_conv_w[..., ML_WIDTH:])], axis=-1)
    conv_b = jnp.concatenate([_pad_heads_last(mlstm_conv_b[..., :ML_WIDTH]),
                              _pad_heads_last(mlstm_conv_b[..., ML_WIDTH:])], axis=-1)[:, None, :]
    norm_g_b = jnp.broadcast_to(mlstm_norm_g[:, :, None], (DEPTH, ML_WIDTH, ML_CHUNK))
    ln_g3 = ln_g[:, None, :]
    ln_b3 = ln_b[:, None, :]
    sels = _bias_selectors()

    mem_k4, mem_vt = _memkv(mem, w_mem_kv)
    x_mxu = x
    for l in range(DEPTH):
        kc, qct, mvt, ogt, zgt, ufx, rqt, rzg, g, fqt, fvt = _project(x_mxu, w_re, conv_w, conv_b, l)
        kb, qbt, maug, mrow = _gates(g, bias_rows[l], sels)
        y_fox = _fox(fqt, qbt, ufx, kb, fvt)
        y_ml = _mlstm(kc, qct, mvt, ogt, zgt, maug, mrow, norm_g_b[l])
        x, x_mxu = _out_proj(x, y_fox, y_ml, rqt, rzg, mem_k4, mem_vt, l, w_out_b, ln_g3, ln_b3)
    return x
```
